```python
import math
import jax, jax.numpy as jnp
from jax import lax
import numpy as np

D_MODEL = 1024
BATCH = 4
SEQ = 8192
DEPTH = 2

CHUNK = 64
MEM_LEN = 256
POOL_GROUPS = 4
POOL_WINDOWS = (2, 4, 8, 16)
POOL_WIDTH = D_MODEL // 2
POOL_GC = POOL_WIDTH // POOL_GROUPS
N_HEADS = 8
HEAD_DIM = 64
ATTN_WIDTH = N_HEADS * HEAD_DIM
IDX_HEADS = 8
IDX_DIM = 64
TOPK_MAX = 256
Q_BLOCK = 128
ATTN_SCALE = HEAD_DIM ** -0.5
IDX_W_SCALE = (IDX_DIM ** -0.5) * (IDX_HEADS ** -0.5)
MEM_HEADS = 4
MEM_HEAD_DIM = 128
MEM_WIDTH = MEM_HEADS * MEM_HEAD_DIM
MEM_SCALE = MEM_HEAD_DIM ** -0.5
N_BRANCH = 3
REL_BUCKETS = 32
REL_MAX_DIST = 1024
EPS = 1e-6
NEG = -1e30
IN_SPLITS = (POOL_WIDTH, POOL_WIDTH,
             ATTN_WIDTH, ATTN_WIDTH, ATTN_WIDTH, ATTN_WIDTH,
             IDX_HEADS * IDX_DIM, IDX_DIM, IDX_HEADS,
             MEM_WIDTH, MEM_WIDTH,
             N_BRANCH * D_MODEL)
IN_COLS = sum(IN_SPLITS)

kernel_name = "hybrid_pool_dsa_mem_gated_trunk"


def rms_norm(x, g):
    x32 = x.astype(jnp.float32)
    y = x32 * lax.rsqrt(jnp.mean(x32 * x32, axis=-1, keepdims=True) + EPS)
    return (y * g.astype(jnp.float32)).astype(x.dtype)


def rel_bucket(rel):
    half = REL_BUCKETS // 2
    max_exact = half // 2
    ret = jnp.where(rel > 0, half, 0)
    n = jnp.abs(rel)
    nf = jnp.maximum(n, 1).astype(jnp.float32)
    large = max_exact + (jnp.log(nf / max_exact) / math.log(REL_MAX_DIST / max_exact)
                         * (half - max_exact)).astype(jnp.int32)
    large = jnp.minimum(large, half - 1)
    return ret + jnp.where(n < max_exact, n, large)


def multi_scale_pool(u, pool_w, pool_scale):
    b, s, _ = u.shape
    ug = u.reshape(b, s, POOL_GROUPS, POOL_GC).astype(jnp.float32)
    cs = jnp.cumsum(ug, axis=1)
    cs = jnp.concatenate([jnp.zeros_like(cs[:, :1]), cs], axis=1)
    t = jnp.arange(s, dtype=jnp.int32)[:, None]
    win = jnp.asarray(POOL_WINDOWS, jnp.int32)[None, :]
    start = jnp.maximum(t + 1 - win, 0)
    grp = jnp.arange(POOL_GROUPS, dtype=jnp.int32)[None, :]
    lo = cs[:, start, grp, :]
    count = (t + 1 - start).astype(jnp.float32)[None, :, :, None]
    pooled = (cs[:, 1:] - lo) / count - ug
    mixed = jnp.einsum('bsgc,gcd->bsgd', pooled.astype(u.dtype), pool_w)
    return mixed.reshape(b, s, POOL_WIDTH) * pool_scale


def dsa_attention(q, k, v, iq, ik, iw, rel_bias):
    b, s = q.shape[:2]
    topk = min(TOPK_MAX, s // 4)
    nb = s // Q_BLOCK
    key_pos = jnp.arange(s, dtype=jnp.int32)
    ik32 = ik.astype(jnp.float32)
    gather = jax.vmap(lambda arr, idx: arr[idx])

    def to_blocks(a):
        return jnp.moveaxis(a.reshape((b, nb, Q_BLOCK) + a.shape[2:]), 1, 0)

    def block(args):
        qb, iqb, iwb, qpos = args
        limit = (qpos // CHUNK + 1) * CHUNK
        sc = jnp.einsum('bqhd,bsd->bqhs', iqb.astype(jnp.float32), ik32)
        index_score = jnp.einsum('bqhs,bqh->bqs', jax.nn.relu(sc),
                                 iwb.astype(jnp.float32) * IDX_W_SCALE)
        admissible = key_pos[None, :] < limit[:, None]
        index_score = jnp.where(admissible[None], index_score, NEG)
        _, idx = lax.top_k(index_score, topk)
        k_sel = gather(k, idx)
        v_sel = gather(v, idx)
        logits = jnp.einsum('bqhd,bqkhd->bqhk', qb, k_sel).astype(jnp.float32) * ATTN_SCALE
        bias = rel_bias[rel_bucket(idx - qpos[None, :, None])]
        logits = logits + jnp.moveaxis(bias, -1, 2).astype(jnp.float32)
        valid = idx < limit[None, :, None]
        logits = jnp.where(valid[:, :, None, :], logits, NEG)
        p = jax.nn.softmax(logits, axis=-1).astype(v.dtype)
        return jnp.einsum('bqhk,bqkhd->bqhd', p, v_sel)

    out = lax.map(block, (to_blocks(q), to_blocks(iq), to_blocks(iw),
                          key_pos.reshape(nb, Q_BLOCK)))
    return jnp.moveaxis(out, 0, 1).reshape(b, s, N_HEADS * HEAD_DIM)


def memory_attention(q, mem_k, mem_v):
    b, s = q.shape[:2]
    logits = jnp.einsum('bshd,bmhd->bhsm', q, mem_k).astype(jnp.float32) * MEM_SCALE
    p = jax.nn.softmax(logits, axis=-1).astype(mem_v.dtype)
    return jnp.einsum('bhsm,bmhd->bshd', p, mem_v).reshape(b, s, MEM_WIDTH)


def setup_inputs(seed: int = 0) -> dict:
    key = jax.random.key(seed)
    ks = jax.random.split(key, 14)
    f32 = jnp.float32
    nrm = lambda k, shape, scale: jax.random.normal(k, shape, f32) * scale
    return {
        "x": nrm(ks[0], (BATCH, SEQ, D_MODEL), 1.0),
        "mem": nrm(ks[1], (BATCH, MEM_LEN, D_MODEL), 1.0),
        "norm_g": 1.0 + nrm(ks[2], (DEPTH, D_MODEL), 0.02),
        "w_in": nrm(ks[3], (DEPTH, D_MODEL, IN_COLS), D_MODEL ** -0.5),
        "pool_w": nrm(ks[4], (DEPTH, POOL_GROUPS, POOL_GC, POOL_GC), POOL_GC ** -0.5),
        "pool_scale": 1.0 + nrm(ks[5], (DEPTH, POOL_WIDTH), 0.02),
        "mem_norm_g": 1.0 + nrm(ks[6], (DEPTH, D_MODEL), 0.02),
        "w_mem_kv": nrm(ks[7], (DEPTH, D_MODEL, 2 * MEM_WIDTH), D_MODEL ** -0.5),
        "w_branch": nrm(ks[8], (DEPTH, N_BRANCH, POOL_WIDTH, D_MODEL), POOL_WIDTH ** -0.5),
        "w_out": nrm(ks[9], (DEPTH, D_MODEL, D_MODEL), D_MODEL ** -0.5),
        "rel_bias": nrm(ks[10], (REL_BUCKETS, N_HEADS), 0.5),
        "final_g": 1.0 + nrm(ks[11], (D_MODEL,), 0.02),
    }


def reference(x, mem, norm_g, w_in, pool_w, pool_scale, mem_norm_g, w_mem_kv,
              w_branch, w_out, rel_bias, final_g):
    b, s, _ = x.shape
    split_points = np.cumsum(np.asarray(IN_SPLITS))[:-1].tolist()
    for l in range(DEPTH):
        h = rms_norm(x, norm_g[l])
        proj = h @ w_in[l]
        (pool_u, pool_z, aq, ak, av, az, iq, ik, iw, mq, mz, gates) = jnp.split(
            proj, split_points, axis=-1)

        y_pool = multi_scale_pool(pool_u, pool_w[l], pool_scale[l]) * jax.nn.silu(pool_z)

        y_attn = dsa_attention(
            aq.reshape(b, s, N_HEADS, HEAD_DIM), ak.reshape(b, s, N_HEADS, HEAD_DIM),
            av.reshape(b, s, N_HEADS, HEAD_DIM), iq.reshape(b, s, IDX_HEADS, IDX_DIM),
            ik, iw, rel_bias) * jax.nn.silu(az)

        mkv = rms_norm(mem, mem_norm_g[l]) @ w_mem_kv[l]
        mk, mv = jnp.split(mkv, 2, axis=-1)
        mm = mem.shape[1]
        y_mem = memory_attention(
            mq.reshape(b, s, MEM_HEADS, MEM_HEAD_DIM),
            mk.reshape(b, mm, MEM_HEADS, MEM_HEAD_DIM),
            mv.reshape(b, mm, MEM_HEADS, MEM_HEAD_DIM)) * jax.nn.silu(mz)

        g = jax.nn.sigmoid(gates).reshape(b, s, N_BRANCH, D_MODEL)
        merged = (g[:, :, 0] * (y_pool @ w_branch[l, 0])
                  + g[:, :, 1] * (y_attn @ w_branch[l, 1])
                  + g[:, :, 2] * (y_mem @ w_branch[l, 2]))
        x = x + merged @ w_out[l]
    return rms_norm(x, final_g)
```

```python
import functools
import math

import numpy as np
import jax
import jax.numpy as jnp
from jax import lax
from jax.experimental import pallas as pl
from jax.experimental.pallas import tpu as pltpu

F32 = jnp.float32
BF16 = jnp.bfloat16
I32 = jnp.int32

D_MODEL = 1024
CHUNK = 64
MEM_LEN = 256
POOL_GROUPS = 4
POOL_WINDOWS = (2, 4, 8, 16)
POOL_WIDTH = D_MODEL // 2
POOL_GC = POOL_WIDTH // POOL_GROUPS
N_HEADS = 8
HEAD_DIM = 64
ATTN_WIDTH = N_HEADS * HEAD_DIM
IDX_HEADS = 8
IDX_DIM = 64
TOPK = 256
ATTN_SCALE = HEAD_DIM ** -0.5
IDX_W_SCALE = (IDX_DIM ** -0.5) * (IDX_HEADS ** -0.5)
MEM_HEADS = 4
MEM_HEAD_DIM = 128
MEM_WIDTH = MEM_HEADS * MEM_HEAD_DIM
MEM_SCALE = MEM_HEAD_DIM ** -0.5
N_BRANCH = 3
REL_BUCKETS = 32
REL_MAX_DIST = 1024
EPS = 1e-6
NEG = -1e30
HALF_NEG = -5e29

IN_SPLITS = (POOL_WIDTH, POOL_WIDTH, ATTN_WIDTH, ATTN_WIDTH, ATTN_WIDTH, ATTN_WIDTH,
             IDX_HEADS * IDX_DIM, IDX_DIM, IDX_HEADS, MEM_WIDTH, MEM_WIDTH, N_BRANCH * D_MODEL)
_OFF = np.concatenate([[0], np.cumsum(IN_SPLITS)]).tolist()
(O_PU, O_PZ, O_AQ, O_AK, O_AV, O_AZ, O_IQ, O_IK, O_IW, O_MQ, O_MZ, O_G, O_END) = _OFF

LANES = 128
SUBLANES = 8
TQ = 256
KT = 256
TM = 256
HALO = 16
N_NEAR = 4
COUNT_ROWS = 256
WT_ROWS = 528
VMEM_LIMIT = 56 * 1024 * 1024

_NT_DIMS = (((1,), (1,)), ((), ()))


def _rms_norm_f32(x, g):
    ms = jnp.mean(x * x, axis=-1, keepdims=True)
    return x * lax.rsqrt(ms + EPS) * g


def _proj_kernel(x_ref, g_ref, wf_ref, wb_ref, wt_ref,
                 pu_ref, pz_ref, az_ref, mz_ref, gates_ref,
                 aq_ref, ak_ref, iq_ref, mq_ref, ik2_ref, avt_ref, iwt_ref):
    h = _rms_norm_f32(x_ref[0], g_ref[...]).astype(BF16)

    def mm(w_ref, lo, hi):
        return jnp.dot(h, w_ref[:, lo:hi], preferred_element_type=F32)

    pu_ref[0] = mm(wf_ref, 0, 512)
    pz_ref[0] = mm(wf_ref, 512, 1024)
    az_ref[0] = mm(wf_ref, 1024, 1536)
    mz_ref[0] = mm(wf_ref, 1536, 2048)
    for c in range(N_BRANCH * D_MODEL // 512):
        gates_ref[0, :, 512 * c:512 * (c + 1)] = mm(wf_ref, 2048 + 512 * c, 2048 + 512 * (c + 1))
    aq_ref[0] = mm(wb_ref, 0, 512).astype(BF16)
    ak_ref[0] = mm(wb_ref, 512, 1024).astype(BF16)
    iq_ref[0] = mm(wb_ref, 1024, 1536).astype(BF16)
    mq_ref[0] = mm(wb_ref, 1536, 2048).astype(BF16)
    ik2_ref[0] = mm(wb_ref, 2048, 2176).astype(BF16)
    t = lax.dot_general(wt_ref[...], h, _NT_DIMS, preferred_element_type=F32)
    avt_ref[0, 0] = t[0:ATTN_WIDTH].astype(BF16)
    iwt_ref[0] = t[ATTN_WIDTH:ATTN_WIDTH + IDX_HEADS] * IDX_W_SCALE


def _project(x, g, wf, wb, wt):
    b, s, d = x.shape
    nt = s // TM
    row = lambda w: pl.BlockSpec((1, TM, w), lambda bi, i: (bi, i, 0))
    full = lambda a: pl.BlockSpec(a.shape, lambda bi, i: (0,) * a.ndim)
    f32o = lambda w: jax.ShapeDtypeStruct((b, s, w), F32)
    b16o = lambda w: jax.ShapeDtypeStruct((b, s, w), BF16)
    out_shape = (f32o(512), f32o(512), f32o(512), f32o(512), f32o(N_BRANCH * D_MODEL),
                 b16o(512), b16o(512), b16o(512), b16o(512), b16o(128),
                 jax.ShapeDtypeStruct((b, nt, ATTN_WIDTH, TM), BF16),
                 jax.ShapeDtypeStruct((b, IDX_HEADS, s), F32))
    out_specs = (row(512), row(512), row(512), row(512), row(N_BRANCH * D_MODEL),
                 row(512), row(512), row(512), row(512), row(128),
                 pl.BlockSpec((1, 1, ATTN_WIDTH, TM), lambda bi, i: (bi, i, 0, 0)),
                 pl.BlockSpec((1, IDX_HEADS, TM), lambda bi, i: (bi, 0, i)))
    return pl.pallas_call(
        _proj_kernel,
        grid=(b, nt),
        in_specs=[row(d), full(g), full(wf), full(wb), full(wt)],
        out_specs=out_specs,
        out_shape=out_shape,
        compiler_params=pltpu.CompilerParams(
            dimension_semantics=("arbitrary", "arbitrary"), vmem_limit_bytes=VMEM_LIMIT),
        name="proj",
    )(x, g, wf, wb, wt)


def _memkv_kernel(mem_ref, g_ref, w_ref, mk_ref, mv_ref):
    h = _rms_norm_f32(mem_ref[0], g_ref[...]).astype(BF16)
    mk_ref[0] = jnp.dot(h, w_ref[:, 0:MEM_WIDTH], preferred_element_type=F32).astype(BF16)
    mv_ref[0] = jnp.dot(h, w_ref[:, MEM_WIDTH:2 * MEM_WIDTH], preferred_element_type=F32).astype(BF16)


def _mem_kv(mem, g, w):
    b, m, d = mem.shape
    o = jax.ShapeDtypeStruct((b, m, MEM_WIDTH), BF16)
    return pl.pallas_call(
        _memkv_kernel,
        grid=(b,),
        in_specs=[pl.BlockSpec((1, m, d), lambda bi: (bi, 0, 0)),
                  pl.BlockSpec(g.shape, lambda bi: (0, 0)),
                  pl.BlockSpec(w.shape, lambda bi: (0, 0))],
        out_specs=(pl.BlockSpec((1, m, MEM_WIDTH), lambda bi: (bi, 0, 0)),) * 2,
        out_shape=(o, o),
        compiler_params=pltpu.CompilerParams(
            dimension_semantics=("arbitrary",), vmem_limit_bytes=VMEM_LIMIT),
        name="memkv",
    )(mem, g, w)


def _bucket_table():
    half = REL_BUCKETS // 2
    max_exact = half // 2
    delta = np.arange(N_NEAR, dtype=np.int64)[:, None, None]
    key = np.arange(KT, dtype=np.int64)[None, :, None]
    qry = np.arange(TQ, dtype=np.int64)[None, None, :]
    rel = key - qry - delta * KT
    ret = np.where(rel > 0, half, 0)
    n = np.abs(rel)
    nf = np.maximum(n, 1).astype(np.float32)
    large = max_exact + (np.log(nf / np.float32(max_exact))
                         / np.float32(math.log(REL_MAX_DIST / max_exact))
                         * np.float32(half - max_exact)).astype(np.int32)
    large = np.minimum(large, half - 1)
    return (ret + np.where(n < max_exact, n, large)).astype(np.int32)


FAR_BUCKET = REL_BUCKETS // 2 - 1


def _bias_kernel(rb_ref, bk_ref, o_ref):
    h = pl.program_id(1)
    bk = bk_ref[0]
    far = rb_ref[FAR_BUCKET, h]
    acc = jnp.zeros((KT, TQ), F32)
    for bkt in range(REL_BUCKETS):
        acc = jnp.where(bk == bkt, rb_ref[bkt, h] - far, acc)
    o_ref[0, 0] = acc


def _bias_tiles(rel_bias):
    table = jnp.asarray(_bucket_table())
    return pl.pallas_call(
        _bias_kernel,
        grid=(N_NEAR, N_HEADS),
        in_specs=[pl.BlockSpec(memory_space=pltpu.SMEM),
                  pl.BlockSpec((1, KT, TQ), lambda dl, h: (dl, 0, 0))],
        out_specs=pl.BlockSpec((1, 1, KT, TQ), lambda dl, h: (dl, h, 0, 0)),
        out_shape=jax.ShapeDtypeStruct((N_NEAR, N_HEADS, KT, TQ), F32),
        compiler_params=pltpu.CompilerParams(dimension_semantics=("arbitrary", "arbitrary")),
        name="bias_tiles",
    )(rel_bias, table)


def _dsa_kernel(q_ref, iq_ref, iwt_ref, az_ref, ik2_ref, k_ref, vt_ref, bias_ref, o_ref,
                sc_ref, qw_ref, iqw_ref, acc_ref, m_ref, l_ref, thr_ref):
    i = pl.program_id(1)
    n_tiles = i + 1

    lane = lax.broadcasted_iota(I32, (TQ, LANES), 1)
    for h in range(N_HEADS):
        j = h // 2
        sel = (lane >= HEAD_DIM) if (h % 2) else (lane < HEAD_DIM)
        qw_ref[h] = jnp.where(sel, q_ref[0, :, LANES * j:LANES * (j + 1)].astype(F32), 0.0).astype(BF16)
        iqw_ref[h] = jnp.where(sel, iq_ref[0, :, LANES * j:LANES * (j + 1)].astype(F32), 0.0).astype(BF16)

    key_l = lax.broadcasted_iota(I32, (KT, TQ), 0)
    qry_l = lax.broadcasted_iota(I32, (KT, TQ), 1)
    adm_diag = key_l < ((qry_l // CHUNK) + 1) * CHUNK

    def score_tile(t, carry):
        r0 = pl.multiple_of(t * KT, KT)
        ikt = ik2_ref[0, pl.ds(r0, KT), :]
        acc = jnp.zeros((KT, TQ), F32)
        for h in range(IDX_HEADS):
            y = lax.dot_general(ikt, iqw_ref[h], _NT_DIMS, preferred_element_type=F32)
            acc = acc + jnp.maximum(y, 0.0) * iwt_ref[0, h:h + 1, :]
        adm = jnp.logical_or(adm_diag, t < i)
        sc_ref[pl.ds(r0, KT), :] = jnp.where(adm, acc, NEG)
        return carry

    lax.fori_loop(0, n_tiles, score_tile, 0)

    def count(pred):
        def body(c, acc):
            r0 = pl.multiple_of(c * COUNT_ROWS, COUNT_ROWS)
            x = sc_ref[pl.ds(r0, COUNT_ROWS), :]
            ind = jnp.where(pred(x, r0), F32(1.0), F32(0.0))
            return acc + ind.reshape(COUNT_ROWS // SUBLANES, SUBLANES, TQ).sum(axis=0)
        acc = lax.fori_loop(0, n_tiles * (KT // COUNT_ROWS), body, jnp.zeros((SUBLANES, TQ), F32))
        return acc.sum(axis=0, keepdims=True)

    def to_float(u):
        bits = jnp.where(u < 0, u ^ I32(-2 ** 31), ~u)
        return lax.bitcast_convert_type(bits, F32)

    @pl.when(i == 0)
    def _():
        thr_ref[...] = jnp.full((1, TQ), HALF_NEG, F32)

    @pl.when(i > 0)
    def _():
        def bit_body(bi, u):
            cand = u | lax.shift_left(I32(1), I32(31) - bi)
            cf = to_float(cand)
            cnt = count(lambda x, r0: x >= cf)
            return jnp.where(cnt >= float(TOPK), cand, u)

        u = lax.fori_loop(0, 32, bit_body, jnp.zeros((1, TQ), I32))
        thr = to_float(u)
        thr_ref[...] = thr
        cnt_ge = count(lambda x, r0: x >= thr)

        @pl.when(jnp.max(cnt_ge) > float(TOPK))
        def _():
            need = float(TOPK) - count(lambda x, r0: x > thr)
            row = lax.broadcasted_iota(I32, (COUNT_ROWS, TQ), 0)
            n_bits = max(1, (sc_ref.shape[0] - 1).bit_length())

            def idx_body(bi, p):
                cand = p | lax.shift_left(I32(1), I32(n_bits - 1) - bi)
                cnt = count(lambda x, r0: jnp.logical_and(x == thr, row + r0 < cand))
                return jnp.where(cnt < need, cand, p)

            last = lax.fori_loop(0, n_bits, idx_body, jnp.zeros((1, TQ), I32))

            def drop(c, carry):
                r0 = pl.multiple_of(c * COUNT_ROWS, COUNT_ROWS)
                x = sc_ref[pl.ds(r0, COUNT_ROWS), :]
                gone = jnp.logical_and(x == thr, row + r0 > last)
                sc_ref[pl.ds(r0, COUNT_ROWS), :] = jnp.where(gone, NEG, x)
                return carry

            lax.fori_loop(0, n_tiles * (KT // COUNT_ROWS), drop, 0)

    m_ref[...] = jnp.full(m_ref.shape, NEG, F32)
    l_ref[...] = jnp.zeros(l_ref.shape, F32)
    acc_ref[...] = jnp.zeros(acc_ref.shape, F32)
    thr = thr_ref[...]

    def attn_tile(t, near):
        r0 = pl.multiple_of(t * KT, KT)
        mask = sc_ref[pl.ds(r0, KT), :] >= thr
        for h in range(N_HEADS):
            j = h // 2
            kk = k_ref[0, pl.ds(r0, KT), LANES * j:LANES * (j + 1)]
            s = lax.dot_general(kk, qw_ref[h], _NT_DIMS, preferred_element_type=F32)
            if near is not None:
                s = s + bias_ref[near, h]
            s = jnp.where(mask, s, NEG)
            m_old = m_ref[h:h + 1, :]
            m_new = jnp.maximum(m_old, s.max(axis=0, keepdims=True))
            alpha = jnp.exp(m_old - m_new)
            p = jnp.exp(s - m_new)
            l_ref[h:h + 1, :] = alpha * l_ref[h:h + 1, :] + p.sum(axis=0, keepdims=True)
            pv = jnp.dot(vt_ref[0, t, HEAD_DIM * h:HEAD_DIM * (h + 1), :], p.astype(BF16),
                         preferred_element_type=F32)
            acc_ref[HEAD_DIM * h:HEAD_DIM * (h + 1), :] = (
                alpha * acc_ref[HEAD_DIM * h:HEAD_DIM * (h + 1), :] + pv)
            m_ref[h:h + 1, :] = m_new

    def far_tile(t, carry):
        attn_tile(t, None)
        return carry

    lax.fori_loop(0, jnp.maximum(n_tiles - N_NEAR, 0), far_tile, 0)
    for near in range(N_NEAR - 1, -1, -1):
        @pl.when(i >= near)
        def _(near=near):
            attn_tile(i - near, near)

    inv_l = 1.0 / l_ref[...]
    for h in range(N_HEADS):
        acc_ref[HEAD_DIM * h:HEAD_DIM * (h + 1), :] = (
            acc_ref[HEAD_DIM * h:HEAD_DIM * (h + 1), :] * inv_l[h:h + 1, :])
    y = acc_ref[...].T
    o_ref[0] = (y * jax.nn.silu(az_ref[0])).astype(BF16)


def _dsa(aq, iq, iwt, az, ik2, ak, avt, bias):
    b, s, _ = aq.shape
    nq = s // TQ
    qrow = lambda w: pl.BlockSpec((1, TQ, w), lambda bi, i: (bi, i, 0))
    once = pl.Buffered(1)
    return pl.pallas_call(
        _dsa_kernel,
        grid=(b, nq),
        in_specs=[qrow(ATTN_WIDTH), qrow(IDX_HEADS * IDX_DIM),
                  pl.BlockSpec((1, IDX_HEADS, TQ), lambda bi, i: (bi, 0, i)),
                  qrow(ATTN_WIDTH),
                  pl.BlockSpec((1, s, LANES), lambda bi, i: (bi, 0, 0), pipeline_mode=once),
                  pl.BlockSpec((1, s, ATTN_WIDTH), lambda bi, i: (bi, 0, 0), pipeline_mode=once),
                  pl.BlockSpec((1, s // KT, ATTN_WIDTH, KT), lambda bi, i: (bi, 0, 0, 0), pipeline_mode=once),
                  pl.BlockSpec(bias.shape, lambda bi, i: (0, 0, 0, 0), pipeline_mode=once)],
        out_specs=qrow(ATTN_WIDTH),
        out_shape=jax.ShapeDtypeStruct((b, s, ATTN_WIDTH), BF16),
        scratch_shapes=[pltpu.VMEM((s, TQ), F32),
                        pltpu.VMEM((N_HEADS, TQ, LANES), BF16),
                        pltpu.VMEM((IDX_HEADS, TQ, LANES), BF16),
                        pltpu.VMEM((ATTN_WIDTH, TQ), F32),
                        pltpu.VMEM((N_HEADS, TQ), F32),
                        pltpu.VMEM((N_HEADS, TQ), F32),
                        pltpu.VMEM((1, TQ), F32)],
        compiler_params=pltpu.CompilerParams(
            dimension_semantics=("arbitrary", "arbitrary"), vmem_limit_bytes=VMEM_LIMIT),
        name="dsa",
    )(aq, iq, iwt, az, ik2, ak, avt, bias)


def _merge_kernel(final, x_ref, pu_ref, halo_ref, pz_ref, mq_ref, mz_ref, ya_ref, gates_ref,
                  mk_ref, mv_ref, pw_ref, ps_ref, wb_ref, wo_ref, fg_ref, o_ref):
    i = pl.program_id(1)

    u = pu_ref[0]
    halo = jnp.where(i == 0, 0.0, halo_ref[0])
    ext = jnp.concatenate([halo, u], axis=0)
    t_glob = i * TM + lax.broadcasted_iota(I32, (TM, POOL_GC), 0)
    mixed = []
    for g in range(POOL_GROUPS):
        sl = slice(POOL_GC * g, POOL_GC * (g + 1))
        wsum = ext[:, sl]
        for step in range(g + 1):
            wsum = wsum + pltpu.roll(wsum, 2 ** step, axis=0)
        cnt = jnp.minimum(t_glob + 1, POOL_WINDOWS[g]).astype(F32)
        pooled = wsum[HALO:, :] / cnt - u[:, sl]
        mixed.append(jnp.dot(pooled.astype(BF16), pw_ref[g], preferred_element_type=F32))
    y_pool = jnp.concatenate(mixed, axis=1) * ps_ref[...] * jax.nn.silu(pz_ref[0])

    mem_out = []
    for h in range(MEM_HEADS):
        sl = slice(MEM_HEAD_DIM * h, MEM_HEAD_DIM * (h + 1))
        logits = lax.dot_general(mq_ref[0, :, sl], mk_ref[0, :, sl], _NT_DIMS,
                                 preferred_element_type=F32) * MEM_SCALE
        e = jnp.exp(logits - logits.max(axis=-1, keepdims=True))
        p = e / e.sum(axis=-1, keepdims=True)
        mem_out.append(jnp.dot(p.astype(BF16), mv_ref[0, :, sl], preferred_element_type=F32))
    y_mem = jnp.concatenate(mem_out, axis=1) * jax.nn.silu(mz_ref[0])

    branches = (y_pool.astype(BF16), ya_ref[0], y_mem.astype(BF16))
    merged = jnp.zeros((TM, D_MODEL), F32)
    for br in range(N_BRANCH):
        gate = jax.nn.sigmoid(gates_ref[0, :, D_MODEL * br:D_MODEL * (br + 1)])
        merged = merged + gate * jnp.dot(branches[br], wb_ref[br], preferred_element_type=F32)
    out = x_ref[0] + jnp.dot(merged.astype(BF16), wo_ref[...], preferred_element_type=F32)
    if final:
        out = _rms_norm_f32(out, fg_ref[...])
    o_ref[0] = out


def _merge(final, x, pu, pz, mq, mz, ya, gates, mk, mv, pw, ps, wb, wo, fg):
    b, s, d = x.shape
    row = lambda w: pl.BlockSpec((1, TM, w), lambda bi, i: (bi, i, 0))
    full = lambda a: pl.BlockSpec(a.shape, lambda bi, i: (0,) * a.ndim)
    per_b = lambda a: pl.BlockSpec((1,) + a.shape[1:], lambda bi, i: (bi,) + (0,) * (a.ndim - 1))
    halo = pl.BlockSpec((1, HALO, POOL_WIDTH),
                        lambda bi, i: (bi, jnp.maximum(i * (TM // HALO) - 1, 0), 0))
    return pl.pallas_call(
        functools.partial(_merge_kernel, final),
        grid=(b, s // TM),
        in_specs=[row(d), row(POOL_WIDTH), halo, row(POOL_WIDTH), row(MEM_WIDTH), row(MEM_WIDTH),
                  row(ATTN_WIDTH), row(N_BRANCH * D_MODEL), per_b(mk), per_b(mv),
                  full(pw), full(ps), full(wb), full(wo), full(fg)],
        out_specs=row(d),
        out_shape=jax.ShapeDtypeStruct((b, s, d), F32),
        compiler_params=pltpu.CompilerParams(
            dimension_semantics=("arbitrary", "arbitrary"), vmem_limit_bytes=VMEM_LIMIT),
        name="merge_final" if final else "merge",
    )(x, pu, pu, pz, mq, mz, ya, gates, mk, mv, pw, ps, wb, wo, fg)


def _split_w_in(w):
    col = lambda lo, hi: w[:, lo:hi]
    wf = jnp.concatenate([col(O_PU, O_PZ), col(O_PZ, O_AQ), col(O_AZ, O_IQ), col(O_MZ, O_G),
                          col(O_G, O_END)], axis=1)
    wb = jnp.concatenate([col(O_AQ, O_AK) * ATTN_SCALE, col(O_AK, O_AV), col(O_IQ, O_IK),
                          col(O_MQ, O_MZ), col(O_IK, O_IW), col(O_IK, O_IW)], axis=1)
    wt = jnp.concatenate([col(O_AV, O_AZ), col(O_IW, O_MQ),
                          jnp.zeros((w.shape[0], WT_ROWS - ATTN_WIDTH - IDX_HEADS), w.dtype)], axis=1).T
    return wf.astype(BF16), wb.astype(BF16), wt.astype(BF16)


def kernel(x, mem, norm_g, w_in, pool_w, pool_scale, mem_norm_g, w_mem_kv, w_branch, w_out,
           rel_bias, final_g):
    depth = w_in.shape[0]
    assert x.shape[1] % TQ == 0 and x.shape[2] == D_MODEL and mem.shape[1] == MEM_LEN
    assert w_in.shape[2] == O_END and TQ == KT == TM
    bias = _bias_tiles(rel_bias)
    fg = final_g.reshape(1, D_MODEL)
    for l in range(depth):
        wf, wb, wt = _split_w_in(w_in[l])
        (pu, pz, az, mz, gates, aq, ak, iq, mq, ik2, avt, iwt) = _project(
            x, norm_g[l].reshape(1, D_MODEL), wf, wb, wt)
        mk, mv = _mem_kv(mem, mem_norm_g[l].reshape(1, D_MODEL), w_mem_kv[l].astype(BF16))
        ya = _dsa(aq, iq, iwt, az, ik2, ak, avt, bias)
        x = _merge(l == depth - 1, x, pu, pz, mq, mz, ya, gates, mk, mv,
                   pool_w[l].astype(BF16), pool_scale[l].reshape(1, POOL_WIDTH),
                   w_branch[l].astype(BF16), w_out[l].astype(BF16), fg)
    return x
```

```python
import functools
import math

import numpy as np
import jax
import jax.numpy as jnp
from jax import lax
from jax.experimental import pallas as pl
from jax.experimental.pallas import tpu as pltpu

F32 = jnp.float32
BF16 = jnp.bfloat16
I32 = jnp.int32

D_MODEL = 1024
CHUNK = 64
MEM_LEN = 256
POOL_GROUPS = 4
POOL_WINDOWS = (2, 4, 8, 16)
POOL_WIDTH = D_MODEL // 2
POOL_GC = POOL_WIDTH // POOL_GROUPS
N_HEADS = 8
HEAD_DIM = 64
ATTN_WIDTH = N_HEADS * HEAD_DIM
IDX_HEADS = 8
IDX_DIM = 64
TOPK = 256
ATTN_SCALE = HEAD_DIM ** -0.5
IDX_W_SCALE = (IDX_DIM ** -0.5) * (IDX_HEADS ** -0.5)
MEM_HEADS = 4
MEM_HEAD_DIM = 128
MEM_WIDTH = MEM_HEADS * MEM_HEAD_DIM
MEM_SCALE = MEM_HEAD_DIM ** -0.5
N_BRANCH = 3
REL_BUCKETS = 32
REL_MAX_DIST = 1024
EPS = 1e-6
NEG = -1e30
HALF_NEG = -5e29

IN_SPLITS = (POOL_WIDTH, POOL_WIDTH, ATTN_WIDTH, ATTN_WIDTH, ATTN_WIDTH, ATTN_WIDTH,
             IDX_HEADS * IDX_DIM, IDX_DIM, IDX_HEADS, MEM_WIDTH, MEM_WIDTH, N_BRANCH * D_MODEL)
_OFF = np.concatenate([[0], np.cumsum(IN_SPLITS)]).tolist()
(O_PU, O_PZ, O_AQ, O_AK, O_AV, O_AZ, O_IQ, O_IK, O_IW, O_MQ, O_MZ, O_G, O_END) = _OFF

LANES = 128
SUBLANES = 8
TQ = 256
KT = 256
TM = 256
HALO = 16
N_NEAR = 4
COUNT_ROWS = 2 * KT
ACC_ROWS = 4 * SUBLANES
WT_ROWS = 528
VMEM_LIMIT = 56 * 1024 * 1024

_NT_DIMS = (((1,), (1,)), ((), ()))


def _rms_norm_f32(x, g):
    ms = jnp.mean(x * x, axis=-1, keepdims=True)
    return x * lax.rsqrt(ms + EPS) * g


def _proj_kernel(x_ref, g_ref, wf_ref, wb_ref, wt_ref,
                 pu_ref, pz_ref, az_ref, mz_ref, gates_ref,
                 aq_ref, ak_ref, iq_ref, mq_ref, ik2_ref, avt_ref, iwt_ref):
    h = _rms_norm_f32(x_ref[0], g_ref[...]).astype(BF16)

    def mm(w_ref, lo, hi):
        return jnp.dot(h, w_ref[:, lo:hi], preferred_element_type=F32)

    pu_ref[0] = mm(wf_ref, 0, 512)
    pz_ref[0] = mm(wf_ref, 512, 1024)
    az_ref[0] = mm(wf_ref, 1024, 1536)
    mz_ref[0] = mm(wf_ref, 1536, 2048)
    for c in range(N_BRANCH * D_MODEL // 512):
        gates_ref[0, :, 512 * c:512 * (c + 1)] = mm(wf_ref, 2048 + 512 * c, 2048 + 512 * (c + 1))
    aq_ref[0] = mm(wb_ref, 0, 512).astype(BF16)
    ak_ref[0] = mm(wb_ref, 512, 1024).astype(BF16)
    iq_ref[0] = mm(wb_ref, 1024, 1536).astype(BF16)
    mq_ref[0] = mm(wb_ref, 1536, 2048).astype(BF16)
    ik2_ref[0] = mm(wb_ref, 2048, 2176).astype(BF16)
    t = lax.dot_general(wt_ref[...], h, _NT_DIMS, preferred_element_type=F32)
    avt_ref[0, 0] = t[0:ATTN_WIDTH].astype(BF16)
    iwt_ref[0] = t[ATTN_WIDTH:ATTN_WIDTH + IDX_HEADS] * IDX_W_SCALE


def _project(x, g, wf, wb, wt):
    b, s, d = x.shape
    nt = s // TM
    row = lambda w: pl.BlockSpec((1, TM, w), lambda bi, i: (bi, i, 0))
    full = lambda a: pl.BlockSpec(a.shape, lambda bi, i: (0,) * a.ndim)
    f32o = lambda w: jax.ShapeDtypeStruct((b, s, w), F32)
    b16o = lambda w: jax.ShapeDtypeStruct((b, s, w), BF16)
    out_shape = (f32o(512), f32o(512), f32o(512), f32o(512), f32o(N_BRANCH * D_MODEL),
                 b16o(512), b16o(512), b16o(512), b16o(512), b16o(128),
                 jax.ShapeDtypeStruct((b, nt, ATTN_WIDTH, TM), BF16),
                 jax.ShapeDtypeStruct((b, IDX_HEADS, s), F32))
    out_specs = (row(512), row(512), row(512), row(512), row(N_BRANCH * D_MODEL),
                 row(512), row(512), row(512), row(512), row(128),
                 pl.BlockSpec((1, 1, ATTN_WIDTH, TM), lambda bi, i: (bi, i, 0, 0)),
                 pl.BlockSpec((1, IDX_HEADS, TM), lambda bi, i: (bi, 0, i)))
    return pl.pallas_call(
        _proj_kernel,
        grid=(b, nt),
        in_specs=[row(d), full(g), full(wf), full(wb), full(wt)],
        out_specs=out_specs,
        out_shape=out_shape,
        compiler_params=pltpu.CompilerParams(
            dimension_semantics=("arbitrary", "arbitrary"), vmem_limit_bytes=VMEM_LIMIT),
        name="proj",
    )(x, g, wf, wb, wt)


def _memkv_kernel(mem_ref, g_ref, w_ref, mk_ref, mv_ref):
    h = _rms_norm_f32(mem_ref[0], g_ref[...]).astype(BF16)
    mk_ref[0] = jnp.dot(h, w_ref[:, 0:MEM_WIDTH], preferred_element_type=F32).astype(BF16)
    mv_ref[0] = jnp.dot(h, w_ref[:, MEM_WIDTH:2 * MEM_WIDTH], preferred_element_type=F32).astype(BF16)


def _mem_kv(mem, g, w):
    b, m, d = mem.shape
    o = jax.ShapeDtypeStruct((b, m, MEM_WIDTH), BF16)
    return pl.pallas_call(
        _memkv_kernel,
        grid=(b,),
        in_specs=[pl.BlockSpec((1, m, d), lambda bi: (bi, 0, 0)),
                  pl.BlockSpec(g.shape, lambda bi: (0, 0)),
                  pl.BlockSpec(w.shape, lambda bi: (0, 0))],
        out_specs=(pl.BlockSpec((1, m, MEM_WIDTH), lambda bi: (bi, 0, 0)),) * 2,
        out_shape=(o, o),
        compiler_params=pltpu.CompilerParams(
            dimension_semantics=("arbitrary",), vmem_limit_bytes=VMEM_LIMIT),
        name="memkv",
    )(mem, g, w)


def _bucket_table():
    half = REL_BUCKETS // 2
    max_exact = half // 2
    delta = np.arange(N_NEAR, dtype=np.int64)[:, None, None]
    key = np.arange(KT, dtype=np.int64)[None, :, None]
    qry = np.arange(TQ, dtype=np.int64)[None, None, :]
    rel = key - qry - delta * KT
    ret = np.where(rel > 0, half, 0)
    n = np.abs(rel)
    nf = np.maximum(n, 1).astype(np.float32)
    large = max_exact + (np.log(nf / np.float32(max_exact))
                         / np.float32(math.log(REL_MAX_DIST / max_exact))
                         * np.float32(half - max_exact)).astype(np.int32)
    large = np.minimum(large, half - 1)
    return (ret + np.where(n < max_exact, n, large)).astype(np.int32)


FAR_BUCKET = REL_BUCKETS // 2 - 1


def _bias_kernel(rb_ref, bk_ref, o_ref):
    h = pl.program_id(1)
    bk = bk_ref[0]
    far = rb_ref[FAR_BUCKET, h]
    acc = jnp.zeros((KT, TQ), F32)
    for bkt in range(REL_BUCKETS):
        acc = jnp.where(bk == bkt, rb_ref[bkt, h] - far, acc)
    o_ref[0, 0] = acc


def _bias_tiles(rel_bias):
    table = jnp.asarray(_bucket_table())
    return pl.pallas_call(
        _bias_kernel,
        grid=(N_NEAR, N_HEADS),
        in_specs=[pl.BlockSpec(memory_space=pltpu.SMEM),
                  pl.BlockSpec((1, KT, TQ), lambda dl, h: (dl, 0, 0))],
        out_specs=pl.BlockSpec((1, 1, KT, TQ), lambda dl, h: (dl, h, 0, 0)),
        out_shape=jax.ShapeDtypeStruct((N_NEAR, N_HEADS, KT, TQ), F32),
        compiler_params=pltpu.CompilerParams(dimension_semantics=("arbitrary", "arbitrary")),
        name="bias_tiles",
    )(rel_bias, table)


def _dsa_kernel(q_ref, iq_ref, iwt_ref, az_ref, ik2_ref, k_ref, vt_ref, bias_ref, o_ref,
                sc_ref, qw_ref, iqw_ref, acc_ref, m_ref, l_ref, thr_ref,
                stage_ref, tmax_ref, tsum_ref):
    i = pl.program_id(1)
    n_tiles = i + 1

    lane = lax.broadcasted_iota(I32, (TQ, LANES), 1)
    for h in range(N_HEADS):
        j = h // 2
        sel = (lane >= HEAD_DIM) if (h % 2) else (lane < HEAD_DIM)
        qw_ref[h] = jnp.where(sel, q_ref[0, :, LANES * j:LANES * (j + 1)].astype(F32), 0.0).astype(BF16)
        iqw_ref[h] = jnp.where(sel, iq_ref[0, :, LANES * j:LANES * (j + 1)].astype(F32), 0.0).astype(BF16)

    key_l = lax.broadcasted_iota(I32, (KT, TQ), 0)
    qry_l = lax.broadcasted_iota(I32, (KT, TQ), 1)
    adm_diag = key_l < ((qry_l // CHUNK) + 1) * CHUNK

    def score_tile(t, carry):
        r0 = pl.multiple_of(t * KT, KT)
        ikt = ik2_ref[0, pl.ds(r0, KT), :]
        acc = jnp.zeros((KT, TQ), F32)
        for h in range(IDX_HEADS):
            y = lax.dot_general(ikt, iqw_ref[h], _NT_DIMS, preferred_element_type=F32)
            acc = acc + jnp.maximum(y, 0.0) * iwt_ref[0, h:h + 1, :]
        adm = jnp.logical_or(adm_diag, t < i)
        sc_ref[pl.ds(r0, KT), :] = jnp.where(adm, acc, NEG)
        return carry

    lax.fori_loop(0, n_tiles, score_tile, 0)

    n_steps = (n_tiles + 1) // 2

    @pl.when(n_tiles % 2 == 1)
    def _():
        sc_ref[pl.ds(pl.multiple_of(n_tiles * KT, KT), KT), :] = jnp.full((KT, TQ), NEG, F32)

    def count(pred):
        def body(c, acc):
            r0 = pl.multiple_of(c * COUNT_ROWS, COUNT_ROWS)
            x = sc_ref[pl.ds(r0, COUNT_ROWS), :]
            ind = jnp.where(pred(x, r0), F32(1.0), F32(0.0))
            return acc + ind.reshape(COUNT_ROWS // ACC_ROWS, ACC_ROWS, TQ).sum(axis=0)
        acc = lax.fori_loop(0, n_steps, body, jnp.zeros((ACC_ROWS, TQ), F32))
        return acc.sum(axis=0, keepdims=True)

    def to_float(u):
        bits = jnp.where(u < 0, u ^ I32(-2 ** 31), ~u)
        return lax.bitcast_convert_type(bits, F32)

    @pl.when(i == 0)
    def _():
        thr_ref[...] = jnp.full((1, TQ), HALF_NEG, F32)

    @pl.when(i > 0)
    def _():
        def bit_body(bi, u):
            cand = u | lax.shift_left(I32(1), I32(31) - bi)
            cf = to_float(cand)
            cnt = count(lambda x, r0: x >= cf)
            return jnp.where(cnt >= float(TOPK), cand, u)

        u = lax.fori_loop(0, 32, bit_body, jnp.zeros((1, TQ), I32))
        thr = to_float(u)
        thr_ref[...] = thr
        cnt_ge = count(lambda x, r0: x >= thr)

        @pl.when(jnp.max(cnt_ge) > float(TOPK))
        def _():
            need = float(TOPK) - count(lambda x, r0: x > thr)
            row = lax.broadcasted_iota(I32, (COUNT_ROWS, TQ), 0)
            n_bits = max(1, (sc_ref.shape[0] - 1).bit_length())

            def idx_body(bi, p):
                cand = p | lax.shift_left(I32(1), I32(n_bits - 1) - bi)
                cnt = count(lambda x, r0: jnp.logical_and(x == thr, row + r0 < cand))
                return jnp.where(cnt < need, cand, p)

            last = lax.fori_loop(0, n_bits, idx_body, jnp.zeros((1, TQ), I32))

            def drop(c, carry):
                r0 = pl.multiple_of(c * COUNT_ROWS, COUNT_ROWS)
                x = sc_ref[pl.ds(r0, COUNT_ROWS), :]
                gone = jnp.logical_and(x == thr, row + r0 > last)
                sc_ref[pl.ds(r0, COUNT_ROWS), :] = jnp.where(gone, NEG, x)
                return carry

            lax.fori_loop(0, n_steps, drop, 0)

    m_ref[...] = jnp.full(m_ref.shape, NEG, F32)
    l_ref[...] = jnp.zeros(l_ref.shape, F32)
    acc_ref[...] = jnp.zeros(acc_ref.shape, F32)
    thr = thr_ref[...]

    def to_mask(c, carry):
        r0 = pl.multiple_of(c * KT, KT)
        sc_ref[pl.ds(r0, KT), :] = jnp.where(sc_ref[pl.ds(r0, KT), :] >= thr, F32(0.0), F32(NEG))
        return carry

    lax.fori_loop(0, n_tiles, to_mask, 0)

    def attn_tile(t, near):
        r0 = pl.multiple_of(t * KT, KT)
        for h in range(N_HEADS):
            j = h // 2
            kk = k_ref[0, pl.ds(r0, KT), LANES * j:LANES * (j + 1)]
            s = lax.dot_general(kk, qw_ref[h], _NT_DIMS, preferred_element_type=F32)
            s = s + sc_ref[pl.ds(r0, KT), :]
            if near is not None:
                s = s + bias_ref[near, h]
            stage_ref[h] = s
            tmax_ref[h:h + 1, :] = s.reshape(KT // ACC_ROWS, ACC_ROWS, TQ).max(axis=0).max(
                axis=0, keepdims=True)
        m_old = m_ref[...]
        m_new = jnp.maximum(m_old, tmax_ref[...])
        alpha = jnp.exp(m_old - m_new)
        m_ref[...] = m_new
        for h in range(N_HEADS):
            p = jnp.exp(stage_ref[h] - m_new[h:h + 1, :])
            tsum_ref[h:h + 1, :] = p.reshape(KT // ACC_ROWS, ACC_ROWS, TQ).sum(axis=0).sum(
                axis=0, keepdims=True)
            pv = jnp.dot(vt_ref[0, t, HEAD_DIM * h:HEAD_DIM * (h + 1), :], p.astype(BF16),
                         preferred_element_type=F32)
            acc_ref[HEAD_DIM * h:HEAD_DIM * (h + 1), :] = (
                alpha[h:h + 1, :] * acc_ref[HEAD_DIM * h:HEAD_DIM * (h + 1), :] + pv)
        l_ref[...] = alpha * l_ref[...] + tsum_ref[...]

    def far_tile(t, carry):
        attn_tile(t, None)
        return carry

    lax.fori_loop(0, jnp.maximum(n_tiles - N_NEAR, 0), far_tile, 0)
    for near in range(N_NEAR - 1, -1, -1):
        @pl.when(i >= near)
        def _(near=near):
            attn_tile(i - near, near)

    inv_l = 1.0 / l_ref[...]
    for h in range(N_HEADS):
        acc_ref[HEAD_DIM * h:HEAD_DIM * (h + 1), :] = (
            acc_ref[HEAD_DIM * h:HEAD_DIM * (h + 1), :] * inv_l[h:h + 1, :])
    y = acc_ref[...].T
    o_ref[0] = (y * jax.nn.silu(az_ref[0])).astype(BF16)


def _dsa(aq, iq, iwt, az, ik2, ak, avt, bias):
    b, s, _ = aq.shape
    nq = s // TQ
    qrow = lambda w: pl.BlockSpec((1, TQ, w), lambda bi, i: (bi, i, 0))
    once = pl.Buffered(1)
    return pl.pallas_call(
        _dsa_kernel,
        grid=(b, nq),
        in_specs=[qrow(ATTN_WIDTH), qrow(IDX_HEADS * IDX_DIM),
                  pl.BlockSpec((1, IDX_HEADS, TQ), lambda bi, i: (bi, 0, i)),
                  qrow(ATTN_WIDTH),
                  pl.BlockSpec((1, s, LANES), lambda bi, i: (bi, 0, 0), pipeline_mode=once),
                  pl.BlockSpec((1, s, ATTN_WIDTH), lambda bi, i: (bi, 0, 0), pipeline_mode=once),
                  pl.BlockSpec((1, s // KT, ATTN_WIDTH, KT), lambda bi, i: (bi, 0, 0, 0), pipeline_mode=once),
                  pl.BlockSpec(bias.shape, lambda bi, i: (0, 0, 0, 0), pipeline_mode=once)],
        out_specs=qrow(ATTN_WIDTH),
        out_shape=jax.ShapeDtypeStruct((b, s, ATTN_WIDTH), BF16),
        scratch_shapes=[pltpu.VMEM((s, TQ), F32),
                        pltpu.VMEM((N_HEADS, TQ, LANES), BF16),
                        pltpu.VMEM((IDX_HEADS, TQ, LANES), BF16),
                        pltpu.VMEM((ATTN_WIDTH, TQ), F32),
                        pltpu.VMEM((N_HEADS, TQ), F32),
                        pltpu.VMEM((N_HEADS, TQ), F32),
                        pltpu.VMEM((1, TQ), F32),
                        pltpu.VMEM((N_HEADS, KT, TQ), F32),
                        pltpu.VMEM((N_HEADS, TQ), F32),
                        pltpu.VMEM((N_HEADS, TQ), F32)],
        compiler_params=pltpu.CompilerParams(
            dimension_semantics=("arbitrary", "arbitrary"), vmem_limit_bytes=VMEM_LIMIT),
        name="dsa",
    )(aq, iq, iwt, az, ik2, ak, avt, bias)


def _merge_kernel(final, x_ref, pu_ref, halo_ref, pz_ref, mq_ref, mz_ref, ya_ref, gates_ref,
                  mk_ref, mv_ref, pw_ref, ps_ref, wb_ref, wo_ref, fg_ref, o_ref):
    i = pl.program_id(1)

    u = pu_ref[0]
    halo = jnp.where(i == 0, 0.0, halo_ref[0])
    ext = jnp.concatenate([halo, u], axis=0)
    t_glob = i * TM + lax.broadcasted_iota(I32, (TM, POOL_GC), 0)
    mixed = []
    for g in range(POOL_GROUPS):
        sl = slice(POOL_GC * g, POOL_GC * (g + 1))
        wsum = ext[:, sl]
        for step in range(g + 1):
            wsum = wsum + pltpu.roll(wsum, 2 ** step, axis=0)
        cnt = jnp.minimum(t_glob + 1, POOL_WINDOWS[g]).astype(F32)
        pooled = wsum[HALO:, :] / cnt - u[:, sl]
        mixed.append(jnp.dot(pooled.astype(BF16), pw_ref[g], preferred_element_type=F32))
    y_pool = jnp.concatenate(mixed, axis=1) * ps_ref[...] * jax.nn.silu(pz_ref[0])

    mem_out = []
    for h in range(MEM_HEADS):
        sl = slice(MEM_HEAD_DIM * h, MEM_HEAD_DIM * (h + 1))
        logits = lax.dot_general(mq_ref[0, :, sl], mk_ref[0, :, sl], _NT_DIMS,
                                 preferred_element_type=F32) * MEM_SCALE
        e = jnp.exp(logits - logits.max(axis=-1, keepdims=True))
        p = e / e.sum(axis=-1, keepdims=True)
        mem_out.append(jnp.dot(p.astype(BF16), mv_ref[0, :, sl], preferred_element_type=F32))
    y_mem = jnp.concatenate(mem_out, axis=1) * jax.nn.silu(mz_ref[0])

    branches = (y_pool.astype(BF16), ya_ref[0], y_mem.astype(BF16))
    merged = jnp.zeros((TM, D_MODEL), F32)
    for br in range(N_BRANCH):
        gate = jax.nn.sigmoid(gates_ref[0, :, D_MODEL * br:D_MODEL * (br + 1)])
        merged = merged + gate * jnp.dot(branches[br], wb_ref[br], preferred_element_type=F32)
    out = x_ref[0] + jnp.dot(merged.astype(BF16), wo_ref[...], preferred_element_type=F32)
    if final:
        out = _rms_norm_f32(out, fg_ref[...])
    o_ref[0] = out


def _merge(final, x, pu, pz, mq, mz, ya, gates, mk, mv, pw, ps, wb, wo, fg):
    b, s, d = x.shape
    row = lambda w: pl.BlockSpec((1, TM, w), lambda bi, i: (bi, i, 0))
    full = lambda a: pl.BlockSpec(a.shape, lambda bi, i: (0,) * a.ndim)
    per_b = lambda a: pl.BlockSpec((1,) + a.shape[1:], lambda bi, i: (bi,) + (0,) * (a.ndim - 1))
    halo = pl.BlockSpec((1, HALO, POOL_WIDTH),
                        lambda bi, i: (bi, jnp.maximum(i * (TM // HALO) - 1, 0), 0))
    return pl.pallas_call(
        functools.partial(_merge_kernel, final),
        grid=(b, s // TM),
        in_specs=[row(d), row(POOL_WIDTH), halo, row(POOL_WIDTH), row(MEM_WIDTH), row(MEM_WIDTH),
                  row(ATTN_WIDTH), row(N_BRANCH * D_MODEL), per_b(mk), per_b(mv),
                  full(pw), full(ps), full(wb), full(wo), full(fg)],
        out_specs=row(d),
        out_shape=jax.ShapeDtypeStruct((b, s, d), F32),
        compiler_params=pltpu.CompilerParams(
            dimension_semantics=("arbitrary", "arbitrary"), vmem_limit_bytes=VMEM_LIMIT),
        name="merge_final" if final else "merge",
    )(x, pu, pu, pz, mq, mz, ya, gates, mk, mv, pw, ps, wb, wo, fg)


def _split_w_in(w):
    col = lambda lo, hi: w[:, lo:hi]
    wf = jnp.concatenate([col(O_PU, O_PZ), col(O_PZ, O_AQ), col(O_AZ, O_IQ), col(O_MZ, O_G),
                          col(O_G, O_END)], axis=1)
    wb = jnp.concatenate([col(O_AQ, O_AK) * ATTN_SCALE, col(O_AK, O_AV), col(O_IQ, O_IK),
                          col(O_MQ, O_MZ), col(O_IK, O_IW), col(O_IK, O_IW)], axis=1)
    wt = jnp.concatenate([col(O_AV, O_AZ), col(O_IW, O_MQ),
                          jnp.zeros((w.shape[0], WT_ROWS - ATTN_WIDTH - IDX_HEADS), w.dtype)], axis=1).T
    return wf.astype(BF16), wb.astype(BF16), wt.astype(BF16)


def kernel(x, mem, norm_g, w_in, pool_w, pool_scale, mem_norm_g, w_mem_kv, w_branch, w_out,
           rel_bias, final_g):
    depth = w_in.shape[0]
    assert x.shape[1] % COUNT_ROWS == 0 and x.shape[2] == D_MODEL and mem.shape[1] == MEM_LEN
    assert w_in.shape[2] == O_END and TQ == KT == TM
    bias = _bias_tiles(rel_bias)
    fg = final_g.reshape(1, D_MODEL)
    for l in range(depth):
        wf, wb, wt = _split_w_in(w_in[l])
        (pu, pz, az, mz, gates, aq, ak, iq, mq, ik2, avt, iwt) = _project(
            x, norm_g[l].reshape(1, D_MODEL), wf, wb, wt)
        mk, mv = _mem_kv(mem, mem_norm_g[l].reshape(1, D_MODEL), w_mem_kv[l].astype(BF16))
        ya = _dsa(aq, iq, iwt, az, ik2, ak, avt, bias)
        x = _merge(l == depth - 1, x, pu, pz, mq, mz, ya, gates, mk, mv,
                   pool_w[l].astype(BF16), pool_scale[l].reshape(1, POOL_WIDTH),
                   w_branch[l].astype(BF16), w_out[l].astype(BF16), fg)
    return x
```

```python
import functools
import math

import numpy as np
import jax
import jax.numpy as jnp
from jax import lax
from jax.experimental import pallas as pl
from jax.experimental.pallas import tpu as pltpu

F32 = jnp.float32
BF16 = jnp.bfloat16
I32 = jnp.int32

D_MODEL = 1024
CHUNK = 64
MEM_LEN = 256
POOL_GROUPS = 4
POOL_WINDOWS = (2, 4, 8, 16)
POOL_WIDTH = D_MODEL // 2
POOL_GC = POOL_WIDTH // POOL_GROUPS
N_HEADS = 8
HEAD_DIM = 64
ATTN_WIDTH = N_HEADS * HEAD_DIM
IDX_HEADS = 8
IDX_DIM = 64
TOPK = 256
ATTN_SCALE = HEAD_DIM ** -0.5
LOG2E = math.log2(math.e)
IDX_W_SCALE = (IDX_DIM ** -0.5) * (IDX_HEADS ** -0.5)
MEM_HEADS = 4
MEM_HEAD_DIM = 128
MEM_WIDTH = MEM_HEADS * MEM_HEAD_DIM
MEM_SCALE = MEM_HEAD_DIM ** -0.5
N_BRANCH = 3
REL_BUCKETS = 32
REL_MAX_DIST = 1024
EPS = 1e-6
NEG = -1e30
HALF_NEG = -5e29

IN_SPLITS = (POOL_WIDTH, POOL_WIDTH, ATTN_WIDTH, ATTN_WIDTH, ATTN_WIDTH, ATTN_WIDTH,
             IDX_HEADS * IDX_DIM, IDX_DIM, IDX_HEADS, MEM_WIDTH, MEM_WIDTH, N_BRANCH * D_MODEL)
_OFF = np.concatenate([[0], np.cumsum(IN_SPLITS)]).tolist()
(O_PU, O_PZ, O_AQ, O_AK, O_AV, O_AZ, O_IQ, O_IK, O_IW, O_MQ, O_MZ, O_G, O_END) = _OFF

LANES = 128
SUBLANES = 8
TQ = 256
KT = 256
TM = 256
HALO = 16
N_NEAR = 4
COUNT_ROWS = 2 * KT
ACC_ROWS = 4 * SUBLANES
WT_ROWS = 528
V_ROWS = HEAD_DIM + 16
VMEM_LIMIT = 56 * 1024 * 1024

_NT_DIMS = (((1,), (1,)), ((), ()))


def _rms_norm_f32(x, g):
    ms = jnp.mean(x * x, axis=-1, keepdims=True)
    return x * lax.rsqrt(ms + EPS) * g


def _proj_kernel(x_ref, g_ref, wf_ref, wb_ref, wt_ref,
                 pu_ref, pz_ref, az_ref, mz_ref, gates_ref,
                 aq_ref, ak_ref, iq_ref, mq_ref, ik2_ref, avt_ref, iwt_ref):
    h = _rms_norm_f32(x_ref[0], g_ref[...]).astype(BF16)

    def mm(w_ref, lo, hi):
        return jnp.dot(h, w_ref[:, lo:hi], preferred_element_type=F32)

    pu_ref[0] = mm(wf_ref, 0, 512)
    pz_ref[0] = mm(wf_ref, 512, 1024)
    az_ref[0] = mm(wf_ref, 1024, 1536)
    mz_ref[0] = mm(wf_ref, 1536, 2048)
    for c in range(N_BRANCH * D_MODEL // 512):
        gates_ref[0, :, 512 * c:512 * (c + 1)] = mm(wf_ref, 2048 + 512 * c, 2048 + 512 * (c + 1))
    aq_ref[0] = mm(wb_ref, 0, 512).astype(BF16)
    ak_ref[0] = mm(wb_ref, 512, 1024).astype(BF16)
    iq_ref[0] = mm(wb_ref, 1024, 1536).astype(BF16)
    mq_ref[0] = mm(wb_ref, 1536, 2048).astype(BF16)
    ik2_ref[0] = mm(wb_ref, 2048, 2176).astype(BF16)
    t = lax.dot_general(wt_ref[...], h, _NT_DIMS, preferred_element_type=F32)
    pad_row = lax.broadcasted_iota(I32, (V_ROWS - HEAD_DIM, TM), 0)
    pad = jnp.where(pad_row == 0, 1.0, 0.0).astype(BF16)
    for hd in range(N_HEADS):
        avt_ref[0, 0, V_ROWS * hd:V_ROWS * hd + HEAD_DIM, :] = (
            t[HEAD_DIM * hd:HEAD_DIM * (hd + 1)].astype(BF16))
        avt_ref[0, 0, V_ROWS * hd + HEAD_DIM:V_ROWS * (hd + 1), :] = pad
    iwt_ref[0] = t[ATTN_WIDTH:ATTN_WIDTH + IDX_HEADS] * IDX_W_SCALE


def _project(x, g, wf, wb, wt):
    b, s, d = x.shape
    nt = s // TM
    row = lambda w: pl.BlockSpec((1, TM, w), lambda bi, i: (bi, i, 0))
    full = lambda a: pl.BlockSpec(a.shape, lambda bi, i: (0,) * a.ndim)
    f32o = lambda w: jax.ShapeDtypeStruct((b, s, w), F32)
    b16o = lambda w: jax.ShapeDtypeStruct((b, s, w), BF16)
    out_shape = (f32o(512), f32o(512), f32o(512), f32o(512), f32o(N_BRANCH * D_MODEL),
                 b16o(512), b16o(512), b16o(512), b16o(512), b16o(128),
                 jax.ShapeDtypeStruct((b, nt, N_HEADS * V_ROWS, TM), BF16),
                 jax.ShapeDtypeStruct((b, IDX_HEADS, s), F32))
    out_specs = (row(512), row(512), row(512), row(512), row(N_BRANCH * D_MODEL),
                 row(512), row(512), row(512), row(512), row(128),
                 pl.BlockSpec((1, 1, N_HEADS * V_ROWS, TM), lambda bi, i: (bi, i, 0, 0)),
                 pl.BlockSpec((1, IDX_HEADS, TM), lambda bi, i: (bi, 0, i)))
    return pl.pallas_call(
        _proj_kernel,
        grid=(b, nt),
        in_specs=[row(d), full(g), full(wf), full(wb), full(wt)],
        out_specs=out_specs,
        out_shape=out_shape,
        compiler_params=pltpu.CompilerParams(
            dimension_semantics=("arbitrary", "arbitrary"), vmem_limit_bytes=VMEM_LIMIT),
        name="proj",
    )(x, g, wf, wb, wt)


def _memkv_kernel(mem_ref, g_ref, w_ref, mk_ref, mv_ref):
    h = _rms_norm_f32(mem_ref[0], g_ref[...]).astype(BF16)
    mk_ref[0] = jnp.dot(h, w_ref[:, 0:MEM_WIDTH], preferred_element_type=F32).astype(BF16)
    mv_ref[0] = jnp.dot(h, w_ref[:, MEM_WIDTH:2 * MEM_WIDTH], preferred_element_type=F32).astype(BF16)


def _mem_kv(mem, g, w):
    b, m, d = mem.shape
    o = jax.ShapeDtypeStruct((b, m, MEM_WIDTH), BF16)
    return pl.pallas_call(
        _memkv_kernel,
        grid=(b,),
        in_specs=[pl.BlockSpec((1, m, d), lambda bi: (bi, 0, 0)),
                  pl.BlockSpec(g.shape, lambda bi: (0, 0)),
                  pl.BlockSpec(w.shape, lambda bi: (0, 0))],
        out_specs=(pl.BlockSpec((1, m, MEM_WIDTH), lambda bi: (bi, 0, 0)),) * 2,
        out_shape=(o, o),
        compiler_params=pltpu.CompilerParams(
            dimension_semantics=("arbitrary",), vmem_limit_bytes=VMEM_LIMIT),
        name="memkv",
    )(mem, g, w)


def _bucket_table():
    half = REL_BUCKETS // 2
    max_exact = half // 2
    delta = np.arange(N_NEAR + 1, dtype=np.int64)[:, None, None]
    key = np.arange(KT, dtype=np.int64)[None, :, None]
    qry = np.arange(TQ, dtype=np.int64)[None, None, :]
    rel = key - qry - delta * KT
    ret = np.where(rel > 0, half, 0)
    n = np.abs(rel)
    nf = np.maximum(n, 1).astype(np.float32)
    large = max_exact + (np.log(nf / np.float32(max_exact))
                         / np.float32(math.log(REL_MAX_DIST / max_exact))
                         * np.float32(half - max_exact)).astype(np.int32)
    large = np.minimum(large, half - 1)
    return (ret + np.where(n < max_exact, n, large)).astype(np.int32)


FAR_BUCKET = REL_BUCKETS // 2 - 1


def _bias_kernel(rb_ref, bk_ref, o_ref):
    h = pl.program_id(1)
    bk = bk_ref[0]
    far = rb_ref[FAR_BUCKET, h]
    acc = jnp.zeros((KT, TQ), F32)
    for bkt in range(REL_BUCKETS):
        acc = jnp.where(bk == bkt, (rb_ref[bkt, h] - far) * LOG2E, acc)
    o_ref[0, 0] = acc


def _bias_tiles(rel_bias):
    table = jnp.asarray(_bucket_table())
    return pl.pallas_call(
        _bias_kernel,
        grid=(N_NEAR + 1, N_HEADS),
        in_specs=[pl.BlockSpec(memory_space=pltpu.SMEM),
                  pl.BlockSpec((1, KT, TQ), lambda dl, h: (dl, 0, 0))],
        out_specs=pl.BlockSpec((1, 1, KT, TQ), lambda dl, h: (dl, h, 0, 0)),
        out_shape=jax.ShapeDtypeStruct((N_NEAR + 1, N_HEADS, KT, TQ), F32),
        compiler_params=pltpu.CompilerParams(dimension_semantics=("arbitrary", "arbitrary")),
        name="bias_tiles",
    )(rel_bias, table)


def _dsa_kernel(q_ref, iq_ref, iwt_ref, az_ref, ik2_ref, k_ref, vt_ref, bias_ref, o_ref,
                sc_ref, qw_ref, iqw_ref, acc_ref, m_ref, thr_ref,
                stage0_ref, stage1_ref, tmax0_ref, tmax1_ref):
    i = pl.program_id(1)
    n_tiles = i + 1

    lane = lax.broadcasted_iota(I32, (TQ, LANES), 1)
    for h in range(N_HEADS):
        j = h // 2
        sel = (lane >= HEAD_DIM) if (h % 2) else (lane < HEAD_DIM)
        qw_ref[h] = jnp.where(sel, q_ref[0, :, LANES * j:LANES * (j + 1)].astype(F32), 0.0).astype(BF16)
        iqw_ref[h] = jnp.where(sel, iq_ref[0, :, LANES * j:LANES * (j + 1)].astype(F32), 0.0).astype(BF16)

    key_l = lax.broadcasted_iota(I32, (KT, TQ), 0)
    qry_l = lax.broadcasted_iota(I32, (KT, TQ), 1)
    adm_diag = key_l < ((qry_l // CHUNK) + 1) * CHUNK

    def score_tile(t):
        r0 = pl.multiple_of(t * KT, KT)
        ikt = ik2_ref[0, pl.ds(r0, KT), :]
        acc = jnp.zeros((KT, TQ), F32)
        for h in range(IDX_HEADS):
            y = lax.dot_general(ikt, iqw_ref[h], _NT_DIMS, preferred_element_type=F32)
            acc = acc + jnp.maximum(y, 0.0) * iwt_ref[0, h:h + 1, :]
        adm = jnp.logical_or(jnp.logical_and(adm_diag, t == i), t < i)
        sc_ref[pl.ds(r0, KT), :] = jnp.where(adm, acc, NEG)

    n_steps = (n_tiles + 1) // 2

    def score_step(c, carry):
        score_tile(2 * c)
        score_tile(2 * c + 1)
        return carry

    lax.fori_loop(0, n_steps, score_step, 0)

    def count(pred):
        def body(c, acc):
            for k in range(COUNT_ROWS // ACC_ROWS):
                r0 = pl.multiple_of(c * COUNT_ROWS + k * ACC_ROWS, ACC_ROWS)
                x = sc_ref[pl.ds(r0, ACC_ROWS), :]
                acc = jnp.where(pred(x, r0), acc + 1.0, acc)
            return acc
        acc = lax.fori_loop(0, n_steps, body, jnp.zeros((ACC_ROWS, TQ), F32))
        return acc.sum(axis=0, keepdims=True)

    def to_float(u):
        bits = jnp.where(u < 0, u ^ I32(-2 ** 31), ~u)
        return lax.bitcast_convert_type(bits, F32)

    @pl.when(i == 0)
    def _():
        thr_ref[...] = jnp.full((1, TQ), HALF_NEG, F32)

    @pl.when(i > 0)
    def _():
        def bit_body(bi, u):
            cand = u | lax.shift_left(I32(1), I32(31) - bi)
            cf = to_float(cand)
            cnt = count(lambda x, r0: x >= cf)
            return jnp.where(cnt >= float(TOPK), cand, u)

        u = lax.fori_loop(0, 32, bit_body, jnp.zeros((1, TQ), I32))
        thr = to_float(u)
        thr_ref[...] = thr
        cnt_ge = count(lambda x, r0: x >= thr)

        @pl.when(jnp.max(cnt_ge) > float(TOPK))
        def _():
            need = float(TOPK) - count(lambda x, r0: x > thr)
            row = lax.broadcasted_iota(I32, (ACC_ROWS, TQ), 0)
            row_step = lax.broadcasted_iota(I32, (COUNT_ROWS, TQ), 0)
            n_bits = max(1, (sc_ref.shape[0] - 1).bit_length())

            def idx_body(bi, p):
                cand = p | lax.shift_left(I32(1), I32(n_bits - 1) - bi)
                cnt = count(lambda x, r0: jnp.logical_and(x == thr, row + r0 < cand))
                return jnp.where(cnt < need, cand, p)

            last = lax.fori_loop(0, n_bits, idx_body, jnp.zeros((1, TQ), I32))

            def drop(c, carry):
                r0 = pl.multiple_of(c * COUNT_ROWS, COUNT_ROWS)
                x = sc_ref[pl.ds(r0, COUNT_ROWS), :]
                gone = jnp.logical_and(x == thr, row_step + r0 > last)
                sc_ref[pl.ds(r0, COUNT_ROWS), :] = jnp.where(gone, NEG, x)
                return carry

            lax.fori_loop(0, n_steps, drop, 0)

    m_ref[...] = jnp.full(m_ref.shape, NEG, F32)
    acc_ref[...] = jnp.zeros(acc_ref.shape, F32)
    thr = thr_ref[...]

    def to_mask(c, carry):
        r0 = pl.multiple_of(c * KT, KT)
        sc_ref[pl.ds(r0, KT), :] = jnp.where(sc_ref[pl.ds(r0, KT), :] >= thr, F32(0.0), F32(NEG))
        return carry

    lax.fori_loop(0, n_tiles, to_mask, 0)

    stages = ((stage0_ref, tmax0_ref), (stage1_ref, tmax1_ref))

    def pass_a(t, slot):
        stage_ref, tmax_ref = stages[slot]
        r0 = pl.multiple_of(t * KT, KT)
        dl = jnp.minimum(i - t, N_NEAR)
        for h in range(N_HEADS):
            j = h // 2
            kk = k_ref[0, pl.ds(r0, KT), LANES * j:LANES * (j + 1)]
            s = lax.dot_general(kk, qw_ref[h], _NT_DIMS, preferred_element_type=F32)
            s = s + sc_ref[pl.ds(r0, KT), :] + bias_ref[dl, h]
            stage_ref[h] = s
            tmax_ref[h:h + 1, :] = s.reshape(KT // ACC_ROWS, ACC_ROWS, TQ).max(axis=0).max(
                axis=0, keepdims=True)

    def pass_b(t, slot):
        stage_ref, tmax_ref = stages[slot]
        m_old = m_ref[...]
        m_new = jnp.maximum(m_old, tmax_ref[...])
        alpha = jnp.exp2(m_old - m_new)
        m_ref[...] = m_new
        for h in range(N_HEADS):
            p = jnp.exp2(stage_ref[h] - m_new[h:h + 1, :])
            pv = jnp.dot(vt_ref[0, t, V_ROWS * h:V_ROWS * (h + 1), :], p.astype(BF16),
                         preferred_element_type=F32)
            acc_ref[V_ROWS * h:V_ROWS * (h + 1), :] = (
                alpha[h:h + 1, :] * acc_ref[V_ROWS * h:V_ROWS * (h + 1), :] + pv)

    pass_a(0, 0)
    n_pairs = (n_tiles - 1) // 2

    def attn_pair(pr, carry):
        t = 2 * pr
        pass_a(t + 1, 1)
        pass_b(t, 0)
        pass_a(t + 2, 0)
        pass_b(t + 1, 1)
        return carry

    lax.fori_loop(0, n_pairs, attn_pair, 0)
    t_last = 2 * n_pairs

    @pl.when(n_tiles - 1 > t_last)
    def _():
        pass_a(t_last + 1, 1)
        pass_b(t_last, 0)
        pass_b(t_last + 1, 1)

    @pl.when(n_tiles - 1 == t_last)
    def _():
        pass_b(t_last, 0)

    heads = []
    for h in range(N_HEADS):
        denom = acc_ref[V_ROWS * h + HEAD_DIM:V_ROWS * h + HEAD_DIM + 1, :]
        heads.append(acc_ref[V_ROWS * h:V_ROWS * h + HEAD_DIM, :] * (1.0 / denom))
    y = jnp.concatenate(heads, axis=0).T
    o_ref[0] = (y * jax.nn.silu(az_ref[0])).astype(BF16)


def _dsa(aq, iq, iwt, az, ik2, ak, avt, bias):
    b, s, _ = aq.shape
    nq = s // TQ
    qrow = lambda w: pl.BlockSpec((1, TQ, w), lambda bi, i: (bi, i, 0))
    once = pl.Buffered(1)
    return pl.pallas_call(
        _dsa_kernel,
        grid=(b, nq),
        in_specs=[qrow(ATTN_WIDTH), qrow(IDX_HEADS * IDX_DIM),
                  pl.BlockSpec((1, IDX_HEADS, TQ), lambda bi, i: (bi, 0, i)),
                  qrow(ATTN_WIDTH),
                  pl.BlockSpec((1, s, LANES), lambda bi, i: (bi, 0, 0), pipeline_mode=once),
                  pl.BlockSpec((1, s, ATTN_WIDTH), lambda bi, i: (bi, 0, 0), pipeline_mode=once),
                  pl.BlockSpec((1, s // KT, N_HEADS * V_ROWS, KT), lambda bi, i: (bi, 0, 0, 0),
                               pipeline_mode=once),
                  pl.BlockSpec(bias.shape, lambda bi, i: (0, 0, 0, 0), pipeline_mode=once)],
        out_specs=qrow(ATTN_WIDTH),
        out_shape=jax.ShapeDtypeStruct((b, s, ATTN_WIDTH), BF16),
        scratch_shapes=[pltpu.VMEM((s, TQ), F32),
                        pltpu.VMEM((N_HEADS, TQ, LANES), BF16),
                        pltpu.VMEM((IDX_HEADS, TQ, LANES), BF16),
                        pltpu.VMEM((N_HEADS * V_ROWS, TQ), F32),
                        pltpu.VMEM((N_HEADS, TQ), F32),
                        pltpu.VMEM((1, TQ), F32),
                        pltpu.VMEM((N_HEADS, KT, TQ), F32),
                        pltpu.VMEM((N_HEADS, KT, TQ), F32),
                        pltpu.VMEM((N_HEADS, TQ), F32),
                        pltpu.VMEM((N_HEADS, TQ), F32)],
        compiler_params=pltpu.CompilerParams(
            dimension_semantics=("arbitrary", "arbitrary"), vmem_limit_bytes=VMEM_LIMIT),
        name="dsa",
    )(aq, iq, iwt, az, ik2, ak, avt, bias)


def _merge_kernel(final, x_ref, pu_ref, halo_ref, pz_ref, mq_ref, mz_ref, ya_ref, gates_ref,
                  mk_ref, mv_ref, pw_ref, ps_ref, wb_ref, wo_ref, fg_ref, o_ref):
    i = pl.program_id(1)

    u = pu_ref[0]
    halo = jnp.where(i == 0, 0.0, halo_ref[0])
    ext = jnp.concatenate([halo, u], axis=0)
    t_glob = i * TM + lax.broadcasted_iota(I32, (TM, POOL_GC), 0)
    mixed = []
    for g in range(POOL_GROUPS):
        sl = slice(POOL_GC * g, POOL_GC * (g + 1))
        wsum = ext[:, sl]
        for step in range(g + 1):
            wsum = wsum + pltpu.roll(wsum, 2 ** step, axis=0)
        cnt = jnp.minimum(t_glob + 1, POOL_WINDOWS[g]).astype(F32)
        pooled = wsum[HALO:, :] / cnt - u[:, sl]
        mixed.append(jnp.dot(pooled.astype(BF16), pw_ref[g], preferred_element_type=F32))
    y_pool = jnp.concatenate(mixed, axis=1) * ps_ref[...] * jax.nn.silu(pz_ref[0])

    mem_out = []
    for h in range(MEM_HEADS):
        sl = slice(MEM_HEAD_DIM * h, MEM_HEAD_DIM * (h + 1))
        logits = lax.dot_general(mq_ref[0, :, sl], mk_ref[0, :, sl], _NT_DIMS,
                                 preferred_element_type=F32) * MEM_SCALE
        e = jnp.exp(logits - logits.max(axis=-1, keepdims=True))
        p = e / e.sum(axis=-1, keepdims=True)
        mem_out.append(jnp.dot(p.astype(BF16), mv_ref[0, :, sl], preferred_element_type=F32))
    y_mem = jnp.concatenate(mem_out, axis=1) * jax.nn.silu(mz_ref[0])

    branches = (y_pool.astype(BF16), ya_ref[0], y_mem.astype(BF16))
    merged = jnp.zeros((TM, D_MODEL), F32)
    for br in range(N_BRANCH):
        gate = jax.nn.sigmoid(gates_ref[0, :, D_MODEL * br:D_MODEL * (br + 1)])
        merged = merged + gate * jnp.dot(branches[br], wb_ref[br], preferred_element_type=F32)
    out = x_ref[0] + jnp.dot(merged.astype(BF16), wo_ref[...], preferred_element_type=F32)
    if final:
        out = _rms_norm_f32(out, fg_ref[...])
    o_ref[0] = out


def _merge(final, x, pu, pz, mq, mz, ya, gates, mk, mv, pw, ps, wb, wo, fg):
    b, s, d = x.shape
    row = lambda w: pl.BlockSpec((1, TM, w), lambda bi, i: (bi, i, 0))
    full = lambda a: pl.BlockSpec(a.shape, lambda bi, i: (0,) * a.ndim)
    per_b = lambda a: pl.BlockSpec((1,) + a.shape[1:], lambda bi, i: (bi,) + (0,) * (a.ndim - 1))
    halo = pl.BlockSpec((1, HALO, POOL_WIDTH),
                        lambda bi, i: (bi, jnp.maximum(i * (TM // HALO) - 1, 0), 0))
    return pl.pallas_call(
        functools.partial(_merge_kernel, final),
        grid=(b, s // TM),
        in_specs=[row(d), row(POOL_WIDTH), halo, row(POOL_WIDTH), row(MEM_WIDTH), row(MEM_WIDTH),
                  row(ATTN_WIDTH), row(N_BRANCH * D_MODEL), per_b(mk), per_b(mv),
                  full(pw), full(ps), full(wb), full(wo), full(fg)],
        out_specs=row(d),
        out_shape=jax.ShapeDtypeStruct((b, s, d), F32),
        compiler_params=pltpu.CompilerParams(
            dimension_semantics=("arbitrary", "arbitrary"), vmem_limit_bytes=VMEM_LIMIT),
        name="merge_final" if final else "merge",
    )(x, pu, pu, pz, mq, mz, ya, gates, mk, mv, pw, ps, wb, wo, fg)


def _split_w_in(w):
    col = lambda lo, hi: w[:, lo:hi]
    wf = jnp.concatenate([col(O_PU, O_PZ), col(O_PZ, O_AQ), col(O_AZ, O_IQ), col(O_MZ, O_G),
                          col(O_G, O_END)], axis=1)
    wb = jnp.concatenate([col(O_AQ, O_AK) * (ATTN_SCALE * LOG2E), col(O_AK, O_AV), col(O_IQ, O_IK),
                          col(O_MQ, O_MZ), col(O_IK, O_IW), col(O_IK, O_IW)], axis=1)
    wt = jnp.concatenate([col(O_AV, O_AZ), col(O_IW, O_MQ),
                          jnp.zeros((w.shape[0], WT_ROWS - ATTN_WIDTH - IDX_HEADS), w.dtype)], axis=1).T
    return wf.astype(BF16), wb.astype(BF16), wt.astype(BF16)


def kernel(x, mem, norm_g, w_in, pool_w, pool_scale, mem_norm_g, w_mem_kv, w_branch, w_out,
           rel_bias, final_g):
    depth = w_in.shape[0]
    assert x.shape[1] % COUNT_ROWS == 0 and x.shape[2] == D_MODEL and mem.shape[1] == MEM_LEN
    assert w_in.shape[2] == O_END and TQ == KT == TM
    bias = _bias_tiles(rel_bias)
    fg = final_g.reshape(1, D_MODEL)
    for l in range(depth):
        wf, wb, wt = _split_w_in(w_in[l])
        (pu, pz, az, mz, gates, aq, ak, iq, mq, ik2, avt, iwt) = _project(
            x, norm_g[l].reshape(1, D_MODEL), wf, wb, wt)
        mk, mv = _mem_kv(mem, mem_norm_g[l].reshape(1, D_MODEL), w_mem_kv[l].astype(BF16))
        ya = _dsa(aq, iq, iwt, az, ik2, ak, avt, bias)
        x = _merge(l == depth - 1, x, pu, pz, mq, mz, ya, gates, mk, mv,
                   pool_w[l].astype(BF16), pool_scale[l].reshape(1, POOL_WIDTH),
                   w_branch[l].astype(BF16), w_out[l].astype(BF16), fg)
    return x
```

```python
import functools
import math

import numpy as np
import jax
import jax.numpy as jnp
from jax import lax
from jax.experimental import pallas as pl
from jax.experimental.pallas import tpu as pltpu

F32 = jnp.float32
BF16 = jnp.bfloat16
I32 = jnp.int32

D_MODEL = 1024
CHUNK = 64
MEM_LEN = 256
POOL_GROUPS = 4
POOL_WINDOWS = (2, 4, 8, 16)
POOL_WIDTH = D_MODEL // 2
POOL_GC = POOL_WIDTH // POOL_GROUPS
N_HEADS = 8
HEAD_DIM = 64
ATTN_WIDTH = N_HEADS * HEAD_DIM
IDX_HEADS = 8
IDX_DIM = 64
TOPK = 256
ATTN_SCALE = HEAD_DIM ** -0.5
LOG2E = math.log2(math.e)
IDX_W_SCALE = (IDX_DIM ** -0.5) * (IDX_HEADS ** -0.5)
MEM_HEADS = 4
MEM_HEAD_DIM = 128
MEM_WIDTH = MEM_HEADS * MEM_HEAD_DIM
MEM_SCALE = MEM_HEAD_DIM ** -0.5
N_BRANCH = 3
REL_BUCKETS = 32
REL_MAX_DIST = 1024
EPS = 1e-6
NEG = -1e30
HALF_NEG = -5e29

IN_SPLITS = (POOL_WIDTH, POOL_WIDTH, ATTN_WIDTH, ATTN_WIDTH, ATTN_WIDTH, ATTN_WIDTH,
             IDX_HEADS * IDX_DIM, IDX_DIM, IDX_HEADS, MEM_WIDTH, MEM_WIDTH, N_BRANCH * D_MODEL)
_OFF = np.concatenate([[0], np.cumsum(IN_SPLITS)]).tolist()
(O_PU, O_PZ, O_AQ, O_AK, O_AV, O_AZ, O_IQ, O_IK, O_IW, O_MQ, O_MZ, O_G, O_END) = _OFF

LANES = 128
SUBLANES = 8
TQ = 256
KT = 256
TM = 256
HALO = 16
N_NEAR = 4
COUNT_ROWS = 2 * KT
ACC_ROWS = 4 * SUBLANES
TOP_ROWS = 8 * SUBLANES
WT_ROWS = 528
V_ROWS = HEAD_DIM + 16
VMEM_LIMIT = 56 * 1024 * 1024

_NT_DIMS = (((1,), (1,)), ((), ()))


def _rms_norm_f32(x, g):
    ms = jnp.mean(x * x, axis=-1, keepdims=True)
    return x * lax.rsqrt(ms + EPS) * g


def _proj_kernel(x_ref, g_ref, wf_ref, wb_ref, wt_ref,
                 pu_ref, pz_ref, az_ref, mz_ref, gates_ref,
                 aq_ref, ak_ref, iq_ref, mq_ref, ik2_ref, avt_ref, iwt_ref):
    h = _rms_norm_f32(x_ref[0], g_ref[...]).astype(BF16)

    def mm(w_ref, lo, hi):
        return jnp.dot(h, w_ref[:, lo:hi], preferred_element_type=F32)

    pu_ref[0] = mm(wf_ref, 0, 512)
    pz_ref[0] = mm(wf_ref, 512, 1024)
    az_ref[0] = mm(wf_ref, 1024, 1536)
    mz_ref[0] = mm(wf_ref, 1536, 2048)
    for c in range(N_BRANCH * D_MODEL // 512):
        gates_ref[0, :, 512 * c:512 * (c + 1)] = mm(wf_ref, 2048 + 512 * c, 2048 + 512 * (c + 1))
    aq_ref[0] = mm(wb_ref, 0, 512).astype(BF16)
    ak_ref[0] = mm(wb_ref, 512, 1024).astype(BF16)
    iq_ref[0] = mm(wb_ref, 1024, 1536).astype(BF16)
    mq_ref[0] = mm(wb_ref, 1536, 2048).astype(BF16)
    ik2_ref[0] = mm(wb_ref, 2048, 2176).astype(BF16)
    t = lax.dot_general(wt_ref[...], h, _NT_DIMS, preferred_element_type=F32)
    pad_row = lax.broadcasted_iota(I32, (V_ROWS - HEAD_DIM, TM), 0)
    pad = jnp.where(pad_row == 0, 1.0, 0.0).astype(BF16)
    for hd in range(N_HEADS):
        avt_ref[0, 0, V_ROWS * hd:V_ROWS * hd + HEAD_DIM, :] = (
            t[HEAD_DIM * hd:HEAD_DIM * (hd + 1)].astype(BF16))
        avt_ref[0, 0, V_ROWS * hd + HEAD_DIM:V_ROWS * (hd + 1), :] = pad
    iwt_ref[0] = t[ATTN_WIDTH:ATTN_WIDTH + IDX_HEADS] * IDX_W_SCALE


def _project(x, g, wf, wb, wt):
    b, s, d = x.shape
    nt = s // TM
    row = lambda w: pl.BlockSpec((1, TM, w), lambda bi, i: (bi, i, 0))
    full = lambda a: pl.BlockSpec(a.shape, lambda bi, i: (0,) * a.ndim)
    f32o = lambda w: jax.ShapeDtypeStruct((b, s, w), F32)
    b16o = lambda w: jax.ShapeDtypeStruct((b, s, w), BF16)
    out_shape = (f32o(512), f32o(512), f32o(512), f32o(512), f32o(N_BRANCH * D_MODEL),
                 b16o(512), b16o(512), b16o(512), b16o(512), b16o(128),
                 jax.ShapeDtypeStruct((b, nt, N_HEADS * V_ROWS, TM), BF16),
                 jax.ShapeDtypeStruct((b, IDX_HEADS, s), F32))
    out_specs = (row(512), row(512), row(512), row(512), row(N_BRANCH * D_MODEL),
                 row(512), row(512), row(512), row(512), row(128),
                 pl.BlockSpec((1, 1, N_HEADS * V_ROWS, TM), lambda bi, i: (bi, i, 0, 0)),
                 pl.BlockSpec((1, IDX_HEADS, TM), lambda bi, i: (bi, 0, i)))
    return pl.pallas_call(
        _proj_kernel,
        grid=(b, nt),
        in_specs=[row(d), full(g), full(wf), full(wb), full(wt)],
        out_specs=out_specs,
        out_shape=out_shape,
        compiler_params=pltpu.CompilerParams(
            dimension_semantics=("arbitrary", "arbitrary"), vmem_limit_bytes=VMEM_LIMIT),
        name="proj",
    )(x, g, wf, wb, wt)


def _memkv_kernel(mem_ref, g_ref, w_ref, mk_ref, mv_ref):
    h = _rms_norm_f32(mem_ref[0], g_ref[...]).astype(BF16)
    mk_ref[0] = jnp.dot(h, w_ref[:, 0:MEM_WIDTH], preferred_element_type=F32).astype(BF16)
    mv_ref[0] = jnp.dot(h, w_ref[:, MEM_WIDTH:2 * MEM_WIDTH], preferred_element_type=F32).astype(BF16)


def _mem_kv(mem, g, w):
    b, m, d = mem.shape
    o = jax.ShapeDtypeStruct((b, m, MEM_WIDTH), BF16)
    return pl.pallas_call(
        _memkv_kernel,
        grid=(b,),
        in_specs=[pl.BlockSpec((1, m, d), lambda bi: (bi, 0, 0)),
                  pl.BlockSpec(g.shape, lambda bi: (0, 0)),
                  pl.BlockSpec(w.shape, lambda bi: (0, 0))],
        out_specs=(pl.BlockSpec((1, m, MEM_WIDTH), lambda bi: (bi, 0, 0)),) * 2,
        out_shape=(o, o),
        compiler_params=pltpu.CompilerParams(
            dimension_semantics=("arbitrary",), vmem_limit_bytes=VMEM_LIMIT),
        name="memkv",
    )(mem, g, w)


def _bucket_table():
    half = REL_BUCKETS // 2
    max_exact = half // 2
    delta = np.arange(N_NEAR + 1, dtype=np.int64)[:, None, None]
    key = np.arange(KT, dtype=np.int64)[None, :, None]
    qry = np.arange(TQ, dtype=np.int64)[None, None, :]
    rel = key - qry - delta * KT
    ret = np.where(rel > 0, half, 0)
    n = np.abs(rel)
    nf = np.maximum(n, 1).astype(np.float32)
    large = max_exact + (np.log(nf / np.float32(max_exact))
                         / np.float32(math.log(REL_MAX_DIST / max_exact))
                         * np.float32(half - max_exact)).astype(np.int32)
    large = np.minimum(large, half - 1)
    return (ret + np.where(n < max_exact, n, large)).astype(np.int32)


FAR_BUCKET = REL_BUCKETS // 2 - 1


def _bias_kernel(rb_ref, bk_ref, o_ref):
    h = pl.program_id(1)
    bk = bk_ref[0]
    far = rb_ref[FAR_BUCKET, h]
    acc = jnp.zeros((KT, TQ), F32)
    for bkt in range(REL_BUCKETS):
        acc = jnp.where(bk == bkt, (rb_ref[bkt, h] - far) * LOG2E, acc)
    o_ref[0, 0] = acc


def _bias_tiles(rel_bias):
    table = jnp.asarray(_bucket_table())
    return pl.pallas_call(
        _bias_kernel,
        grid=(N_NEAR + 1, N_HEADS),
        in_specs=[pl.BlockSpec(memory_space=pltpu.SMEM),
                  pl.BlockSpec((1, KT, TQ), lambda dl, h: (dl, 0, 0))],
        out_specs=pl.BlockSpec((1, 1, KT, TQ), lambda dl, h: (dl, h, 0, 0)),
        out_shape=jax.ShapeDtypeStruct((N_NEAR + 1, N_HEADS, KT, TQ), F32),
        compiler_params=pltpu.CompilerParams(dimension_semantics=("arbitrary", "arbitrary")),
        name="bias_tiles",
    )(rel_bias, table)


def _dsa_kernel(q_ref, iq_ref, iwt_ref, az_ref, ik2_ref, k_ref, vt_ref, bias_ref, o_ref,
                sc_ref, hi_ref, qw_ref, iqw_ref, acc_ref, m_ref, thr_ref,
                stage0_ref, stage1_ref, tmax0_ref, tmax1_ref):
    i = pl.program_id(1)
    n_tiles = i + 1

    lane = lax.broadcasted_iota(I32, (TQ, LANES), 1)
    for h in range(N_HEADS):
        j = h // 2
        sel = (lane >= HEAD_DIM) if (h % 2) else (lane < HEAD_DIM)
        qw_ref[h] = jnp.where(sel, q_ref[0, :, LANES * j:LANES * (j + 1)].astype(F32), 0.0).astype(BF16)
        iqw_ref[h] = jnp.where(sel, iq_ref[0, :, LANES * j:LANES * (j + 1)].astype(F32), 0.0).astype(BF16)

    key_l = lax.broadcasted_iota(I32, (KT, TQ), 0)
    qry_l = lax.broadcasted_iota(I32, (KT, TQ), 1)
    adm_diag = key_l < ((qry_l // CHUNK) + 1) * CHUNK

    def score_tile(t):
        r0 = pl.multiple_of(t * KT, KT)
        ikt = ik2_ref[0, pl.ds(r0, KT), :]
        acc = jnp.zeros((KT, TQ), F32)
        for h in range(IDX_HEADS):
            y = lax.dot_general(ikt, iqw_ref[h], _NT_DIMS, preferred_element_type=F32)
            acc = acc + jnp.maximum(y, 0.0) * iwt_ref[0, h:h + 1, :]
        adm = jnp.logical_or(jnp.logical_and(adm_diag, t == i), t < i)
        val = jnp.where(adm, acc, NEG)
        sc_ref[pl.ds(r0, KT), :] = val
        top = lax.bitcast_convert_type(val, I32) & I32(-(1 << 16))
        hi_ref[pl.ds(r0, KT), :] = lax.bitcast_convert_type(top, F32).astype(BF16)

    n_steps = (n_tiles + 1) // 2

    def score_step(c, carry):
        score_tile(2 * c)
        score_tile(2 * c + 1)
        return carry

    lax.fori_loop(0, n_steps, score_step, 0)

    def count(pred):
        def body(c, acc):
            for k in range(COUNT_ROWS // ACC_ROWS):
                r0 = pl.multiple_of(c * COUNT_ROWS + k * ACC_ROWS, ACC_ROWS)
                x = sc_ref[pl.ds(r0, ACC_ROWS), :]
                acc = jnp.where(pred(x, r0), acc + 1.0, acc)
            return acc
        acc = lax.fori_loop(0, n_steps, body, jnp.zeros((ACC_ROWS, TQ), F32))
        return acc.sum(axis=0, keepdims=True)

    def to_float(u):
        bits = jnp.where(u < 0, u ^ I32(-2 ** 31), ~u)
        return lax.bitcast_convert_type(bits, F32)

    @pl.when(i == 0)
    def _():
        thr_ref[...] = jnp.full((1, TQ), HALF_NEG, F32)

    def count_top(cand_bf16):
        cb = jnp.broadcast_to(cand_bf16, (TOP_ROWS, TQ))

        def body(c, acc):
            for k in range(COUNT_ROWS // TOP_ROWS):
                r0 = pl.multiple_of(c * COUNT_ROWS + k * TOP_ROWS, TOP_ROWS)
                acc = jnp.where(hi_ref[pl.ds(r0, TOP_ROWS), :] >= cb, acc + 1.0, acc)
            return acc
        acc = lax.fori_loop(0, n_steps, body, jnp.zeros((TOP_ROWS, TQ), BF16))
        return acc.astype(F32).sum(axis=0, keepdims=True)

    @pl.when(i > 0)
    def _():
        def top_body(bi, carry):
            u, cnt_u = carry
            cand = u | lax.shift_left(I32(1 << 16), I32(15) - bi)
            top = lax.bitcast_convert_type(to_float(cand), I32) & I32(-(1 << 16))
            cnt = count_top(lax.bitcast_convert_type(top, F32).astype(BF16))
            keep = cnt >= float(TOPK)
            return jnp.where(keep, cand, u), jnp.where(keep, cnt, cnt_u)

        def bit_body(bi, carry):
            u, cnt_u = carry
            cand = u | lax.shift_left(I32(1), I32(15) - bi)
            cf = to_float(cand)
            cnt = count(lambda x, r0: x >= cf)
            keep = cnt >= float(TOPK)
            return jnp.where(keep, cand, u), jnp.where(keep, cnt, cnt_u)

        carry = (jnp.zeros((1, TQ), I32), jnp.zeros((1, TQ), F32))
        carry = lax.fori_loop(0, 16, top_body, carry)
        u, cnt_ge = lax.fori_loop(0, 16, bit_body, carry)
        thr = to_float(u)
        thr_ref[...] = thr

        @pl.when(jnp.max(cnt_ge) > float(TOPK))
        def _():
            need = float(TOPK) - count(lambda x, r0: x > thr)
            row = lax.broadcasted_iota(I32, (ACC_ROWS, TQ), 0)
            row_step = lax.broadcasted_iota(I32, (COUNT_ROWS, TQ), 0)
            n_bits = max(1, (sc_ref.shape[0] - 1).bit_length())

            def idx_body(bi, p):
                cand = p | lax.shift_left(I32(1), I32(n_bits - 1) - bi)
                cnt = count(lambda x, r0: jnp.logical_and(x == thr, row + r0 < cand))
                return jnp.where(cnt < need, cand, p)

            last = lax.fori_loop(0, n_bits, idx_body, jnp.zeros((1, TQ), I32))

            def drop(c, carry):
                r0 = pl.multiple_of(c * COUNT_ROWS, COUNT_ROWS)
                x = sc_ref[pl.ds(r0, COUNT_ROWS), :]
                gone = jnp.logical_and(x == thr, row_step + r0 > last)
                sc_ref[pl.ds(r0, COUNT_ROWS), :] = jnp.where(gone, NEG, x)
                return carry

            lax.fori_loop(0, n_steps, drop, 0)

    m_ref[...] = jnp.full(m_ref.shape, NEG, F32)
    acc_ref[...] = jnp.zeros(acc_ref.shape, F32)
    thr = thr_ref[...]

    def to_mask(c, carry):
        r0 = pl.multiple_of(c * KT, KT)
        sc_ref[pl.ds(r0, KT), :] = jnp.where(sc_ref[pl.ds(r0, KT), :] >= thr, F32(0.0), F32(NEG))
        return carry

    lax.fori_loop(0, n_tiles, to_mask, 0)

    stages = ((stage0_ref, tmax0_ref), (stage1_ref, tmax1_ref))

    def pass_a(t, slot):
        stage_ref, tmax_ref = stages[slot]
        r0 = pl.multiple_of(t * KT, KT)
        dl = jnp.minimum(i - t, N_NEAR)
        for h in range(N_HEADS):
            j = h // 2
            kk = k_ref[0, pl.ds(r0, KT), LANES * j:LANES * (j + 1)]
            s = lax.dot_general(kk, qw_ref[h], _NT_DIMS, preferred_element_type=F32)
            s = s + sc_ref[pl.ds(r0, KT), :] + bias_ref[dl, h]
            stage_ref[h] = s
            tmax_ref[h:h + 1, :] = s.reshape(KT // ACC_ROWS, ACC_ROWS, TQ).max(axis=0).max(
                axis=0, keepdims=True)

    def pass_b(t, slot):
        stage_ref, tmax_ref = stages[slot]
        m_old = m_ref[...]
        m_new = jnp.maximum(m_old, tmax_ref[...])
        alpha = jnp.exp2(m_old - m_new)
        m_ref[...] = m_new
        for h in range(N_HEADS):
            p = jnp.exp2(stage_ref[h] - m_new[h:h + 1, :])
            pv = jnp.dot(vt_ref[0, t, V_ROWS * h:V_ROWS * (h + 1), :], p.astype(BF16),
                         preferred_element_type=F32)
            acc_ref[V_ROWS * h:V_ROWS * (h + 1), :] = (
                alpha[h:h + 1, :] * acc_ref[V_ROWS * h:V_ROWS * (h + 1), :] + pv)

    pass_a(0, 0)
    n_pairs = (n_tiles - 1) // 2

    def attn_pair(pr, carry):
        t = 2 * pr
        pass_a(t + 1, 1)
        pass_b(t, 0)
        pass_a(t + 2, 0)
        pass_b(t + 1, 1)
        return carry

    lax.fori_loop(0, n_pairs, attn_pair, 0)
    t_last = 2 * n_pairs

    @pl.when(n_tiles - 1 > t_last)
    def _():
        pass_a(t_last + 1, 1)
        pass_b(t_last, 0)
        pass_b(t_last + 1, 1)

    @pl.when(n_tiles - 1 == t_last)
    def _():
        pass_b(t_last, 0)

    heads = []
    for h in range(N_HEADS):
        denom = acc_ref[V_ROWS * h + HEAD_DIM:V_ROWS * h + HEAD_DIM + 1, :]
        heads.append(acc_ref[V_ROWS * h:V_ROWS * h + HEAD_DIM, :] * (1.0 / denom))
    y = jnp.concatenate(heads, axis=0).T
    o_ref[0] = (y * jax.nn.silu(az_ref[0])).astype(BF16)


def _dsa(aq, iq, iwt, az, ik2, ak, avt, bias):
    b, s, _ = aq.shape
    nq = s // TQ
    qrow = lambda w: pl.BlockSpec((1, TQ, w), lambda bi, i: (bi, i, 0))
    once = pl.Buffered(1)
    return pl.pallas_call(
        _dsa_kernel,
        grid=(b, nq),
        in_specs=[qrow(ATTN_WIDTH), qrow(IDX_HEADS * IDX_DIM),
                  pl.BlockSpec((1, IDX_HEADS, TQ), lambda bi, i: (bi, 0, i)),
                  qrow(ATTN_WIDTH),
                  pl.BlockSpec((1, s, LANES), lambda bi, i: (bi, 0, 0), pipeline_mode=once),
                  pl.BlockSpec((1, s, ATTN_WIDTH), lambda bi, i: (bi, 0, 0), pipeline_mode=once),
                  pl.BlockSpec((1, s // KT, N_HEADS * V_ROWS, KT), lambda bi, i: (bi, 0, 0, 0),
                               pipeline_mode=once),
                  pl.BlockSpec(bias.shape, lambda bi, i: (0, 0, 0, 0), pipeline_mode=once)],
        out_specs=qrow(ATTN_WIDTH),
        out_shape=jax.ShapeDtypeStruct((b, s, ATTN_WIDTH), BF16),
        scratch_shapes=[pltpu.VMEM((s, TQ), F32),
                        pltpu.VMEM((s, TQ), BF16),
                        pltpu.VMEM((N_HEADS, TQ, LANES), BF16),
                        pltpu.VMEM((IDX_HEADS, TQ, LANES), BF16),
                        pltpu.VMEM((N_HEADS * V_ROWS, TQ), F32),
                        pltpu.VMEM((N_HEADS, TQ), F32),
                        pltpu.VMEM((1, TQ), F32),
                        pltpu.VMEM((N_HEADS, KT, TQ), F32),
                        pltpu.VMEM((N_HEADS, KT, TQ), F32),
                        pltpu.VMEM((N_HEADS, TQ), F32),
                        pltpu.VMEM((N_HEADS, TQ), F32)],
        compiler_params=pltpu.CompilerParams(
            dimension_semantics=("arbitrary", "arbitrary"), vmem_limit_bytes=VMEM_LIMIT),
        name="dsa",
    )(aq, iq, iwt, az, ik2, ak, avt, bias)


def _merge_kernel(final, x_ref, pu_ref, halo_ref, pz_ref, mq_ref, mz_ref, ya_ref, gates_ref,
                  mk_ref, mv_ref, pw_ref, ps_ref, wb_ref, wo_ref, fg_ref, o_ref):
    i = pl.program_id(1)

    u = pu_ref[0]
    halo = jnp.where(i == 0, 0.0, halo_ref[0])
    ext = jnp.concatenate([halo, u], axis=0)
    t_glob = i * TM + lax.broadcasted_iota(I32, (TM, POOL_GC), 0)
    mixed = []
    for g in range(POOL_GROUPS):
        sl = slice(POOL_GC * g, POOL_GC * (g + 1))
        wsum = ext[:, sl]
        for step in range(g + 1):
            wsum = wsum + pltpu.roll(wsum, 2 ** step, axis=0)
        cnt = jnp.minimum(t_glob + 1, POOL_WINDOWS[g]).astype(F32)
        pooled = wsum[HALO:, :] / cnt - u[:, sl]
        mixed.append(jnp.dot(pooled.astype(BF16), pw_ref[g], preferred_element_type=F32))
    y_pool = jnp.concatenate(mixed, axis=1) * ps_ref[...] * jax.nn.silu(pz_ref[0])

    mem_out = []
    for h in range(MEM_HEADS):
        sl = slice(MEM_HEAD_DIM * h, MEM_HEAD_DIM * (h + 1))
        logits = lax.dot_general(mq_ref[0, :, sl], mk_ref[0, :, sl], _NT_DIMS,
                                 preferred_element_type=F32) * MEM_SCALE
        e = jnp.exp(logits - logits.max(axis=-1, keepdims=True))
        p = e / e.sum(axis=-1, keepdims=True)
        mem_out.append(jnp.dot(p.astype(BF16), mv_ref[0, :, sl], preferred_element_type=F32))
    y_mem = jnp.concatenate(mem_out, axis=1) * jax.nn.silu(mz_ref[0])

    branches = (y_pool.astype(BF16), ya_ref[0], y_mem.astype(BF16))
    merged = jnp.zeros((TM, D_MODEL), F32)
    for br in range(N_BRANCH):
        gate = jax.nn.sigmoid(gates_ref[0, :, D_MODEL * br:D_MODEL * (br + 1)])
        merged = merged + gate * jnp.dot(branches[br], wb_ref[br], preferred_element_type=F32)
    out = x_ref[0] + jnp.dot(merged.astype(BF16), wo_ref[...], preferred_element_type=F32)
    if final:
        out = _rms_norm_f32(out, fg_ref[...])
    o_ref[0] = out


def _merge(final, x, pu, pz, mq, mz, ya, gates, mk, mv, pw, ps, wb, wo, fg):
    b, s, d = x.shape
    row = lambda w: pl.BlockSpec((1, TM, w), lambda bi, i: (bi, i, 0))
    full = lambda a: pl.BlockSpec(a.shape, lambda bi, i: (0,) * a.ndim)
    per_b = lambda a: pl.BlockSpec((1,) + a.shape[1:], lambda bi, i: (bi,) + (0,) * (a.ndim - 1))
    halo = pl.BlockSpec((1, HALO, POOL_WIDTH),
                        lambda bi, i: (bi, jnp.maximum(i * (TM // HALO) - 1, 0), 0))
    return pl.pallas_call(
        functools.partial(_merge_kernel, final),
        grid=(b, s // TM),
        in_specs=[row(d), row(POOL_WIDTH), halo, row(POOL_WIDTH), row(MEM_WIDTH), row(MEM_WIDTH),
                  row(ATTN_WIDTH), row(N_BRANCH * D_MODEL), per_b(mk), per_b(mv),
                  full(pw), full(ps), full(wb), full(wo), full(fg)],
        out_specs=row(d),
        out_shape=jax.ShapeDtypeStruct((b, s, d), F32),
        compiler_params=pltpu.CompilerParams(
            dimension_semantics=("arbitrary", "arbitrary"), vmem_limit_bytes=VMEM_LIMIT),
        name="merge_final" if final else "merge",
    )(x, pu, pu, pz, mq, mz, ya, gates, mk, mv, pw, ps, wb, wo, fg)


def _split_w_in(w):
    col = lambda lo, hi: w[:, lo:hi]
    wf = jnp.concatenate([col(O_PU, O_PZ), col(O_PZ, O_AQ), col(O_AZ, O_IQ), col(O_MZ, O_G),
                          col(O_G, O_END)], axis=1)
    wb = jnp.concatenate([col(O_AQ, O_AK) * (ATTN_SCALE * LOG2E), col(O_AK, O_AV), col(O_IQ, O_IK),
                          col(O_MQ, O_MZ), col(O_IK, O_IW), col(O_IK, O_IW)], axis=1)
    wt = jnp.concatenate([col(O_AV, O_AZ), col(O_IW, O_MQ),
                          jnp.zeros((w.shape[0], WT_ROWS - ATTN_WIDTH - IDX_HEADS), w.dtype)], axis=1).T
    return wf.astype(BF16), wb.astype(BF16), wt.astype(BF16)


def kernel(x, mem, norm_g, w_in, pool_w, pool_scale, mem_norm_g, w_mem_kv, w_branch, w_out,
           rel_bias, final_g):
    depth = w_in.shape[0]
    assert x.shape[1] % COUNT_ROWS == 0 and x.shape[2] == D_MODEL and mem.shape[1] == MEM_LEN
    assert w_in.shape[2] == O_END and TQ == KT == TM
    assert x.shape[1] // TOP_ROWS <= 256
    bias = _bias_tiles(rel_bias)
    fg = final_g.reshape(1, D_MODEL)
    for l in range(depth):
        wf, wb, wt = _split_w_in(w_in[l])
        (pu, pz, az, mz, gates, aq, ak, iq, mq, ik2, avt, iwt) = _project(
            x, norm_g[l].reshape(1, D_MODEL), wf, wb, wt)
        mk, mv = _mem_kv(mem, mem_norm_g[l].reshape(1, D_MODEL), w_mem_kv[l].astype(BF16))
        ya = _dsa(aq, iq, iwt, az, ik2, ak, avt, bias)
        x = _merge(l == depth - 1, x, pu, pz, mq, mz, ya, gates, mk, mv,
                   pool_w[l].astype(BF16), pool_scale[l].reshape(1, POOL_WIDTH),
                   w_branch[l].astype(BF16), w_out[l].astype(BF16), fg)
    return x
```

```python
import functools
import math

import numpy as np
import jax
import jax.numpy as jnp
from jax import lax
from jax.experimental import pallas as pl
from jax.experimental.pallas import tpu as pltpu

F32 = jnp.float32
BF16 = jnp.bfloat16
I32 = jnp.int32

D_MODEL = 1024
CHUNK = 64
MEM_LEN = 256
POOL_GROUPS = 4
POOL_WINDOWS = (2, 4, 8, 16)
POOL_WIDTH = D_MODEL // 2
POOL_GC = POOL_WIDTH // POOL_GROUPS
N_HEADS = 8
HEAD_DIM = 64
ATTN_WIDTH = N_HEADS * HEAD_DIM
IDX_HEADS = 8
IDX_DIM = 64
TOPK = 256
ATTN_SCALE = HEAD_DIM ** -0.5
LOG2E = math.log2(math.e)
IDX_W_SCALE = (IDX_DIM ** -0.5) * (IDX_HEADS ** -0.5)
MEM_HEADS = 4
MEM_HEAD_DIM = 128
MEM_WIDTH = MEM_HEADS * MEM_HEAD_DIM
MEM_SCALE = MEM_HEAD_DIM ** -0.5
N_BRANCH = 3
REL_BUCKETS = 32
REL_MAX_DIST = 1024
EPS = 1e-6
NEG = -1e30
HALF_NEG = -5e29

IN_SPLITS = (POOL_WIDTH, POOL_WIDTH, ATTN_WIDTH, ATTN_WIDTH, ATTN_WIDTH, ATTN_WIDTH,
             IDX_HEADS * IDX_DIM, IDX_DIM, IDX_HEADS, MEM_WIDTH, MEM_WIDTH, N_BRANCH * D_MODEL)
_OFF = np.concatenate([[0], np.cumsum(IN_SPLITS)]).tolist()
(O_PU, O_PZ, O_AQ, O_AK, O_AV, O_AZ, O_IQ, O_IK, O_IW, O_MQ, O_MZ, O_G, O_END) = _OFF

LANES = 128
SUBLANES = 8
TQ = 256
KT = 256
TM = 256
HALO = 16
N_NEAR = 4
COUNT_ROWS = 2 * KT
ACC_ROWS = 4 * SUBLANES
TOP_ROWS = 8 * SUBLANES
GROUP_ROWS = 128
CROWS = COUNT_ROWS // GROUP_ROWS * SUBLANES
WT_ROWS = 528
V_ROWS = HEAD_DIM + 16
VMEM_LIMIT = 56 * 1024 * 1024

_NT_DIMS = (((1,), (1,)), ((), ()))


def _rms_norm_f32(x, g):
    ms = jnp.mean(x * x, axis=-1, keepdims=True)
    return x * lax.rsqrt(ms + EPS) * g


def _proj_kernel(x_ref, g_ref, wf_ref, wb_ref, wt_ref,
                 pu_ref, pz_ref, az_ref, mz_ref, gates_ref,
                 aq_ref, ak_ref, iq_ref, mq_ref, ik2_ref, avt_ref, iwt_ref):
    h = _rms_norm_f32(x_ref[0], g_ref[...]).astype(BF16)

    def mm(w_ref, lo, hi):
        return jnp.dot(h, w_ref[:, lo:hi], preferred_element_type=F32)

    pu_ref[0] = mm(wf_ref, 0, 512)
    pz_ref[0] = mm(wf_ref, 512, 1024)
    az_ref[0] = mm(wf_ref, 1024, 1536)
    mz_ref[0] = mm(wf_ref, 1536, 2048)
    for c in range(N_BRANCH * D_MODEL // 512):
        gates_ref[0, :, 512 * c:512 * (c + 1)] = mm(wf_ref, 2048 + 512 * c, 2048 + 512 * (c + 1))
    aq_ref[0] = mm(wb_ref, 0, 512).astype(BF16)
    ak_ref[0] = mm(wb_ref, 512, 1024).astype(BF16)
    iq_ref[0] = mm(wb_ref, 1024, 1536).astype(BF16)
    mq_ref[0] = mm(wb_ref, 1536, 2048).astype(BF16)
    ik2_ref[0] = mm(wb_ref, 2048, 2176).astype(BF16)
    t = lax.dot_general(wt_ref[...], h, _NT_DIMS, preferred_element_type=F32)
    pad_row = lax.broadcasted_iota(I32, (V_ROWS - HEAD_DIM, TM), 0)
    pad = jnp.where(pad_row == 0, 1.0, 0.0).astype(BF16)
    for hd in range(N_HEADS):
        avt_ref[0, 0, V_ROWS * hd:V_ROWS * hd + HEAD_DIM, :] = (
            t[HEAD_DIM * hd:HEAD_DIM * (hd + 1)].astype(BF16))
        avt_ref[0, 0, V_ROWS * hd + HEAD_DIM:V_ROWS * (hd + 1), :] = pad
    iwt_ref[0] = t[ATTN_WIDTH:ATTN_WIDTH + IDX_HEADS] * IDX_W_SCALE


def _project(x, g, wf, wb, wt):
    b, s, d = x.shape
    nt = s // TM
    row = lambda w: pl.BlockSpec((1, TM, w), lambda bi, i: (bi, i, 0))
    full = lambda a: pl.BlockSpec(a.shape, lambda bi, i: (0,) * a.ndim)
    f32o = lambda w: jax.ShapeDtypeStruct((b, s, w), F32)
    b16o = lambda w: jax.ShapeDtypeStruct((b, s, w), BF16)
    out_shape = (f32o(512), f32o(512), f32o(512), f32o(512), f32o(N_BRANCH * D_MODEL),
                 b16o(512), b16o(512), b16o(512), b16o(512), b16o(128),
                 jax.ShapeDtypeStruct((b, nt, N_HEADS * V_ROWS, TM), BF16),
                 jax.ShapeDtypeStruct((b, IDX_HEADS, s), F32))
    out_specs = (row(512), row(512), row(512), row(512), row(N_BRANCH * D_MODEL),
                 row(512), row(512), row(512), row(512), row(128),
                 pl.BlockSpec((1, 1, N_HEADS * V_ROWS, TM), lambda bi, i: (bi, i, 0, 0)),
                 pl.BlockSpec((1, IDX_HEADS, TM), lambda bi, i: (bi, 0, i)))
    return pl.pallas_call(
        _proj_kernel,
        grid=(b, nt),
        in_specs=[row(d), full(g), full(wf), full(wb), full(wt)],
        out_specs=out_specs,
        out_shape=out_shape,
        compiler_params=pltpu.CompilerParams(
            dimension_semantics=("arbitrary", "arbitrary"), vmem_limit_bytes=VMEM_LIMIT),
        name="proj",
    )(x, g, wf, wb, wt)


def _memkv_kernel(mem_ref, g_ref, w_ref, mk_ref, mv_ref):
    h = _rms_norm_f32(mem_ref[0], g_ref[...]).astype(BF16)
    mk_ref[0] = jnp.dot(h, w_ref[:, 0:MEM_WIDTH], preferred_element_type=F32).astype(BF16)
    mv_ref[0] = jnp.dot(h, w_ref[:, MEM_WIDTH:2 * MEM_WIDTH], preferred_element_type=F32).astype(BF16)


def _mem_kv(mem, g, w):
    b, m, d = mem.shape
    o = jax.ShapeDtypeStruct((b, m, MEM_WIDTH), BF16)
    return pl.pallas_call(
        _memkv_kernel,
        grid=(b,),
        in_specs=[pl.BlockSpec((1, m, d), lambda bi: (bi, 0, 0)),
                  pl.BlockSpec(g.shape, lambda bi: (0, 0)),
                  pl.BlockSpec(w.shape, lambda bi: (0, 0))],
        out_specs=(pl.BlockSpec((1, m, MEM_WIDTH), lambda bi: (bi, 0, 0)),) * 2,
        out_shape=(o, o),
        compiler_params=pltpu.CompilerParams(
            dimension_semantics=("arbitrary",), vmem_limit_bytes=VMEM_LIMIT),
        name="memkv",
    )(mem, g, w)


def _bucket_table():
    half = REL_BUCKETS // 2
    max_exact = half // 2
    delta = np.arange(N_NEAR + 1, dtype=np.int64)[:, None, None]
    key = np.arange(KT, dtype=np.int64)[None, :, None]
    qry = np.arange(TQ, dtype=np.int64)[None, None, :]
    rel = key - qry - delta * KT
    ret = np.where(rel > 0, half, 0)
    n = np.abs(rel)
    nf = np.maximum(n, 1).astype(np.float32)
    large = max_exact + (np.log(nf / np.float32(max_exact))
                         / np.float32(math.log(REL_MAX_DIST / max_exact))
                         * np.float32(half - max_exact)).astype(np.int32)
    large = np.minimum(large, half - 1)
    return (ret + np.where(n < max_exact, n, large)).astype(np.int32)


FAR_BUCKET = REL_BUCKETS // 2 - 1


def _bias_kernel(rb_ref, bk_ref, o_ref):
    h = pl.program_id(1)
    bk = bk_ref[0]
    far = rb_ref[FAR_BUCKET, h]
    acc = jnp.zeros((KT, TQ), F32)
    for bkt in range(REL_BUCKETS):
        acc = jnp.where(bk == bkt, (rb_ref[bkt, h] - far) * LOG2E, acc)
    o_ref[0, 0] = acc


def _bias_tiles(rel_bias):
    table = jnp.asarray(_bucket_table())
    return pl.pallas_call(
        _bias_kernel,
        grid=(N_NEAR + 1, N_HEADS),
        in_specs=[pl.BlockSpec(memory_space=pltpu.SMEM),
                  pl.BlockSpec((1, KT, TQ), lambda dl, h: (dl, 0, 0))],
        out_specs=pl.BlockSpec((1, 1, KT, TQ), lambda dl, h: (dl, h, 0, 0)),
        out_shape=jax.ShapeDtypeStruct((N_NEAR + 1, N_HEADS, KT, TQ), F32),
        compiler_params=pltpu.CompilerParams(dimension_semantics=("arbitrary", "arbitrary")),
        name="bias_tiles",
    )(rel_bias, table)


def _dsa_kernel(q_ref, iq_ref, iwt_ref, az_ref, ik2_ref, k_ref, vt_ref, bias_ref, o_ref,
                sc_ref, hi_ref, c1_ref, c2_ref, cge_ref, qw_ref, iqw_ref, acc_ref, m_ref, thr_ref,
                stage0_ref, stage1_ref, tmax0_ref, tmax1_ref):
    i = pl.program_id(1)
    n_tiles = i + 1

    lane = lax.broadcasted_iota(I32, (TQ, LANES), 1)
    for h in range(N_HEADS):
        j = h // 2
        sel = (lane >= HEAD_DIM) if (h % 2) else (lane < HEAD_DIM)
        qw_ref[h] = jnp.where(sel, q_ref[0, :, LANES * j:LANES * (j + 1)].astype(F32), 0.0).astype(BF16)
        iqw_ref[h] = jnp.where(sel, iq_ref[0, :, LANES * j:LANES * (j + 1)].astype(F32), 0.0).astype(BF16)

    key_l = lax.broadcasted_iota(I32, (KT, TQ), 0)
    qry_l = lax.broadcasted_iota(I32, (KT, TQ), 1)
    adm_diag = key_l < ((qry_l // CHUNK) + 1) * CHUNK

    def score_tile(t):
        r0 = pl.multiple_of(t * KT, KT)
        ikt = ik2_ref[0, pl.ds(r0, KT), :]
        acc = jnp.zeros((KT, TQ), F32)
        for h in range(IDX_HEADS):
            y = lax.dot_general(ikt, iqw_ref[h], _NT_DIMS, preferred_element_type=F32)
            acc = acc + jnp.maximum(y, 0.0) * iwt_ref[0, h:h + 1, :]
        adm = jnp.logical_or(jnp.logical_and(adm_diag, t == i), t < i)
        val = jnp.where(adm, acc, NEG)
        sc_ref[pl.ds(r0, KT), :] = val
        top = lax.bitcast_convert_type(val, I32) & I32(-(1 << 16))
        hi_ref[pl.ds(r0, KT), :] = lax.bitcast_convert_type(top, F32).astype(BF16)

    n_steps = (n_tiles + 1) // 2

    def score_step(c, carry):
        score_tile(2 * c)
        score_tile(2 * c + 1)
        return carry

    lax.fori_loop(0, n_steps, score_step, 0)

    def count(pred):
        def body(c, acc):
            for k in range(COUNT_ROWS // ACC_ROWS):
                r0 = pl.multiple_of(c * COUNT_ROWS + k * ACC_ROWS, ACC_ROWS)
                x = sc_ref[pl.ds(r0, ACC_ROWS), :]
                acc = jnp.where(pred(x, r0), acc + 1.0, acc)
            return acc
        acc = lax.fori_loop(0, n_steps, body, jnp.zeros((ACC_ROWS, TQ), F32))
        return acc.sum(axis=0, keepdims=True)

    def to_float(u):
        bits = jnp.where(u < 0, u ^ I32(-2 ** 31), ~u)
        return lax.bitcast_convert_type(bits, F32)

    @pl.when(i == 0)
    def _():
        thr_ref[...] = jnp.full((1, TQ), HALF_NEG, F32)

    def count_top(cand_bf16):
        cb = jnp.broadcast_to(cand_bf16, (TOP_ROWS, TQ))

        def body(c, acc):
            for k in range(COUNT_ROWS // TOP_ROWS):
                r0 = pl.multiple_of(c * COUNT_ROWS + k * TOP_ROWS, TOP_ROWS)
                acc = jnp.where(hi_ref[pl.ds(r0, TOP_ROWS), :] >= cb, acc + 1.0, acc)
            return acc
        acc = lax.fori_loop(0, n_steps, body, jnp.zeros((TOP_ROWS, TQ), BF16))
        return acc.astype(F32).sum(axis=0, keepdims=True)

    @pl.when(i > 0)
    def _():
        def top_body(bi, carry):
            u, cnt_u = carry
            cand = u | lax.shift_left(I32(1 << 16), I32(15) - bi)
            top = lax.bitcast_convert_type(to_float(cand), I32) & I32(-(1 << 16))
            cnt = count_top(lax.bitcast_convert_type(top, F32).astype(BF16))
            keep = cnt >= float(TOPK)
            return jnp.where(keep, cand, u), jnp.where(keep, cnt, cnt_u)

        def bit_body(bi, carry):
            u, cnt_u = carry
            cand = u | lax.shift_left(I32(1), I32(15) - bi)
            cf = to_float(cand)
            cnt = count(lambda x, r0: x >= cf)
            keep = cnt >= float(TOPK)
            return jnp.where(keep, cand, u), jnp.where(keep, cnt, cnt_u)

        carry = (jnp.zeros((1, TQ), I32), jnp.zeros((1, TQ), F32))
        carry = lax.fori_loop(0, 16, top_body, carry)

        u_top, cnt_top = carry
        f_lo = to_float(u_top)
        f_hi = to_float(u_top + I32(1 << 16))
        top_hi = lax.bitcast_convert_type(f_hi, I32) & I32(-(1 << 16))
        above = count_top(lax.bitcast_convert_type(top_hi, F32).astype(BF16))
        f_hi_b = jnp.broadcast_to(f_hi, (SUBLANES, TQ))

        def compact_step(c, carry):
            for g in range(COUNT_ROWS // GROUP_ROWS):
                m1 = jnp.full((SUBLANES, TQ), NEG, F32)
                m2 = m1
                for k in range(GROUP_ROWS // SUBLANES):
                    r0 = pl.multiple_of(c * COUNT_ROWS + g * GROUP_ROWS + k * SUBLANES, SUBLANES)
                    v = sc_ref[pl.ds(r0, SUBLANES), :]
                    v = jnp.where(v >= f_hi_b, NEG, v)
                    m2 = jnp.maximum(m2, jnp.minimum(m1, v))
                    m1 = jnp.maximum(m1, v)
                o0 = pl.multiple_of((c * (COUNT_ROWS // GROUP_ROWS) + g) * SUBLANES, SUBLANES)
                c1_ref[pl.ds(o0, SUBLANES), :] = m1
                c2_ref[pl.ds(o0, SUBLANES), :] = m2
            return carry

        lax.fori_loop(0, n_steps, compact_step, 0)

        def count_compact(cf):
            def body(c, acc):
                r0 = pl.multiple_of(c * CROWS, CROWS)
                acc = jnp.where(c1_ref[pl.ds(r0, CROWS), :] >= cf, acc + 1.0, acc)
                return jnp.where(c2_ref[pl.ds(r0, CROWS), :] >= cf, acc + 1.0, acc)
            acc = lax.fori_loop(0, n_steps, body, jnp.zeros((CROWS, TQ), F32))
            return acc.sum(axis=0, keepdims=True)

        captured = count_compact(f_lo)
        missed = jnp.max(jnp.abs(captured - (cnt_top - above)))

        @pl.when(missed == 0.0)
        def _():
            def cbit_body(bi, carry):
                u, cnt_u = carry
                cand = u | lax.shift_left(I32(1), I32(15) - bi)
                cnt = above + count_compact(to_float(cand))
                keep = cnt >= float(TOPK)
                return jnp.where(keep, cand, u), jnp.where(keep, cnt, cnt_u)

            u, cnt = lax.fori_loop(0, 16, cbit_body, carry)
            thr_ref[...] = to_float(u)
            cge_ref[...] = cnt

        @pl.when(missed != 0.0)
        def _():
            u, cnt = lax.fori_loop(0, 16, bit_body, carry)
            thr_ref[...] = to_float(u)
            cge_ref[...] = cnt

        thr = thr_ref[...]
        cnt_ge = cge_ref[...]

        @pl.when(jnp.max(cnt_ge) > float(TOPK))
        def _():
            need = float(TOPK) - count(lambda x, r0: x > thr)
            row = lax.broadcasted_iota(I32, (ACC_ROWS, TQ), 0)
            row_step = lax.broadcasted_iota(I32, (COUNT_ROWS, TQ), 0)
            n_bits = max(1, (sc_ref.shape[0] - 1).bit_length())

            def idx_body(bi, p):
                cand = p | lax.shift_left(I32(1), I32(n_bits - 1) - bi)
                cnt = count(lambda x, r0: jnp.logical_and(x == thr, row + r0 < cand))
                return jnp.where(cnt < need, cand, p)

            last = lax.fori_loop(0, n_bits, idx_body, jnp.zeros((1, TQ), I32))

            def drop(c, carry):
                r0 = pl.multiple_of(c * COUNT_ROWS, COUNT_ROWS)
                x = sc_ref[pl.ds(r0, COUNT_ROWS), :]
                gone = jnp.logical_and(x == thr, row_step + r0 > last)
                sc_ref[pl.ds(r0, COUNT_ROWS), :] = jnp.where(gone, NEG, x)
                return carry

            lax.fori_loop(0, n_steps, drop, 0)

    m_ref[...] = jnp.full(m_ref.shape, NEG, F32)
    acc_ref[...] = jnp.zeros(acc_ref.shape, F32)
    thr = thr_ref[...]

    def to_mask(c, carry):
        r0 = pl.multiple_of(c * KT, KT)
        sc_ref[pl.ds(r0, KT), :] = jnp.where(sc_ref[pl.ds(r0, KT), :] >= thr, F32(0.0), F32(NEG))
        return carry

    lax.fori_loop(0, n_tiles, to_mask, 0)

    stages = ((stage0_ref, tmax0_ref), (stage1_ref, tmax1_ref))

    def pass_a(t, slot):
        stage_ref, tmax_ref = stages[slot]
        r0 = pl.multiple_of(t * KT, KT)
        dl = jnp.minimum(i - t, N_NEAR)
        for h in range(N_HEADS):
            j = h // 2
            kk = k_ref[0, pl.ds(r0, KT), LANES * j:LANES * (j + 1)]
            s = lax.dot_general(kk, qw_ref[h], _NT_DIMS, preferred_element_type=F32)
            s = s + sc_ref[pl.ds(r0, KT), :] + bias_ref[dl, h]
            stage_ref[h] = s
            tmax_ref[h:h + 1, :] = s.reshape(KT // ACC_ROWS, ACC_ROWS, TQ).max(axis=0).max(
                axis=0, keepdims=True)

    def pass_b(t, slot):
        stage_ref, tmax_ref = stages[slot]
        m_old = m_ref[...]
        m_new = jnp.maximum(m_old, tmax_ref[...])
        alpha = jnp.exp2(m_old - m_new)
        m_ref[...] = m_new
        for h in range(N_HEADS):
            p = jnp.exp2(stage_ref[h] - m_new[h:h + 1, :])
            pv = jnp.dot(vt_ref[0, t, V_ROWS * h:V_ROWS * (h + 1), :], p.astype(BF16),
                         preferred_element_type=F32)
            acc_ref[V_ROWS * h:V_ROWS * (h + 1), :] = (
                alpha[h:h + 1, :] * acc_ref[V_ROWS * h:V_ROWS * (h + 1), :] + pv)

    pass_a(0, 0)
    n_pairs = (n_tiles - 1) // 2

    def attn_pair(pr, carry):
        t = 2 * pr
        pass_a(t + 1, 1)
        pass_b(t, 0)
        pass_a(t + 2, 0)
        pass_b(t + 1, 1)
        return carry

    lax.fori_loop(0, n_pairs, attn_pair, 0)
    t_last = 2 * n_pairs

    @pl.when(n_tiles - 1 > t_last)
    def _():
        pass_a(t_last + 1, 1)
        pass_b(t_last, 0)
        pass_b(t_last + 1, 1)

    @pl.when(n_tiles - 1 == t_last)
    def _():
        pass_b(t_last, 0)

    heads = []
    for h in range(N_HEADS):
        denom = acc_ref[V_ROWS * h + HEAD_DIM:V_ROWS * h + HEAD_DIM + 1, :]
        heads.append(acc_ref[V_ROWS * h:V_ROWS * h + HEAD_DIM, :] * (1.0 / denom))
    y = jnp.concatenate(heads, axis=0).T
    o_ref[0] = (y * jax.nn.silu(az_ref[0])).astype(BF16)


def _dsa(aq, iq, iwt, az, ik2, ak, avt, bias):
    b, s, _ = aq.shape
    nq = s // TQ
    qrow = lambda w: pl.BlockSpec((1, TQ, w), lambda bi, i: (bi, i, 0))
    once = pl.Buffered(1)
    return pl.pallas_call(
        _dsa_kernel,
        grid=(b, nq),
        in_specs=[qrow(ATTN_WIDTH), qrow(IDX_HEADS * IDX_DIM),
                  pl.BlockSpec((1, IDX_HEADS, TQ), lambda bi, i: (bi, 0, i)),
                  qrow(ATTN_WIDTH),
                  pl.BlockSpec((1, s, LANES), lambda bi, i: (bi, 0, 0), pipeline_mode=once),
                  pl.BlockSpec((1, s, ATTN_WIDTH), lambda bi, i: (bi, 0, 0), pipeline_mode=once),
                  pl.BlockSpec((1, s // KT, N_HEADS * V_ROWS, KT), lambda bi, i: (bi, 0, 0, 0),
                               pipeline_mode=once),
                  pl.BlockSpec(bias.shape, lambda bi, i: (0, 0, 0, 0), pipeline_mode=once)],
        out_specs=qrow(ATTN_WIDTH),
        out_shape=jax.ShapeDtypeStruct((b, s, ATTN_WIDTH), BF16),
        scratch_shapes=[pltpu.VMEM((s, TQ), F32),
                        pltpu.VMEM((s, TQ), BF16),
                        pltpu.VMEM((s // GROUP_ROWS * SUBLANES, TQ), F32),
                        pltpu.VMEM((s // GROUP_ROWS * SUBLANES, TQ), F32),
                        pltpu.VMEM((1, TQ), F32),
                        pltpu.VMEM((N_HEADS, TQ, LANES), BF16),
                        pltpu.VMEM((IDX_HEADS, TQ, LANES), BF16),
                        pltpu.VMEM((N_HEADS * V_ROWS, TQ), F32),
                        pltpu.VMEM((N_HEADS, TQ), F32),
                        pltpu.VMEM((1, TQ), F32),
                        pltpu.VMEM((N_HEADS, KT, TQ), F32),
                        pltpu.VMEM((N_HEADS, KT, TQ), F32),
                        pltpu.VMEM((N_HEADS, TQ), F32),
                        pltpu.VMEM((N_HEADS, TQ), F32)],
        compiler_params=pltpu.CompilerParams(
            dimension_semantics=("arbitrary", "arbitrary"), vmem_limit_bytes=VMEM_LIMIT),
        name="dsa",
    )(aq, iq, iwt, az, ik2, ak, avt, bias)


def _merge_kernel(final, x_ref, pu_ref, halo_ref, pz_ref, mq_ref, mz_ref, ya_ref, gates_ref,
                  mk_ref, mv_ref, pw_ref, ps_ref, wb_ref, wo_ref, fg_ref, o_ref):
    i = pl.program_id(1)

    u = pu_ref[0]
    halo = jnp.where(i == 0, 0.0, halo_ref[0])
    ext = jnp.concatenate([halo, u], axis=0)
    t_glob = i * TM + lax.broadcasted_iota(I32, (TM, POOL_GC), 0)
    mixed = []
    for g in range(POOL_GROUPS):
        sl = slice(POOL_GC * g, POOL_GC * (g + 1))
        wsum = ext[:, sl]
        for step in range(g + 1):
            wsum = wsum + pltpu.roll(wsum, 2 ** step, axis=0)
        cnt = jnp.minimum(t_glob + 1, POOL_WINDOWS[g]).astype(F32)
        pooled = wsum[HALO:, :] / cnt - u[:, sl]
        mixed.append(jnp.dot(pooled.astype(BF16), pw_ref[g], preferred_element_type=F32))
    y_pool = jnp.concatenate(mixed, axis=1) * ps_ref[...] * jax.nn.silu(pz_ref[0])

    mem_out = []
    for h in range(MEM_HEADS):
        sl = slice(MEM_HEAD_DIM * h, MEM_HEAD_DIM * (h + 1))
        logits = lax.dot_general(mq_ref[0, :, sl], mk_ref[0, :, sl], _NT_DIMS,
                                 preferred_element_type=F32) * MEM_SCALE
        e = jnp.exp(logits - logits.max(axis=-1, keepdims=True))
        p = e / e.sum(axis=-1, keepdims=True)
        mem_out.append(jnp.dot(p.astype(BF16), mv_ref[0, :, sl], preferred_element_type=F32))
    y_mem = jnp.concatenate(mem_out, axis=1) * jax.nn.silu(mz_ref[0])

    branches = (y_pool.astype(BF16), ya_ref[0], y_mem.astype(BF16))
    merged = jnp.zeros((TM, D_MODEL), F32)
    for br in range(N_BRANCH):
        gate = jax.nn.sigmoid(gates_ref[0, :, D_MODEL * br:D_MODEL * (br + 1)])
        merged = merged + gate * jnp.dot(branches[br], wb_ref[br], preferred_element_type=F32)
    out = x_ref[0] + jnp.dot(merged.astype(BF16), wo_ref[...], preferred_element_type=F32)
    if final:
        out = _rms_norm_f32(out, fg_ref[...])
    o_ref[0] = out


def _merge(final, x, pu, pz, mq, mz, ya, gates, mk, mv, pw, ps, wb, wo, fg):
    b, s, d = x.shape
    row = lambda w: pl.BlockSpec((1, TM, w), lambda bi, i: (bi, i, 0))
    full = lambda a: pl.BlockSpec(a.shape, lambda bi, i: (0,) * a.ndim)
    per_b = lambda a: pl.BlockSpec((1,) + a.shape[1:], lambda bi, i: (bi,) + (0,) * (a.ndim - 1))
    halo = pl.BlockSpec((1, HALO, POOL_WIDTH),
                        lambda bi, i: (bi, jnp.maximum(i * (TM // HALO) - 1, 0), 0))
    return pl.pallas_call(
        functools.partial(_merge_kernel, final),
        grid=(b, s // TM),
        in_specs=[row(d), row(POOL_WIDTH), halo, row(POOL_WIDTH), row(MEM_WIDTH), row(MEM_WIDTH),
                  row(ATTN_WIDTH), row(N_BRANCH * D_MODEL), per_b(mk), per_b(mv),
                  full(pw), full(ps), full(wb), full(wo), full(fg)],
        out_specs=row(d),
        out_shape=jax.ShapeDtypeStruct((b, s, d), F32),
        compiler_params=pltpu.CompilerParams(
            dimension_semantics=("arbitrary", "arbitrary"), vmem_limit_bytes=VMEM_LIMIT),
        name="merge_final" if final else "merge",
    )(x, pu, pu, pz, mq, mz, ya, gates, mk, mv, pw, ps, wb, wo, fg)


def _split_w_in(w):
    col = lambda lo, hi: w[:, lo:hi]
    wf = jnp.concatenate([col(O_PU, O_PZ), col(O_PZ, O_AQ), col(O_AZ, O_IQ), col(O_MZ, O_G),
                          col(O_G, O_END)], axis=1)
    wb = jnp.concatenate([col(O_AQ, O_AK) * (ATTN_SCALE * LOG2E), col(O_AK, O_AV), col(O_IQ, O_IK),
                          col(O_MQ, O_MZ), col(O_IK, O_IW), col(O_IK, O_IW)], axis=1)
    wt = jnp.concatenate([col(O_AV, O_AZ), col(O_IW, O_MQ),
                          jnp.zeros((w.shape[0], WT_ROWS - ATTN_WIDTH - IDX_HEADS), w.dtype)], axis=1).T
    return wf.astype(BF16), wb.astype(BF16), wt.astype(BF16)


def kernel(x, mem, norm_g, w_in, pool_w, pool_scale, mem_norm_g, w_mem_kv, w_branch, w_out,
           rel_bias, final_g):
    depth = w_in.shape[0]
    assert x.shape[1] % COUNT_ROWS == 0 and x.shape[2] == D_MODEL and mem.shape[1] == MEM_LEN
    assert w_in.shape[2] == O_END and TQ == KT == TM
    assert x.shape[1] // TOP_ROWS <= 256
    bias = _bias_tiles(rel_bias)
    fg = final_g.reshape(1, D_MODEL)
    for l in range(depth):
        wf, wb, wt = _split_w_in(w_in[l])
        (pu, pz, az, mz, gates, aq, ak, iq, mq, ik2, avt, iwt) = _project(
            x, norm_g[l].reshape(1, D_MODEL), wf, wb, wt)
        mk, mv = _mem_kv(mem, mem_norm_g[l].reshape(1, D_MODEL), w_mem_kv[l].astype(BF16))
        ya = _dsa(aq, iq, iwt, az, ik2, ak, avt, bias)
        x = _merge(l == depth - 1, x, pu, pz, mq, mz, ya, gates, mk, mv,
                   pool_w[l].astype(BF16), pool_scale[l].reshape(1, POOL_WIDTH),
                   w_branch[l].astype(BF16), w_out[l].astype(BF16), fg)
    return x
```

```python
import functools
import math

import numpy as np
import jax
import jax.numpy as jnp
from jax import lax
from jax.experimental import pallas as pl
from jax.experimental.pallas import tpu as pltpu

F32 = jnp.float32
BF16 = jnp.bfloat16
I32 = jnp.int32

D_MODEL = 1024
CHUNK = 64
MEM_LEN = 256
POOL_GROUPS = 4
POOL_WINDOWS = (2, 4, 8, 16)
POOL_WIDTH = D_MODEL // 2
POOL_GC = POOL_WIDTH // POOL_GROUPS
N_HEADS = 8
HEAD_DIM = 64
ATTN_WIDTH = N_HEADS * HEAD_DIM
IDX_HEADS = 8
IDX_DIM = 64
TOPK = 256
ATTN_SCALE = HEAD_DIM ** -0.5
LOG2E = math.log2(math.e)
IDX_W_SCALE = (IDX_DIM ** -0.5) * (IDX_HEADS ** -0.5)
MEM_HEADS = 4
MEM_HEAD_DIM = 128
MEM_WIDTH = MEM_HEADS * MEM_HEAD_DIM
MEM_SCALE = MEM_HEAD_DIM ** -0.5
N_BRANCH = 3
REL_BUCKETS = 32
REL_MAX_DIST = 1024
EPS = 1e-6
NEG = -1e30
HALF_NEG = -5e29

IN_SPLITS = (POOL_WIDTH, POOL_WIDTH, ATTN_WIDTH, ATTN_WIDTH, ATTN_WIDTH, ATTN_WIDTH,
             IDX_HEADS * IDX_DIM, IDX_DIM, IDX_HEADS, MEM_WIDTH, MEM_WIDTH, N_BRANCH * D_MODEL)
_OFF = np.concatenate([[0], np.cumsum(IN_SPLITS)]).tolist()
(O_PU, O_PZ, O_AQ, O_AK, O_AV, O_AZ, O_IQ, O_IK, O_IW, O_MQ, O_MZ, O_G, O_END) = _OFF

LANES = 128
SUBLANES = 8
TQ = 256
KT = 256
TM = 256
HALO = 16
N_NEAR = 4
COUNT_ROWS = 2 * KT
ACC_ROWS = 4 * SUBLANES
TOP_ROWS = 8 * SUBLANES
BANK_ROWS = 16
GROUP_ROWS = 128
CROWS = COUNT_ROWS // GROUP_ROWS * SUBLANES
WT_ROWS = 528
V_ROWS = HEAD_DIM + 16
VMEM_LIMIT = 56 * 1024 * 1024

_NT_DIMS = (((1,), (1,)), ((), ()))


def _rms_norm_f32(x, g):
    ms = jnp.mean(x * x, axis=-1, keepdims=True)
    return x * lax.rsqrt(ms + EPS) * g


def _proj_kernel(x_ref, g_ref, wf_ref, wb_ref, wt_ref,
                 pu_ref, pz_ref, az_ref, mz_ref, gates_ref,
                 aq_ref, ak_ref, iq_ref, mq_ref, ik2_ref, avt_ref, iwt_ref):
    h = _rms_norm_f32(x_ref[0], g_ref[...]).astype(BF16)

    def mm(w_ref, lo, hi):
        return jnp.dot(h, w_ref[:, lo:hi], preferred_element_type=F32)

    pu_ref[0] = mm(wf_ref, 0, 512)
    pz_ref[0] = mm(wf_ref, 512, 1024)
    az_ref[0] = mm(wf_ref, 1024, 1536)
    mz_ref[0] = mm(wf_ref, 1536, 2048)
    for c in range(N_BRANCH * D_MODEL // 512):
        gates_ref[0, :, 512 * c:512 * (c + 1)] = mm(wf_ref, 2048 + 512 * c, 2048 + 512 * (c + 1))
    aq_ref[0] = mm(wb_ref, 0, 512).astype(BF16)
    ak_ref[0] = mm(wb_ref, 512, 1024).astype(BF16)
    iq_ref[0] = mm(wb_ref, 1024, 1536).astype(BF16)
    mq_ref[0] = mm(wb_ref, 1536, 2048).astype(BF16)
    ik2_ref[0] = mm(wb_ref, 2048, 2176).astype(BF16)
    t = lax.dot_general(wt_ref[...], h, _NT_DIMS, preferred_element_type=F32)
    pad_row = lax.broadcasted_iota(I32, (V_ROWS - HEAD_DIM, TM), 0)
    pad = jnp.where(pad_row == 0, 1.0, 0.0).astype(BF16)
    for hd in range(N_HEADS):
        avt_ref[0, 0, V_ROWS * hd:V_ROWS * hd + HEAD_DIM, :] = (
            t[HEAD_DIM * hd:HEAD_DIM * (hd + 1)].astype(BF16))
        avt_ref[0, 0, V_ROWS * hd + HEAD_DIM:V_ROWS * (hd + 1), :] = pad
    iwt_ref[0] = t[ATTN_WIDTH:ATTN_WIDTH + IDX_HEADS] * IDX_W_SCALE


def _project(x, g, wf, wb, wt):
    b, s, d = x.shape
    nt = s // TM
    row = lambda w: pl.BlockSpec((1, TM, w), lambda bi, i: (bi, i, 0))
    full = lambda a: pl.BlockSpec(a.shape, lambda bi, i: (0,) * a.ndim)
    f32o = lambda w: jax.ShapeDtypeStruct((b, s, w), F32)
    b16o = lambda w: jax.ShapeDtypeStruct((b, s, w), BF16)
    out_shape = (f32o(512), f32o(512), f32o(512), f32o(512), f32o(N_BRANCH * D_MODEL),
                 b16o(512), b16o(512), b16o(512), b16o(512), b16o(128),
                 jax.ShapeDtypeStruct((b, nt, N_HEADS * V_ROWS, TM), BF16),
                 jax.ShapeDtypeStruct((b, IDX_HEADS, s), F32))
    out_specs = (row(512), row(512), row(512), row(512), row(N_BRANCH * D_MODEL),
                 row(512), row(512), row(512), row(512), row(128),
                 pl.BlockSpec((1, 1, N_HEADS * V_ROWS, TM), lambda bi, i: (bi, i, 0, 0)),
                 pl.BlockSpec((1, IDX_HEADS, TM), lambda bi, i: (bi, 0, i)))
    return pl.pallas_call(
        _proj_kernel,
        grid=(b, nt),
        in_specs=[row(d), full(g), full(wf), full(wb), full(wt)],
        out_specs=out_specs,
        out_shape=out_shape,
        compiler_params=pltpu.CompilerParams(
            dimension_semantics=("arbitrary", "arbitrary"), vmem_limit_bytes=VMEM_LIMIT),
        name="proj",
    )(x, g, wf, wb, wt)


def _memkv_kernel(mem_ref, g_ref, w_ref, mk_ref, mv_ref):
    h = _rms_norm_f32(mem_ref[0], g_ref[...]).astype(BF16)
    mk_ref[0] = jnp.dot(h, w_ref[:, 0:MEM_WIDTH], preferred_element_type=F32).astype(BF16)
    mv_ref[0] = jnp.dot(h, w_ref[:, MEM_WIDTH:2 * MEM_WIDTH], preferred_element_type=F32).astype(BF16)


def _mem_kv(mem, g, w):
    b, m, d = mem.shape
    o = jax.ShapeDtypeStruct((b, m, MEM_WIDTH), BF16)
    return pl.pallas_call(
        _memkv_kernel,
        grid=(b,),
        in_specs=[pl.BlockSpec((1, m, d), lambda bi: (bi, 0, 0)),
                  pl.BlockSpec(g.shape, lambda bi: (0, 0)),
                  pl.BlockSpec(w.shape, lambda bi: (0, 0))],
        out_specs=(pl.BlockSpec((1, m, MEM_WIDTH), lambda bi: (bi, 0, 0)),) * 2,
        out_shape=(o, o),
        compiler_params=pltpu.CompilerParams(
            dimension_semantics=("arbitrary",), vmem_limit_bytes=VMEM_LIMIT),
        name="memkv",
    )(mem, g, w)


def _bucket_table():
    half = REL_BUCKETS // 2
    max_exact = half // 2
    delta = np.arange(N_NEAR + 1, dtype=np.int64)[:, None, None]
    key = np.arange(KT, dtype=np.int64)[None, :, None]
    qry = np.arange(TQ, dtype=np.int64)[None, None, :]
    rel = key - qry - delta * KT
    ret = np.where(rel > 0, half, 0)
    n = np.abs(rel)
    nf = np.maximum(n, 1).astype(np.float32)
    large = max_exact + (np.log(nf / np.float32(max_exact))
                         / np.float32(math.log(REL_MAX_DIST / max_exact))
                         * np.float32(half - max_exact)).astype(np.int32)
    large = np.minimum(large, half - 1)
    return (ret + np.where(n < max_exact, n, large)).astype(np.int32)


FAR_BUCKET = REL_BUCKETS // 2 - 1


def _bias_kernel(rb_ref, bk_ref, o_ref):
    h = pl.program_id(1)
    bk = bk_ref[0]
    far = rb_ref[FAR_BUCKET, h]
    acc = jnp.zeros((KT, TQ), F32)
    for bkt in range(REL_BUCKETS):
        acc = jnp.where(bk == bkt, (rb_ref[bkt, h] - far) * LOG2E, acc)
    o_ref[0, 0] = acc


def _bias_tiles(rel_bias):
    table = jnp.asarray(_bucket_table())
    return pl.pallas_call(
        _bias_kernel,
        grid=(N_NEAR + 1, N_HEADS),
        in_specs=[pl.BlockSpec(memory_space=pltpu.SMEM),
                  pl.BlockSpec((1, KT, TQ), lambda dl, h: (dl, 0, 0))],
        out_specs=pl.BlockSpec((1, 1, KT, TQ), lambda dl, h: (dl, h, 0, 0)),
        out_shape=jax.ShapeDtypeStruct((N_NEAR + 1, N_HEADS, KT, TQ), F32),
        compiler_params=pltpu.CompilerParams(dimension_semantics=("arbitrary", "arbitrary")),
        name="bias_tiles",
    )(rel_bias, table)


def _dsa_kernel(q_ref, iq_ref, iwt_ref, az_ref, ik2_ref, k_ref, vt_ref, bias_ref, o_ref,
                sc_ref, hi_ref, c1_ref, c2_ref, cge_buf, qw_ref, iqw_ref, acc_ref, m_buf, thr_buf,
                stage0_ref, stage1_ref, tmax0_buf, tmax1_buf):
    i = pl.program_id(1)
    n_tiles = i + 1
    cge_ref, thr_ref = cge_buf.at[0:1], thr_buf.at[0:1]
    m_ref, tmax0_ref, tmax1_ref = (r.at[0:N_HEADS] for r in (m_buf, tmax0_buf, tmax1_buf))

    lane = lax.broadcasted_iota(I32, (TQ, LANES), 1)
    for h in range(N_HEADS):
        j = h // 2
        sel = (lane >= HEAD_DIM) if (h % 2) else (lane < HEAD_DIM)
        qw_ref[h] = jnp.where(sel, q_ref[0, :, LANES * j:LANES * (j + 1)].astype(F32), 0.0).astype(BF16)
        iqw_ref[h] = jnp.where(sel, iq_ref[0, :, LANES * j:LANES * (j + 1)].astype(F32), 0.0).astype(BF16)

    key_l = lax.broadcasted_iota(I32, (KT, TQ), 0)
    qry_l = lax.broadcasted_iota(I32, (KT, TQ), 1)
    adm_diag = key_l < ((qry_l // CHUNK) + 1) * CHUNK

    def score_tile(t):
        r0 = pl.multiple_of(t * KT, KT)
        ikt = ik2_ref[0, pl.ds(r0, KT), :]
        acc = jnp.zeros((KT, TQ), F32)
        for h in range(IDX_HEADS):
            y = lax.dot_general(ikt, iqw_ref[h], _NT_DIMS, preferred_element_type=F32)
            acc = acc + jnp.maximum(y, 0.0) * iwt_ref[0, h:h + 1, :]
        adm = jnp.logical_or(jnp.logical_and(adm_diag, t == i), t < i)
        val = jnp.where(adm, acc, NEG)
        sc_ref[pl.ds(r0, KT), :] = val
        top = lax.bitcast_convert_type(val, I32) & I32(-(1 << 16))
        hi_ref[pl.ds(r0, KT), :] = lax.bitcast_convert_type(top, F32).astype(BF16)

    n_steps = (n_tiles + 1) // 2

    def score_step(c, carry):
        score_tile(2 * c)
        score_tile(2 * c + 1)
        return carry

    lax.fori_loop(0, n_steps, score_step, 0)

    def count(pred):
        def body(c, acc):
            for k in range(COUNT_ROWS // ACC_ROWS):
                r0 = pl.multiple_of(c * COUNT_ROWS + k * ACC_ROWS, ACC_ROWS)
                x = sc_ref[pl.ds(r0, ACC_ROWS), :]
                acc = jnp.where(pred(x, r0), acc + 1.0, acc)
            return acc
        acc = lax.fori_loop(0, n_steps, body, jnp.zeros((ACC_ROWS, TQ), F32))
        return acc.sum(axis=0, keepdims=True)

    def to_float(u):
        bits = jnp.where(u < 0, u ^ I32(-2 ** 31), ~u)
        return lax.bitcast_convert_type(bits, F32)

    @pl.when(i == 0)
    def _():
        thr_ref[...] = jnp.full((1, TQ), HALF_NEG, F32)

    def count_top(cand_bf16):
        cb = jnp.broadcast_to(cand_bf16, (TOP_ROWS, TQ))

        def body(c, acc):
            for k in range(COUNT_ROWS // TOP_ROWS):
                r0 = pl.multiple_of(c * COUNT_ROWS + k * TOP_ROWS, TOP_ROWS)
                acc = jnp.where(hi_ref[pl.ds(r0, TOP_ROWS), :] >= cb, acc + 1.0, acc)
            return acc
        acc = lax.fori_loop(0, n_steps, body, jnp.zeros((TOP_ROWS, TQ), BF16))
        return acc.astype(F32).sum(axis=0, keepdims=True)

    @pl.when(i > 0)
    def _():
        def top_body(bi, carry):
            u, cnt_u = carry
            cand = u | lax.shift_left(I32(1 << 16), I32(15) - bi)
            top = lax.bitcast_convert_type(to_float(cand), I32) & I32(-(1 << 16))
            cnt = count_top(lax.bitcast_convert_type(top, F32).astype(BF16))
            keep = cnt >= float(TOPK)
            return jnp.where(keep, cand, u), jnp.where(keep, cnt, cnt_u)

        def bit_body(bi, carry):
            u, cnt_u = carry
            cand = u | lax.shift_left(I32(1), I32(15) - bi)
            cf = to_float(cand)
            cnt = count(lambda x, r0: x >= cf)
            keep = cnt >= float(TOPK)
            return jnp.where(keep, cand, u), jnp.where(keep, cnt, cnt_u)

        carry = (jnp.zeros((1, TQ), I32), jnp.zeros((1, TQ), F32))
        carry = lax.fori_loop(0, 16, top_body, carry)

        u_top, cnt_top = carry
        f_lo = to_float(u_top)
        f_hi = to_float(u_top + I32(1 << 16))
        top_hi = lax.bitcast_convert_type(f_hi, I32) & I32(-(1 << 16))
        above = count_top(lax.bitcast_convert_type(top_hi, F32).astype(BF16))
        f_hi_b = jnp.broadcast_to(f_hi, (SUBLANES, TQ))

        def compact_step(c, carry):
            for g in range(COUNT_ROWS // GROUP_ROWS):
                m1 = jnp.full((SUBLANES, TQ), NEG, F32)
                m2 = m1
                for k in range(GROUP_ROWS // SUBLANES):
                    r0 = pl.multiple_of(c * COUNT_ROWS + g * GROUP_ROWS + k * SUBLANES, SUBLANES)
                    v = sc_ref[pl.ds(r0, SUBLANES), :]
                    v = jnp.where(v >= f_hi_b, NEG, v)
                    m2 = jnp.maximum(m2, jnp.minimum(m1, v))
                    m1 = jnp.maximum(m1, v)
                o0 = pl.multiple_of((c * (COUNT_ROWS // GROUP_ROWS) + g) * SUBLANES, SUBLANES)
                c1_ref[pl.ds(o0, SUBLANES), :] = m1
                c2_ref[pl.ds(o0, SUBLANES), :] = m2
            return carry

        lax.fori_loop(0, n_steps, compact_step, 0)

        def count_compact(cf):
            def body(c, acc):
                r0 = pl.multiple_of(c * CROWS, CROWS)
                acc = jnp.where(c1_ref[pl.ds(r0, CROWS), :] >= cf, acc + 1.0, acc)
                return jnp.where(c2_ref[pl.ds(r0, CROWS), :] >= cf, acc + 1.0, acc)
            acc = lax.fori_loop(0, n_steps, body, jnp.zeros((CROWS, TQ), F32))
            return acc.sum(axis=0, keepdims=True)

        captured = count_compact(f_lo)
        missed = jnp.max(jnp.abs(captured - (cnt_top - above)))

        @pl.when(missed == 0.0)
        def _():
            def cbit_body(bi, carry):
                u, cnt_u = carry
                cand = u | lax.shift_left(I32(1), I32(15) - bi)
                cnt = above + count_compact(to_float(cand))
                keep = cnt >= float(TOPK)
                return jnp.where(keep, cand, u), jnp.where(keep, cnt, cnt_u)

            u, cnt = lax.fori_loop(0, 16, cbit_body, carry)
            thr_ref[...] = to_float(u)
            cge_ref[...] = cnt

        @pl.when(missed != 0.0)
        def _():
            u, cnt = lax.fori_loop(0, 16, bit_body, carry)
            thr_ref[...] = to_float(u)
            cge_ref[...] = cnt

        thr = thr_ref[...]
        cnt_ge = cge_ref[...]

        @pl.when(jnp.max(cnt_ge) > float(TOPK))
        def _():
            need = float(TOPK) - count(lambda x, r0: x > thr)
            row = lax.broadcasted_iota(I32, (ACC_ROWS, TQ), 0)
            row_step = lax.broadcasted_iota(I32, (COUNT_ROWS, TQ), 0)
            n_bits = max(1, (sc_ref.shape[0] - 1).bit_length())

            def idx_body(bi, p):
                cand = p | lax.shift_left(I32(1), I32(n_bits - 1) - bi)
                cnt = count(lambda x, r0: jnp.logical_and(x == thr, row + r0 < cand))
                return jnp.where(cnt < need, cand, p)

            last = lax.fori_loop(0, n_bits, idx_body, jnp.zeros((1, TQ), I32))

            def drop(c, carry):
                r0 = pl.multiple_of(c * COUNT_ROWS, COUNT_ROWS)
                x = sc_ref[pl.ds(r0, COUNT_ROWS), :]
                gone = jnp.logical_and(x == thr, row_step + r0 > last)
                sc_ref[pl.ds(r0, COUNT_ROWS), :] = jnp.where(gone, NEG, x)
                return carry

            lax.fori_loop(0, n_steps, drop, 0)

    m_ref[...] = jnp.full(m_ref.shape, NEG, F32)
    acc_ref[...] = jnp.zeros(acc_ref.shape, F32)
    thr = thr_ref[...]

    def to_mask(c, carry):
        r0 = pl.multiple_of(c * KT, KT)
        sc_ref[pl.ds(r0, KT), :] = jnp.where(sc_ref[pl.ds(r0, KT), :] >= thr, F32(0.0), F32(NEG))
        return carry

    lax.fori_loop(0, n_tiles, to_mask, 0)

    stages = ((stage0_ref, tmax0_ref), (stage1_ref, tmax1_ref))

    def head_a(t, slot, h):
        stage_ref, tmax_ref = stages[slot]
        r0 = pl.multiple_of(t * KT, KT)
        dl = jnp.minimum(i - t, N_NEAR)
        j = h // 2
        kk = k_ref[0, pl.ds(r0, KT), LANES * j:LANES * (j + 1)]
        s = lax.dot_general(kk, qw_ref[h], _NT_DIMS, preferred_element_type=F32)
        s = s + sc_ref[pl.ds(r0, KT), :] + bias_ref[dl, h]
        stage_ref[h] = s
        tmax_ref[h:h + 1, :] = s.reshape(KT // ACC_ROWS, ACC_ROWS, TQ).max(axis=0).max(
            axis=0, keepdims=True)

    def begin_b(slot):
        m_old = m_ref[...]
        m_new = jnp.maximum(m_old, stages[slot][1][...])
        m_ref[...] = m_new
        return m_new, jnp.exp2(m_old - m_new)

    def head_b(t, slot, h, m_new, alpha):
        p = jnp.exp2(stages[slot][0][h] - m_new[h:h + 1, :])
        pv = jnp.dot(vt_ref[0, t, V_ROWS * h:V_ROWS * (h + 1), :], p.astype(BF16),
                     preferred_element_type=F32)
        acc_ref[V_ROWS * h:V_ROWS * (h + 1), :] = (
            alpha[h:h + 1, :] * acc_ref[V_ROWS * h:V_ROWS * (h + 1), :] + pv)

    def pass_a(t, slot):
        for h in range(N_HEADS):
            head_a(t, slot, h)

    def pass_b(t, slot):
        m_new, alpha = begin_b(slot)
        for h in range(N_HEADS):
            head_b(t, slot, h, m_new, alpha)

    pass_a(0, 0)
    n_pairs = (n_tiles - 1) // 2

    def attn_pair(pr, carry):
        t = 2 * pr
        pass_a(t + 1, 1)
        pass_b(t, 0)
        pass_a(t + 2, 0)
        pass_b(t + 1, 1)
        return carry

    lax.fori_loop(0, n_pairs, attn_pair, 0)
    t_last = 2 * n_pairs

    @pl.when(n_tiles - 1 > t_last)
    def _():
        pass_a(t_last + 1, 1)
        pass_b(t_last, 0)
        pass_b(t_last + 1, 1)

    @pl.when(n_tiles - 1 == t_last)
    def _():
        pass_b(t_last, 0)

    heads = []
    for h in range(N_HEADS):
        denom = acc_ref[V_ROWS * h + HEAD_DIM:V_ROWS * h + HEAD_DIM + 1, :]
        heads.append(acc_ref[V_ROWS * h:V_ROWS * h + HEAD_DIM, :] * (1.0 / denom))
    y = jnp.concatenate(heads, axis=0).T
    o_ref[0] = (y * jax.nn.silu(az_ref[0])).astype(BF16)


def _dsa(aq, iq, iwt, az, ik2, ak, avt, bias):
    b, s, _ = aq.shape
    nq = s // TQ
    qrow = lambda w: pl.BlockSpec((1, TQ, w), lambda bi, i: (bi, i, 0))
    once = pl.Buffered(1)
    small = pltpu.VMEM((BANK_ROWS, TQ), F32)
    return pl.pallas_call(
        _dsa_kernel,
        grid=(b, nq),
        in_specs=[qrow(ATTN_WIDTH), qrow(IDX_HEADS * IDX_DIM),
                  pl.BlockSpec((1, IDX_HEADS, TQ), lambda bi, i: (bi, 0, i)),
                  qrow(ATTN_WIDTH),
                  pl.BlockSpec((1, s, LANES), lambda bi, i: (bi, 0, 0), pipeline_mode=once),
                  pl.BlockSpec((1, s, ATTN_WIDTH), lambda bi, i: (bi, 0, 0), pipeline_mode=once),
                  pl.BlockSpec((1, s // KT, N_HEADS * V_ROWS, KT), lambda bi, i: (bi, 0, 0, 0),
                               pipeline_mode=once),
                  pl.BlockSpec(bias.shape, lambda bi, i: (0, 0, 0, 0), pipeline_mode=once)],
        out_specs=qrow(ATTN_WIDTH),
        out_shape=jax.ShapeDtypeStruct((b, s, ATTN_WIDTH), BF16),
        scratch_shapes=[pltpu.VMEM((s, TQ), F32),
                        pltpu.VMEM((s, TQ), BF16),
                        pltpu.VMEM((s // GROUP_ROWS * SUBLANES, TQ), F32),
                        pltpu.VMEM((s // GROUP_ROWS * SUBLANES, TQ), F32),
                        small,
                        pltpu.VMEM((N_HEADS, TQ, LANES), BF16),
                        pltpu.VMEM((IDX_HEADS, TQ, LANES), BF16),
                        pltpu.VMEM((N_HEADS * V_ROWS, TQ), F32),
                        small,
                        small,
                        pltpu.VMEM((N_HEADS, KT, TQ), F32),
                        pltpu.VMEM((N_HEADS, KT, TQ), F32),
                        small,
                        small],
        compiler_params=pltpu.CompilerParams(
            dimension_semantics=("arbitrary", "arbitrary"), vmem_limit_bytes=VMEM_LIMIT),
        name="dsa",
    )(aq, iq, iwt, az, ik2, ak, avt, bias)


def _merge_kernel(final, x_ref, pu_ref, halo_ref, pz_ref, mq_ref, mz_ref, ya_ref, gates_ref,
                  mk_ref, mv_ref, pw_ref, ps_ref, wb_ref, wo_ref, fg_ref, o_ref):
    i = pl.program_id(1)

    u = pu_ref[0]
    halo = jnp.where(i == 0, 0.0, halo_ref[0])
    ext = jnp.concatenate([halo, u], axis=0)
    t_glob = i * TM + lax.broadcasted_iota(I32, (TM, POOL_GC), 0)
    mixed = []
    for g in range(POOL_GROUPS):
        sl = slice(POOL_GC * g, POOL_GC * (g + 1))
        wsum = ext[:, sl]
        for step in range(g + 1):
            wsum = wsum + pltpu.roll(wsum, 2 ** step, axis=0)
        cnt = jnp.minimum(t_glob + 1, POOL_WINDOWS[g]).astype(F32)
        pooled = wsum[HALO:, :] / cnt - u[:, sl]
        mixed.append(jnp.dot(pooled.astype(BF16), pw_ref[g], preferred_element_type=F32))
    y_pool = jnp.concatenate(mixed, axis=1) * ps_ref[...] * jax.nn.silu(pz_ref[0])

    mem_out = []
    for h in range(MEM_HEADS):
        sl = slice(MEM_HEAD_DIM * h, MEM_HEAD_DIM * (h + 1))
        logits = lax.dot_general(mq_ref[0, :, sl], mk_ref[0, :, sl], _NT_DIMS,
                                 preferred_element_type=F32) * MEM_SCALE
        e = jnp.exp(logits - logits.max(axis=-1, keepdims=True))
        p = e / e.sum(axis=-1, keepdims=True)
        mem_out.append(jnp.dot(p.astype(BF16), mv_ref[0, :, sl], preferred_element_type=F32))
    y_mem = jnp.concatenate(mem_out, axis=1) * jax.nn.silu(mz_ref[0])

    branches = (y_pool.astype(BF16), ya_ref[0], y_mem.astype(BF16))
    merged = jnp.zeros((TM, D_MODEL), F32)
    for br in range(N_BRANCH):
        gate = jax.nn.sigmoid(gates_ref[0, :, D_MODEL * br:D_MODEL * (br + 1)])
        merged = merged + gate * jnp.dot(branches[br], wb_ref[br], preferred_element_type=F32)
    out = x_ref[0] + jnp.dot(merged.astype(BF16), wo_ref[...], preferred_element_type=F32)
    if final:
        out = _rms_norm_f32(out, fg_ref[...])
    o_ref[0] = out


def _merge(final, x, pu, pz, mq, mz, ya, gates, mk, mv, pw, ps, wb, wo, fg):
    b, s, d = x.shape
    row = lambda w: pl.BlockSpec((1, TM, w), lambda bi, i: (bi, i, 0))
    full = lambda a: pl.BlockSpec(a.shape, lambda bi, i: (0,) * a.ndim)
    per_b = lambda a: pl.BlockSpec((1,) + a.shape[1:], lambda bi, i: (bi,) + (0,) * (a.ndim - 1))
    halo = pl.BlockSpec((1, HALO, POOL_WIDTH),
                        lambda bi, i: (bi, jnp.maximum(i * (TM // HALO) - 1, 0), 0))
    return pl.pallas_call(
        functools.partial(_merge_kernel, final),
        grid=(b, s // TM),
        in_specs=[row(d), row(POOL_WIDTH), halo, row(POOL_WIDTH), row(MEM_WIDTH), row(MEM_WIDTH),
                  row(ATTN_WIDTH), row(N_BRANCH * D_MODEL), per_b(mk), per_b(mv),
                  full(pw), full(ps), full(wb), full(wo), full(fg)],
        out_specs=row(d),
        out_shape=jax.ShapeDtypeStruct((b, s, d), F32),
        compiler_params=pltpu.CompilerParams(
            dimension_semantics=("arbitrary", "arbitrary"), vmem_limit_bytes=VMEM_LIMIT),
        name="merge_final" if final else "merge",
    )(x, pu, pu, pz, mq, mz, ya, gates, mk, mv, pw, ps, wb, wo, fg)


def _split_w_in(w):
    col = lambda lo, hi: w[:, lo:hi]
    wf = jnp.concatenate([col(O_PU, O_PZ), col(O_PZ, O_AQ), col(O_AZ, O_IQ), col(O_MZ, O_G),
                          col(O_G, O_END)], axis=1)
    wb = jnp.concatenate([col(O_AQ, O_AK) * (ATTN_SCALE * LOG2E), col(O_AK, O_AV), col(O_IQ, O_IK),
                          col(O_MQ, O_MZ), col(O_IK, O_IW), col(O_IK, O_IW)], axis=1)
    wt = jnp.concatenate([col(O_AV, O_AZ), col(O_IW, O_MQ),
                          jnp.zeros((w.shape[0], WT_ROWS - ATTN_WIDTH - IDX_HEADS), w.dtype)], axis=1).T
    return wf.astype(BF16), wb.astype(BF16), wt.astype(BF16)


def kernel(x, mem, norm_g, w_in, pool_w, pool_scale, mem_norm_g, w_mem_kv, w_branch, w_out,
           rel_bias, final_g):
    depth = w_in.shape[0]
    assert x.shape[1] % COUNT_ROWS == 0 and x.shape[2] == D_MODEL and mem.shape[1] == MEM_LEN
    assert w_in.shape[2] == O_END and TQ == KT == TM
    assert x.shape[1] // TOP_ROWS <= 256
    bias = _bias_tiles(rel_bias)
    fg = final_g.reshape(1, D_MODEL)
    for l in range(depth):
        wf, wb, wt = _split_w_in(w_in[l])
        (pu, pz, az, mz, gates, aq, ak, iq, mq, ik2, avt, iwt) = _project(
            x, norm_g[l].reshape(1, D_MODEL), wf, wb, wt)
        mk, mv = _mem_kv(mem, mem_norm_g[l].reshape(1, D_MODEL), w_mem_kv[l].astype(BF16))
        ya = _dsa(aq, iq, iwt, az, ik2, ak, avt, bias)
        x = _merge(l == depth - 1, x, pu, pz, mq, mz, ya, gates, mk, mv,
                   pool_w[l].astype(BF16), pool_scale[l].reshape(1, POOL_WIDTH),
                   w_branch[l].astype(BF16), w_out[l].astype(BF16), fg)
    return x
```

```python
import functools
import math

import numpy as np
import jax
import jax.numpy as jnp
from jax import lax
from jax.experimental import pallas as pl
from jax.experimental.pallas import tpu as pltpu

F32 = jnp.float32
BF16 = jnp.bfloat16
I32 = jnp.int32

D_MODEL = 1024
CHUNK = 64
MEM_LEN = 256
POOL_GROUPS = 4
POOL_WINDOWS = (2, 4, 8, 16)
POOL_WIDTH = D_MODEL // 2
POOL_GC = POOL_WIDTH // POOL_GROUPS
N_HEADS = 8
HEAD_DIM = 64
ATTN_WIDTH = N_HEADS * HEAD_DIM
IDX_HEADS = 8
IDX_DIM = 64
TOPK = 256
ATTN_SCALE = HEAD_DIM ** -0.5
LOG2E = math.log2(math.e)
IDX_W_SCALE = (IDX_DIM ** -0.5) * (IDX_HEADS ** -0.5)
MEM_HEADS = 4
MEM_HEAD_DIM = 128
MEM_WIDTH = MEM_HEADS * MEM_HEAD_DIM
MEM_SCALE = MEM_HEAD_DIM ** -0.5
N_BRANCH = 3
REL_BUCKETS = 32
REL_MAX_DIST = 1024
EPS = 1e-6
NEG = -1e30
HALF_NEG = -5e29

IN_SPLITS = (POOL_WIDTH, POOL_WIDTH, ATTN_WIDTH, ATTN_WIDTH, ATTN_WIDTH, ATTN_WIDTH,
             IDX_HEADS * IDX_DIM, IDX_DIM, IDX_HEADS, MEM_WIDTH, MEM_WIDTH, N_BRANCH * D_MODEL)
_OFF = np.concatenate([[0], np.cumsum(IN_SPLITS)]).tolist()
(O_PU, O_PZ, O_AQ, O_AK, O_AV, O_AZ, O_IQ, O_IK, O_IW, O_MQ, O_MZ, O_G, O_END) = _OFF

LANES = 128
SUBLANES = 8
TQ = 256
KT = 256
TM = 256
HALO = 16
N_NEAR = 4
COUNT_ROWS = 2 * KT
ACC_ROWS = 4 * SUBLANES
TOP_ROWS = 8 * SUBLANES
BANK_ROWS = 16
GROUP_ROWS = 128
CROWS = COUNT_ROWS // GROUP_ROWS * SUBLANES
WT_ROWS = 528
V_ROWS = HEAD_DIM + 16
VMEM_LIMIT = 56 * 1024 * 1024

_NT_DIMS = (((1,), (1,)), ((), ()))


def _rms_norm_f32(x, g):
    ms = jnp.mean(x * x, axis=-1, keepdims=True)
    return x * lax.rsqrt(ms + EPS) * g


def _proj_kernel(x_ref, g_ref, wf_ref, wb_ref, wt_ref,
                 pu_ref, pz_ref, az_ref, mz_ref, gates_ref,
                 aq_ref, ak_ref, iq_ref, mq_ref, ik2_ref, avt_ref, iwt_ref):
    h = _rms_norm_f32(x_ref[0], g_ref[...]).astype(BF16)

    def mm(w_ref, lo, hi):
        return jnp.dot(h, w_ref[:, lo:hi], preferred_element_type=F32)

    pu_ref[0] = mm(wf_ref, 0, 512)
    pz_ref[0] = mm(wf_ref, 512, 1024)
    az_ref[0] = mm(wf_ref, 1024, 1536)
    mz_ref[0] = mm(wf_ref, 1536, 2048)
    for c in range(N_BRANCH * D_MODEL // 512):
        gates_ref[0, :, 512 * c:512 * (c + 1)] = mm(wf_ref, 2048 + 512 * c, 2048 + 512 * (c + 1))
    aq_ref[0] = mm(wb_ref, 0, 512).astype(BF16)
    ak_ref[0] = mm(wb_ref, 512, 1024).astype(BF16)
    iq_ref[0] = mm(wb_ref, 1024, 1536).astype(BF16)
    mq_ref[0] = mm(wb_ref, 1536, 2048).astype(BF16)
    ik2_ref[0] = mm(wb_ref, 2048, 2176).astype(BF16)
    t = lax.dot_general(wt_ref[...], h, _NT_DIMS, preferred_element_type=F32)
    pad_row = lax.broadcasted_iota(I32, (V_ROWS - HEAD_DIM, TM), 0)
    pad = jnp.where(pad_row == 0, 1.0, 0.0).astype(BF16)
    for hd in range(N_HEADS):
        avt_ref[0, 0, V_ROWS * hd:V_ROWS * hd + HEAD_DIM, :] = (
            t[HEAD_DIM * hd:HEAD_DIM * (hd + 1)].astype(BF16))
        avt_ref[0, 0, V_ROWS * hd + HEAD_DIM:V_ROWS * (hd + 1), :] = pad
    iwt_ref[0] = t[ATTN_WIDTH:ATTN_WIDTH + IDX_HEADS] * IDX_W_SCALE


def _project(x, g, wf, wb, wt):
    b, s, d = x.shape
    nt = s // TM
    row = lambda w: pl.BlockSpec((1, TM, w), lambda bi, i: (bi, i, 0))
    full = lambda a: pl.BlockSpec(a.shape, lambda bi, i: (0,) * a.ndim)
    f32o = lambda w: jax.ShapeDtypeStruct((b, s, w), F32)
    b16o = lambda w: jax.ShapeDtypeStruct((b, s, w), BF16)
    out_shape = (f32o(512), f32o(512), f32o(512), f32o(512), f32o(N_BRANCH * D_MODEL),
                 b16o(512), b16o(512), b16o(512), b16o(512), b16o(128),
                 jax.ShapeDtypeStruct((b, nt, N_HEADS * V_ROWS, TM), BF16),
                 jax.ShapeDtypeStruct((b, IDX_HEADS, s), F32))
    out_specs = (row(512), row(512), row(512), row(512), row(N_BRANCH * D_MODEL),
                 row(512), row(512), row(512), row(512), row(128),
                 pl.BlockSpec((1, 1, N_HEADS * V_ROWS, TM), lambda bi, i: (bi, i, 0, 0)),
                 pl.BlockSpec((1, IDX_HEADS, TM), lambda bi, i: (bi, 0, i)))
    return pl.pallas_call(
        _proj_kernel,
        grid=(b, nt),
        in_specs=[row(d), full(g), full(wf), full(wb), full(wt)],
        out_specs=out_specs,
        out_shape=out_shape,
        compiler_params=pltpu.CompilerParams(
            dimension_semantics=("arbitrary", "arbitrary"), vmem_limit_bytes=VMEM_LIMIT),
        name="proj",
    )(x, g, wf, wb, wt)


def _memkv_kernel(mem_ref, g_ref, w_ref, mk_ref, mv_ref):
    h = _rms_norm_f32(mem_ref[0], g_ref[...]).astype(BF16)
    mk_ref[0] = jnp.dot(h, w_ref[:, 0:MEM_WIDTH], preferred_element_type=F32).astype(BF16)
    mv_ref[0] = jnp.dot(h, w_ref[:, MEM_WIDTH:2 * MEM_WIDTH], preferred_element_type=F32).astype(BF16)


def _mem_kv(mem, g, w):
    b, m, d = mem.shape
    o = jax.ShapeDtypeStruct((b, m, MEM_WIDTH), BF16)
    return pl.pallas_call(
        _memkv_kernel,
        grid=(b,),
        in_specs=[pl.BlockSpec((1, m, d), lambda bi: (bi, 0, 0)),
                  pl.BlockSpec(g.shape, lambda bi: (0, 0)),
                  pl.BlockSpec(w.shape, lambda bi: (0, 0))],
        out_specs=(pl.BlockSpec((1, m, MEM_WIDTH), lambda bi: (bi, 0, 0)),) * 2,
        out_shape=(o, o),
        compiler_params=pltpu.CompilerParams(
            dimension_semantics=("arbitrary",), vmem_limit_bytes=VMEM_LIMIT),
        name="memkv",
    )(mem, g, w)


def _bucket_table():
    half = REL_BUCKETS // 2
    max_exact = half // 2
    delta = np.arange(N_NEAR + 1, dtype=np.int64)[:, None, None]
    key = np.arange(KT, dtype=np.int64)[None, :, None]
    qry = np.arange(TQ, dtype=np.int64)[None, None, :]
    rel = key - qry - delta * KT
    ret = np.where(rel > 0, half, 0)
    n = np.abs(rel)
    nf = np.maximum(n, 1).astype(np.float32)
    large = max_exact + (np.log(nf / np.float32(max_exact))
                         / np.float32(math.log(REL_MAX_DIST / max_exact))
                         * np.float32(half - max_exact)).astype(np.int32)
    large = np.minimum(large, half - 1)
    return (ret + np.where(n < max_exact, n, large)).astype(np.int32)


FAR_BUCKET = REL_BUCKETS // 2 - 1


def _bias_kernel(rb_ref, bk_ref, o_ref):
    h = pl.program_id(1)
    bk = bk_ref[0]
    far = rb_ref[FAR_BUCKET, h]
    acc = jnp.zeros((KT, TQ), F32)
    for bkt in range(REL_BUCKETS):
        acc = jnp.where(bk == bkt, (rb_ref[bkt, h] - far) * LOG2E, acc)
    o_ref[0, 0] = acc


def _bias_tiles(rel_bias):
    table = jnp.asarray(_bucket_table())
    return pl.pallas_call(
        _bias_kernel,
        grid=(N_NEAR + 1, N_HEADS),
        in_specs=[pl.BlockSpec(memory_space=pltpu.SMEM),
                  pl.BlockSpec((1, KT, TQ), lambda dl, h: (dl, 0, 0))],
        out_specs=pl.BlockSpec((1, 1, KT, TQ), lambda dl, h: (dl, h, 0, 0)),
        out_shape=jax.ShapeDtypeStruct((N_NEAR + 1, N_HEADS, KT, TQ), F32),
        compiler_params=pltpu.CompilerParams(dimension_semantics=("arbitrary", "arbitrary")),
        name="bias_tiles",
    )(rel_bias, table)


def _dsa_kernel(q_ref, iq_ref, iwt_ref, az_ref, ik2_ref, k_ref, vt_ref, bias_ref, o_ref,
                sc_ref, hi_ref, c1_ref, c2_ref, cge_buf, qw_ref, iqw_ref, acc_ref, m_buf, thr_buf,
                stage0_ref, stage1_ref, tmax0_buf, tmax1_buf):
    i = pl.program_id(1)
    n_tiles = i + 1
    cge_ref, thr_ref = cge_buf.at[0:1], thr_buf.at[0:1]
    m_ref, tmax0_ref, tmax1_ref = (r.at[0:N_HEADS] for r in (m_buf, tmax0_buf, tmax1_buf))

    lane = lax.broadcasted_iota(I32, (TQ, LANES), 1)
    for h in range(N_HEADS):
        j = h // 2
        sel = (lane >= HEAD_DIM) if (h % 2) else (lane < HEAD_DIM)
        qw_ref[h] = jnp.where(sel, q_ref[0, :, LANES * j:LANES * (j + 1)].astype(F32), 0.0).astype(BF16)
        iqw_ref[h] = jnp.where(sel, iq_ref[0, :, LANES * j:LANES * (j + 1)].astype(F32), 0.0).astype(BF16)

    key_l = lax.broadcasted_iota(I32, (KT, TQ), 0)
    qry_l = lax.broadcasted_iota(I32, (KT, TQ), 1)
    adm_diag = key_l < ((qry_l // CHUNK) + 1) * CHUNK

    def score_tile(t):
        r0 = pl.multiple_of(t * KT, KT)
        ikt = ik2_ref[0, pl.ds(r0, KT), :]
        acc = jnp.zeros((KT, TQ), F32)
        for h in range(IDX_HEADS):
            y = lax.dot_general(ikt, iqw_ref[h], _NT_DIMS, preferred_element_type=F32)
            acc = acc + jnp.maximum(y, 0.0) * iwt_ref[0, h:h + 1, :]
        adm = jnp.logical_or(jnp.logical_and(adm_diag, t == i), t < i)
        val = jnp.where(adm, acc, NEG)
        sc_ref[pl.ds(r0, KT), :] = val
        top = lax.bitcast_convert_type(val, I32) & I32(-(1 << 16))
        hi_ref[pl.ds(r0, KT), :] = lax.bitcast_convert_type(top, F32).astype(BF16)

    n_steps = (n_tiles + 1) // 2

    def score_step(c, carry):
        score_tile(2 * c)
        score_tile(2 * c + 1)
        return carry

    def score_two_steps(c2, carry):
        score_step(2 * c2, carry)
        return score_step(2 * c2 + 1, carry)

    lax.fori_loop(0, n_steps // 2, score_two_steps, 0)
    lax.fori_loop(n_steps // 2 * 2, n_steps, score_step, 0)

    def count(pred):
        def body(c, acc):
            for k in range(COUNT_ROWS // ACC_ROWS):
                r0 = pl.multiple_of(c * COUNT_ROWS + k * ACC_ROWS, ACC_ROWS)
                x = sc_ref[pl.ds(r0, ACC_ROWS), :]
                acc = jnp.where(pred(x, r0), acc + 1.0, acc)
            return acc
        acc = lax.fori_loop(0, n_steps, body, jnp.zeros((ACC_ROWS, TQ), F32))
        return acc.sum(axis=0, keepdims=True)

    def to_float(u):
        bits = jnp.where(u < 0, u ^ I32(-2 ** 31), ~u)
        return lax.bitcast_convert_type(bits, F32)

    @pl.when(i == 0)
    def _():
        thr_ref[...] = jnp.full((1, TQ), HALF_NEG, F32)

    def count_top(cand_bf16):
        cb = jnp.broadcast_to(cand_bf16, (TOP_ROWS, TQ))

        def body(c, acc):
            for k in range(COUNT_ROWS // TOP_ROWS):
                r0 = pl.multiple_of(c * COUNT_ROWS + k * TOP_ROWS, TOP_ROWS)
                acc = jnp.where(hi_ref[pl.ds(r0, TOP_ROWS), :] >= cb, acc + 1.0, acc)
            return acc
        acc = lax.fori_loop(0, n_steps, body, jnp.zeros((TOP_ROWS, TQ), BF16))
        return acc.astype(F32).sum(axis=0, keepdims=True)

    @pl.when(i > 0)
    def _():
        def top_body(bi, carry):
            u, cnt_u = carry
            cand = u | lax.shift_left(I32(1 << 16), I32(15) - bi)
            top = lax.bitcast_convert_type(to_float(cand), I32) & I32(-(1 << 16))
            cnt = count_top(lax.bitcast_convert_type(top, F32).astype(BF16))
            keep = cnt >= float(TOPK)
            return jnp.where(keep, cand, u), jnp.where(keep, cnt, cnt_u)

        def bit_body(bi, carry):
            u, cnt_u = carry
            cand = u | lax.shift_left(I32(1), I32(15) - bi)
            cf = to_float(cand)
            cnt = count(lambda x, r0: x >= cf)
            keep = cnt >= float(TOPK)
            return jnp.where(keep, cand, u), jnp.where(keep, cnt, cnt_u)

        carry = (jnp.zeros((1, TQ), I32), jnp.zeros((1, TQ), F32))
        carry = lax.fori_loop(0, 16, top_body, carry)

        u_top, cnt_top = carry
        f_lo = to_float(u_top)
        f_hi = to_float(u_top + I32(1 << 16))
        top_hi = lax.bitcast_convert_type(f_hi, I32) & I32(-(1 << 16))
        above = count_top(lax.bitcast_convert_type(top_hi, F32).astype(BF16))
        f_hi_b = jnp.broadcast_to(f_hi, (SUBLANES, TQ))

        def compact_step(c, carry):
            for g in range(COUNT_ROWS // GROUP_ROWS):
                m1 = jnp.full((SUBLANES, TQ), NEG, F32)
                m2 = m1
                for k in range(GROUP_ROWS // SUBLANES):
                    r0 = pl.multiple_of(c * COUNT_ROWS + g * GROUP_ROWS + k * SUBLANES, SUBLANES)
                    v = sc_ref[pl.ds(r0, SUBLANES), :]
                    v = jnp.where(v >= f_hi_b, NEG, v)
                    m2 = jnp.maximum(m2, jnp.minimum(m1, v))
                    m1 = jnp.maximum(m1, v)
                o0 = pl.multiple_of((c * (COUNT_ROWS // GROUP_ROWS) + g) * SUBLANES, SUBLANES)
                c1_ref[pl.ds(o0, SUBLANES), :] = m1
                c2_ref[pl.ds(o0, SUBLANES), :] = m2
            return carry

        lax.fori_loop(0, n_steps, compact_step, 0)

        def count_compact(cf):
            def body(c, acc):
                r0 = pl.multiple_of(c * CROWS, CROWS)
                acc = jnp.where(c1_ref[pl.ds(r0, CROWS), :] >= cf, acc + 1.0, acc)
                return jnp.where(c2_ref[pl.ds(r0, CROWS), :] >= cf, acc + 1.0, acc)
            acc = lax.fori_loop(0, n_steps, body, jnp.zeros((CROWS, TQ), F32))
            return acc.sum(axis=0, keepdims=True)

        captured = count_compact(f_lo)
        missed = jnp.max(jnp.abs(captured - (cnt_top - above)))

        @pl.when(missed == 0.0)
        def _():
            def cbit_body(bi, carry):
                u, cnt_u = carry
                cand = u | lax.shift_left(I32(1), I32(15) - bi)
                cnt = above + count_compact(to_float(cand))
                keep = cnt >= float(TOPK)
                return jnp.where(keep, cand, u), jnp.where(keep, cnt, cnt_u)

            u, cnt = lax.fori_loop(0, 16, cbit_body, carry)
            thr_ref[...] = to_float(u)
            cge_ref[...] = cnt

        @pl.when(missed != 0.0)
        def _():
            u, cnt = lax.fori_loop(0, 16, bit_body, carry)
            thr_ref[...] = to_float(u)
            cge_ref[...] = cnt

        thr = thr_ref[...]
        cnt_ge = cge_ref[...]

        @pl.when(jnp.max(cnt_ge) > float(TOPK))
        def _():
            need = float(TOPK) - count(lambda x, r0: x > thr)
            row = lax.broadcasted_iota(I32, (ACC_ROWS, TQ), 0)
            row_step = lax.broadcasted_iota(I32, (COUNT_ROWS, TQ), 0)
            n_bits = max(1, (sc_ref.shape[0] - 1).bit_length())

            def idx_body(bi, p):
                cand = p | lax.shift_left(I32(1), I32(n_bits - 1) - bi)
                cnt = count(lambda x, r0: jnp.logical_and(x == thr, row + r0 < cand))
                return jnp.where(cnt < need, cand, p)

            last = lax.fori_loop(0, n_bits, idx_body, jnp.zeros((1, TQ), I32))

            def drop(c, carry):
                r0 = pl.multiple_of(c * COUNT_ROWS, COUNT_ROWS)
                x = sc_ref[pl.ds(r0, COUNT_ROWS), :]
                gone = jnp.logical_and(x == thr, row_step + r0 > last)
                sc_ref[pl.ds(r0, COUNT_ROWS), :] = jnp.where(gone, NEG, x)
                return carry

            lax.fori_loop(0, n_steps, drop, 0)

    m_ref[...] = jnp.full(m_ref.shape, NEG, F32)
    acc_ref[...] = jnp.zeros(acc_ref.shape, F32)
    thr = thr_ref[...]

    def to_mask(c, carry):
        r0 = pl.multiple_of(c * KT, KT)
        sc_ref[pl.ds(r0, KT), :] = jnp.where(sc_ref[pl.ds(r0, KT), :] >= thr, F32(0.0), F32(NEG))
        return carry

    lax.fori_loop(0, n_tiles, to_mask, 0)

    stages = ((stage0_ref, tmax0_ref), (stage1_ref, tmax1_ref))

    def head_a(t, slot, h):
        stage_ref, tmax_ref = stages[slot]
        r0 = pl.multiple_of(t * KT, KT)
        dl = jnp.minimum(i - t, N_NEAR)
        j = h // 2
        kk = k_ref[0, pl.ds(r0, KT), LANES * j:LANES * (j + 1)]
        s = lax.dot_general(kk, qw_ref[h], _NT_DIMS, preferred_element_type=F32)
        s = s + sc_ref[pl.ds(r0, KT), :] + bias_ref[dl, h]
        stage_ref[h] = s
        tmax_ref[h:h + 1, :] = s.reshape(KT // ACC_ROWS, ACC_ROWS, TQ).max(axis=0).max(
            axis=0, keepdims=True)

    def begin_b(slot):
        m_old = m_ref[...]
        m_new = jnp.maximum(m_old, stages[slot][1][...])
        m_ref[...] = m_new
        return m_new, jnp.exp2(m_old - m_new)

    def head_b(t, slot, h, m_new, alpha):
        p = jnp.exp2(stages[slot][0][h] - m_new[h:h + 1, :])
        pv = jnp.dot(vt_ref[0, t, V_ROWS * h:V_ROWS * (h + 1), :], p.astype(BF16),
                     preferred_element_type=F32)
        acc_ref[V_ROWS * h:V_ROWS * (h + 1), :] = (
            alpha[h:h + 1, :] * acc_ref[V_ROWS * h:V_ROWS * (h + 1), :] + pv)

    def pass_a(t, slot):
        for h in range(N_HEADS):
            head_a(t, slot, h)

    def pass_b(t, slot):
        m_new, alpha = begin_b(slot)
        for h in range(N_HEADS):
            head_b(t, slot, h, m_new, alpha)

    pass_a(0, 0)
    n_pairs = (n_tiles - 1) // 2

    def attn_pair(pr, carry):
        t = 2 * pr
        pass_a(t + 1, 1)
        pass_b(t, 0)
        pass_a(t + 2, 0)
        pass_b(t + 1, 1)
        return carry

    lax.fori_loop(0, n_pairs, attn_pair, 0)
    t_last = 2 * n_pairs

    @pl.when(n_tiles - 1 > t_last)
    def _():
        pass_a(t_last + 1, 1)
        pass_b(t_last, 0)
        pass_b(t_last + 1, 1)

    @pl.when(n_tiles - 1 == t_last)
    def _():
        pass_b(t_last, 0)

    heads = []
    for h in range(N_HEADS):
        denom = acc_ref[V_ROWS * h + HEAD_DIM:V_ROWS * h + HEAD_DIM + 1, :]
        heads.append(acc_ref[V_ROWS * h:V_ROWS * h + HEAD_DIM, :] * (1.0 / denom))
    y = jnp.concatenate(heads, axis=0).T
    o_ref[0] = (y * jax.nn.silu(az_ref[0])).astype(BF16)


def _dsa(aq, iq, iwt, az, ik2, ak, avt, bias):
    b, s, _ = aq.shape
    nq = s // TQ
    qrow = lambda w: pl.BlockSpec((1, TQ, w), lambda bi, i: (bi, i, 0))
    once = pl.Buffered(1)
    small = pltpu.VMEM((BANK_ROWS, TQ), F32)
    return pl.pallas_call(
        _dsa_kernel,
        grid=(b, nq),
        in_specs=[qrow(ATTN_WIDTH), qrow(IDX_HEADS * IDX_DIM),
                  pl.BlockSpec((1, IDX_HEADS, TQ), lambda bi, i: (bi, 0, i)),
                  qrow(ATTN_WIDTH),
                  pl.BlockSpec((1, s, LANES), lambda bi, i: (bi, 0, 0), pipeline_mode=once),
                  pl.BlockSpec((1, s, ATTN_WIDTH), lambda bi, i: (bi, 0, 0), pipeline_mode=once),
                  pl.BlockSpec((1, s // KT, N_HEADS * V_ROWS, KT), lambda bi, i: (bi, 0, 0, 0),
                               pipeline_mode=once),
                  pl.BlockSpec(bias.shape, lambda bi, i: (0, 0, 0, 0), pipeline_mode=once)],
        out_specs=qrow(ATTN_WIDTH),
        out_shape=jax.ShapeDtypeStruct((b, s, ATTN_WIDTH), BF16),
        scratch_shapes=[pltpu.VMEM((s, TQ), F32),
                        pltpu.VMEM((s, TQ), BF16),
                        pltpu.VMEM((s // GROUP_ROWS * SUBLANES, TQ), F32),
                        pltpu.VMEM((s // GROUP_ROWS * SUBLANES, TQ), F32),
                        small,
                        pltpu.VMEM((N_HEADS, TQ, LANES), BF16),
                        pltpu.VMEM((IDX_HEADS, TQ, LANES), BF16),
                        pltpu.VMEM((N_HEADS * V_ROWS, TQ), F32),
                        small,
                        small,
                        pltpu.VMEM((N_HEADS, KT, TQ), F32),
                        pltpu.VMEM((N_HEADS, KT, TQ), F32),
                        small,
                        small],
        compiler_params=pltpu.CompilerParams(
            dimension_semantics=("arbitrary", "arbitrary"), vmem_limit_bytes=VMEM_LIMIT),
        name="dsa",
    )(aq, iq, iwt, az, ik2, ak, avt, bias)


def _merge_kernel(final, x_ref, pu_ref, halo_ref, pz_ref, mq_ref, mz_ref, ya_ref, gates_ref,
                  mk_ref, mv_ref, pw_ref, ps_ref, wb_ref, wo_ref, fg_ref, o_ref):
    i = pl.program_id(1)

    u = pu_ref[0]
    halo = jnp.where(i == 0, 0.0, halo_ref[0])
    ext = jnp.concatenate([halo, u], axis=0)
    t_glob = i * TM + lax.broadcasted_iota(I32, (TM, POOL_GC), 0)
    mixed = []
    for g in range(POOL_GROUPS):
        sl = slice(POOL_GC * g, POOL_GC * (g + 1))
        wsum = ext[:, sl]
        for step in range(g + 1):
            wsum = wsum + pltpu.roll(wsum, 2 ** step, axis=0)
        cnt = jnp.minimum(t_glob + 1, POOL_WINDOWS[g]).astype(F32)
        pooled = wsum[HALO:, :] / cnt - u[:, sl]
        mixed.append(jnp.dot(pooled.astype(BF16), pw_ref[g], preferred_element_type=F32))
    y_pool = jnp.concatenate(mixed, axis=1) * ps_ref[...] * jax.nn.silu(pz_ref[0])

    mem_out = []
    for h in range(MEM_HEADS):
        sl = slice(MEM_HEAD_DIM * h, MEM_HEAD_DIM * (h + 1))
        logits = lax.dot_general(mq_ref[0, :, sl], mk_ref[0, :, sl], _NT_DIMS,
                                 preferred_element_type=F32) * MEM_SCALE
        e = jnp.exp(logits - logits.max(axis=-1, keepdims=True))
        p = e / e.sum(axis=-1, keepdims=True)
        mem_out.append(jnp.dot(p.astype(BF16), mv_ref[0, :, sl], preferred_element_type=F32))
    y_mem = jnp.concatenate(mem_out, axis=1) * jax.nn.silu(mz_ref[0])

    branches = (y_pool.astype(BF16), ya_ref[0], y_mem.astype(BF16))
    merged = jnp.zeros((TM, D_MODEL), F32)
    for br in range(N_BRANCH):
        gate = jax.nn.sigmoid(gates_ref[0, :, D_MODEL * br:D_MODEL * (br + 1)])
        merged = merged + gate * jnp.dot(branches[br], wb_ref[br], preferred_element_type=F32)
    out = x_ref[0] + jnp.dot(merged.astype(BF16), wo_ref[...], preferred_element_type=F32)
    if final:
        out = _rms_norm_f32(out, fg_ref[...])
    o_ref[0] = out


def _merge(final, x, pu, pz, mq, mz, ya, gates, mk, mv, pw, ps, wb, wo, fg):
    b, s, d = x.shape
    row = lambda w: pl.BlockSpec((1, TM, w), lambda bi, i: (bi, i, 0))
    full = lambda a: pl.BlockSpec(a.shape, lambda bi, i: (0,) * a.ndim)
    per_b = lambda a: pl.BlockSpec((1,) + a.shape[1:], lambda bi, i: (bi,) + (0,) * (a.ndim - 1))
    halo = pl.BlockSpec((1, HALO, POOL_WIDTH),
                        lambda bi, i: (bi, jnp.maximum(i * (TM // HALO) - 1, 0), 0))
    return pl.pallas_call(
        functools.partial(_merge_kernel, final),
        grid=(b, s // TM),
        in_specs=[row(d), row(POOL_WIDTH), halo, row(POOL_WIDTH), row(MEM_WIDTH), row(MEM_WIDTH),
                  row(ATTN_WIDTH), row(N_BRANCH * D_MODEL), per_b(mk), per_b(mv),
                  full(pw), full(ps), full(wb), full(wo), full(fg)],
        out_specs=row(d),
        out_shape=jax.ShapeDtypeStruct((b, s, d), F32),
        compiler_params=pltpu.CompilerParams(
            dimension_semantics=("arbitrary", "arbitrary"), vmem_limit_bytes=VMEM_LIMIT),
        name="merge_final" if final else "merge",
    )(x, pu, pu, pz, mq, mz, ya, gates, mk, mv, pw, ps, wb, wo, fg)


def _split_w_in(w):
    col = lambda lo, hi: w[:, lo:hi]
    wf = jnp.concatenate([col(O_PU, O_PZ), col(O_PZ, O_AQ), col(O_AZ, O_IQ), col(O_MZ, O_G),
                          col(O_G, O_END)], axis=1)
    wb = jnp.concatenate([col(O_AQ, O_AK) * (ATTN_SCALE * LOG2E), col(O_AK, O_AV), col(O_IQ, O_IK),
                          col(O_MQ, O_MZ), col(O_IK, O_IW), col(O_IK, O_IW)], axis=1)
    wt = jnp.concatenate([col(O_AV, O_AZ), col(O_IW, O_MQ),
                          jnp.zeros((w.shape[0], WT_ROWS - ATTN_WIDTH - IDX_HEADS), w.dtype)], axis=1).T
    return wf.astype(BF16), wb.astype(BF16), wt.astype(BF16)


def kernel(x, mem, norm_g, w_in, pool_w, pool_scale, mem_norm_g, w_mem_kv, w_branch, w_out,
           rel_bias, final_g):
    depth = w_in.shape[0]
    assert x.shape[1] % COUNT_ROWS == 0 and x.shape[2] == D_MODEL and mem.shape[1] == MEM_LEN
    assert w_in.shape[2] == O_END and TQ == KT == TM
    assert x.shape[1] // TOP_ROWS <= 256
    bias = _bias_tiles(rel_bias)
    fg = final_g.reshape(1, D_MODEL)
    for l in range(depth):
        wf, wb, wt = _split_w_in(w_in[l])
        (pu, pz, az, mz, gates, aq, ak, iq, mq, ik2, avt, iwt) = _project(
            x, norm_g[l].reshape(1, D_MODEL), wf, wb, wt)
        mk, mv = _mem_kv(mem, mem_norm_g[l].reshape(1, D_MODEL), w_mem_kv[l].astype(BF16))
        ya = _dsa(aq, iq, iwt, az, ik2, ak, avt, bias)
        x = _merge(l == depth - 1, x, pu, pz, mq, mz, ya, gates, mk, mv,
                   pool_w[l].astype(BF16), pool_scale[l].reshape(1, POOL_WIDTH),
                   w_branch[l].astype(BF16), w_out[l].astype(BF16), fg)
    return x
```

```python
import functools
import math

import numpy as np
import jax
import jax.numpy as jnp
from jax import lax
from jax.experimental import pallas as pl
from jax.experimental.pallas import tpu as pltpu

F32 = jnp.float32
BF16 = jnp.bfloat16
I32 = jnp.int32

D_MODEL = 1024
CHUNK = 64
MEM_LEN = 256
POOL_GROUPS = 4
POOL_WINDOWS = (2, 4, 8, 16)
POOL_WIDTH = D_MODEL // 2
POOL_GC = POOL_WIDTH // POOL_GROUPS
N_HEADS = 8
HEAD_DIM = 64
ATTN_WIDTH = N_HEADS * HEAD_DIM
IDX_HEADS = 8
IDX_DIM = 64
TOPK = 256
ATTN_SCALE = HEAD_DIM ** -0.5
LOG2E = math.log2(math.e)
IDX_W_SCALE = (IDX_DIM ** -0.5) * (IDX_HEADS ** -0.5)
MEM_HEADS = 4
MEM_HEAD_DIM = 128
MEM_WIDTH = MEM_HEADS * MEM_HEAD_DIM
MEM_SCALE = MEM_HEAD_DIM ** -0.5
N_BRANCH = 3
REL_BUCKETS = 32
REL_MAX_DIST = 1024
EPS = 1e-6
NEG = -1e30
HALF_NEG = -5e29

IN_SPLITS = (POOL_WIDTH, POOL_WIDTH, ATTN_WIDTH, ATTN_WIDTH, ATTN_WIDTH, ATTN_WIDTH,
             IDX_HEADS * IDX_DIM, IDX_DIM, IDX_HEADS, MEM_WIDTH, MEM_WIDTH, N_BRANCH * D_MODEL)
_OFF = np.concatenate([[0], np.cumsum(IN_SPLITS)]).tolist()
(O_PU, O_PZ, O_AQ, O_AK, O_AV, O_AZ, O_IQ, O_IK, O_IW, O_MQ, O_MZ, O_G, O_END) = _OFF

LANES = 128
SUBLANES = 8
TQ = 256
KT = 256
TM = 256
HALO = 16
N_NEAR = 4
COUNT_ROWS = 2 * KT
ACC_ROWS = 4 * SUBLANES
TOP_ROWS = 8 * SUBLANES
BANK_ROWS = 16
GROUP_ROWS = 128
CROWS = COUNT_ROWS // GROUP_ROWS * SUBLANES
WT_ROWS = 528
V_ROWS = HEAD_DIM + 16
VMEM_LIMIT = 56 * 1024 * 1024

_NT_DIMS = (((1,), (1,)), ((), ()))


def _rms_norm_f32(x, g):
    ms = jnp.mean(x * x, axis=-1, keepdims=True)
    return x * lax.rsqrt(ms + EPS) * g


def _proj_kernel(x_ref, g_ref, wf_ref, wb_ref, wt_ref,
                 pu_ref, pz_ref, az_ref, mz_ref, gates_ref,
                 aq_ref, ak_ref, iq_ref, mq_ref, ik2_ref, avt_ref, iwt_ref):
    h = _rms_norm_f32(x_ref[0], g_ref[...]).astype(BF16)

    def mm(w_ref, lo, hi):
        return jnp.dot(h, w_ref[:, lo:hi], preferred_element_type=F32)

    pu_ref[0] = mm(wf_ref, 0, 512)
    pz_ref[0] = mm(wf_ref, 512, 1024)
    az_ref[0] = mm(wf_ref, 1024, 1536)
    mz_ref[0] = mm(wf_ref, 1536, 2048)
    for c in range(N_BRANCH * D_MODEL // 512):
        gates_ref[0, :, 512 * c:512 * (c + 1)] = mm(wf_ref, 2048 + 512 * c, 2048 + 512 * (c + 1))
    aq_ref[0] = mm(wb_ref, 0, 512).astype(BF16)
    ak_ref[0] = mm(wb_ref, 512, 1024).astype(BF16)
    iq_ref[0] = mm(wb_ref, 1024, 1536).astype(BF16)
    mq_ref[0] = mm(wb_ref, 1536, 2048).astype(BF16)
    ik2_ref[0] = mm(wb_ref, 2048, 2176).astype(BF16)
    t = lax.dot_general(wt_ref[...], h, _NT_DIMS, preferred_element_type=F32)
    pad_row = lax.broadcasted_iota(I32, (V_ROWS - HEAD_DIM, TM), 0)
    pad = jnp.where(pad_row == 0, 1.0, 0.0).astype(BF16)
    for hd in range(N_HEADS):
        avt_ref[0, 0, V_ROWS * hd:V_ROWS * hd + HEAD_DIM, :] = (
            t[HEAD_DIM * hd:HEAD_DIM * (hd + 1)].astype(BF16))
        avt_ref[0, 0, V_ROWS * hd + HEAD_DIM:V_ROWS * (hd + 1), :] = pad
    iwt_ref[0] = t[ATTN_WIDTH:ATTN_WIDTH + IDX_HEADS] * IDX_W_SCALE


def _project(x, g, wf, wb, wt):
    b, s, d = x.shape
    nt = s // TM
    row = lambda w: pl.BlockSpec((1, TM, w), lambda bi, i: (bi, i, 0))
    full = lambda a: pl.BlockSpec(a.shape, lambda bi, i: (0,) * a.ndim)
    f32o = lambda w: jax.ShapeDtypeStruct((b, s, w), F32)
    b16o = lambda w: jax.ShapeDtypeStruct((b, s, w), BF16)
    out_shape = (f32o(512), f32o(512), f32o(512), f32o(512), f32o(N_BRANCH * D_MODEL),
                 b16o(512), b16o(512), b16o(512), b16o(512), b16o(128),
                 jax.ShapeDtypeStruct((b, nt, N_HEADS * V_ROWS, TM), BF16),
                 jax.ShapeDtypeStruct((b, IDX_HEADS, s), F32))
    out_specs = (row(512), row(512), row(512), row(512), row(N_BRANCH * D_MODEL),
                 row(512), row(512), row(512), row(512), row(128),
                 pl.BlockSpec((1, 1, N_HEADS * V_ROWS, TM), lambda bi, i: (bi, i, 0, 0)),
                 pl.BlockSpec((1, IDX_HEADS, TM), lambda bi, i: (bi, 0, i)))
    return pl.pallas_call(
        _proj_kernel,
        grid=(b, nt),
        in_specs=[row(d), full(g), full(wf), full(wb), full(wt)],
        out_specs=out_specs,
        out_shape=out_shape,
        compiler_params=pltpu.CompilerParams(
            dimension_semantics=("arbitrary", "arbitrary"), vmem_limit_bytes=VMEM_LIMIT),
        name="proj",
    )(x, g, wf, wb, wt)


def _memkv_kernel(mem_ref, g_ref, w_ref, mk_ref, mv_ref):
    h = _rms_norm_f32(mem_ref[0], g_ref[...]).astype(BF16)
    mk_ref[0] = jnp.dot(h, w_ref[:, 0:MEM_WIDTH], preferred_element_type=F32).astype(BF16)
    mv_ref[0] = jnp.dot(h, w_ref[:, MEM_WIDTH:2 * MEM_WIDTH], preferred_element_type=F32).astype(BF16)


def _mem_kv(mem, g, w):
    b, m, d = mem.shape
    o = jax.ShapeDtypeStruct((b, m, MEM_WIDTH), BF16)
    return pl.pallas_call(
        _memkv_kernel,
        grid=(b,),
        in_specs=[pl.BlockSpec((1, m, d), lambda bi: (bi, 0, 0)),
                  pl.BlockSpec(g.shape, lambda bi: (0, 0)),
                  pl.BlockSpec(w.shape, lambda bi: (0, 0))],
        out_specs=(pl.BlockSpec((1, m, MEM_WIDTH), lambda bi: (bi, 0, 0)),) * 2,
        out_shape=(o, o),
        compiler_params=pltpu.CompilerParams(
            dimension_semantics=("arbitrary",), vmem_limit_bytes=VMEM_LIMIT),
        name="memkv",
    )(mem, g, w)


def _bucket_table():
    half = REL_BUCKETS // 2
    max_exact = half // 2
    delta = np.arange(N_NEAR + 1, dtype=np.int64)[:, None, None]
    key = np.arange(KT, dtype=np.int64)[None, :, None]
    qry = np.arange(TQ, dtype=np.int64)[None, None, :]
    rel = key - qry - delta * KT
    ret = np.where(rel > 0, half, 0)
    n = np.abs(rel)
    nf = np.maximum(n, 1).astype(np.float32)
    large = max_exact + (np.log(nf / np.float32(max_exact))
                         / np.float32(math.log(REL_MAX_DIST / max_exact))
                         * np.float32(half - max_exact)).astype(np.int32)
    large = np.minimum(large, half - 1)
    return (ret + np.where(n < max_exact, n, large)).astype(np.int32)


FAR_BUCKET = REL_BUCKETS // 2 - 1


def _bias_kernel(rb_ref, bk_ref, o_ref):
    h = pl.program_id(1)
    bk = bk_ref[0]
    far = rb_ref[FAR_BUCKET, h]
    acc = jnp.zeros((KT, TQ), F32)
    for bkt in range(REL_BUCKETS):
        acc = jnp.where(bk == bkt, (rb_ref[bkt, h] - far) * LOG2E, acc)
    o_ref[0, 0] = acc


def _bias_tiles(rel_bias):
    table = jnp.asarray(_bucket_table())
    return pl.pallas_call(
        _bias_kernel,
        grid=(N_NEAR + 1, N_HEADS),
        in_specs=[pl.BlockSpec(memory_space=pltpu.SMEM),
                  pl.BlockSpec((1, KT, TQ), lambda dl, h: (dl, 0, 0))],
        out_specs=pl.BlockSpec((1, 1, KT, TQ), lambda dl, h: (dl, h, 0, 0)),
        out_shape=jax.ShapeDtypeStruct((N_NEAR + 1, N_HEADS, KT, TQ), F32),
        compiler_params=pltpu.CompilerParams(dimension_semantics=("arbitrary", "arbitrary")),
        name="bias_tiles",
    )(rel_bias, table)


def _dsa_kernel(q_ref, iq_ref, iwt_ref, az_ref, ik2_ref, k_ref, vt_ref, bias_ref, o_ref,
                sc_ref, hi_ref, c1_ref, c2_ref, cge_buf, qw_ref, iqw_ref, acc_ref, m_buf, thr_buf,
                stage0_ref, stage1_ref, tmax0_buf, tmax1_buf):
    i = pl.program_id(1)
    n_tiles = i + 1
    cge_ref, thr_ref = cge_buf.at[0:1], thr_buf.at[0:1]
    m_ref, tmax0_ref, tmax1_ref = (r.at[0:N_HEADS] for r in (m_buf, tmax0_buf, tmax1_buf))

    lane = lax.broadcasted_iota(I32, (TQ, LANES), 1)
    for h in range(N_HEADS):
        j = h // 2
        sel = (lane >= HEAD_DIM) if (h % 2) else (lane < HEAD_DIM)
        qw_ref[h] = jnp.where(sel, q_ref[0, :, LANES * j:LANES * (j + 1)].astype(F32), 0.0).astype(BF16)
        iqw_ref[h] = jnp.where(sel, iq_ref[0, :, LANES * j:LANES * (j + 1)].astype(F32), 0.0).astype(BF16)

    key_l = lax.broadcasted_iota(I32, (KT, TQ), 0)
    qry_l = lax.broadcasted_iota(I32, (KT, TQ), 1)
    adm_diag = key_l < ((qry_l // CHUNK) + 1) * CHUNK

    def score_tile(t):
        r0 = pl.multiple_of(t * KT, KT)
        ikt = ik2_ref[0, pl.ds(r0, KT), :]
        acc = jnp.zeros((KT, TQ), F32)
        for h in range(IDX_HEADS):
            y = lax.dot_general(ikt, iqw_ref[h], _NT_DIMS, preferred_element_type=F32)
            acc = acc + jnp.maximum(y, 0.0) * iwt_ref[0, h:h + 1, :]
        adm = jnp.logical_or(jnp.logical_and(adm_diag, t == i), t < i)
        val = jnp.where(adm, acc, NEG)
        sc_ref[pl.ds(r0, KT), :] = val
        top = lax.bitcast_convert_type(val, I32) & I32(-(1 << 16))
        hi_ref[pl.ds(r0, KT), :] = lax.bitcast_convert_type(top, F32).astype(BF16)

    n_steps = (n_tiles + 1) // 2

    def score_step(c, carry):
        score_tile(2 * c)
        score_tile(2 * c + 1)
        return carry

    def score_two_steps(c2, carry):
        score_step(2 * c2, carry)
        return score_step(2 * c2 + 1, carry)

    lax.fori_loop(0, n_steps // 2, score_two_steps, 0)
    lax.fori_loop(n_steps // 2 * 2, n_steps, score_step, 0)

    def count(pred):
        def body(c, acc):
            for k in range(COUNT_ROWS // ACC_ROWS):
                r0 = pl.multiple_of(c * COUNT_ROWS + k * ACC_ROWS, ACC_ROWS)
                x = sc_ref[pl.ds(r0, ACC_ROWS), :]
                acc = jnp.where(pred(x, r0), acc + 1.0, acc)
            return acc
        acc = lax.fori_loop(0, n_steps, body, jnp.zeros((ACC_ROWS, TQ), F32))
        return acc.sum(axis=0, keepdims=True)

    def to_float(u):
        bits = jnp.where(u < 0, u ^ I32(-2 ** 31), ~u)
        return lax.bitcast_convert_type(bits, F32)

    @pl.when(i == 0)
    def _():
        thr_ref[...] = jnp.full((1, TQ), HALF_NEG, F32)

    def count_top(cand_bf16):
        cb = jnp.broadcast_to(cand_bf16, (TOP_ROWS, TQ))

        def body(c, acc):
            for k in range(COUNT_ROWS // TOP_ROWS):
                r0 = pl.multiple_of(c * COUNT_ROWS + k * TOP_ROWS, TOP_ROWS)
                acc = jnp.where(hi_ref[pl.ds(r0, TOP_ROWS), :] >= cb, acc + 1.0, acc)
            return acc
        acc = lax.fori_loop(0, n_steps, body, jnp.zeros((TOP_ROWS, TQ), BF16))
        return acc.astype(F32).sum(axis=0, keepdims=True)

    @pl.when(i > 0)
    def _():
        def top_body(bi, carry):
            u, cnt_u = carry
            cand = u | lax.shift_left(I32(1 << 16), I32(15) - bi)
            top = lax.bitcast_convert_type(to_float(cand), I32) & I32(-(1 << 16))
            cnt = count_top(lax.bitcast_convert_type(top, F32).astype(BF16))
            keep = cnt >= float(TOPK)
            return jnp.where(keep, cand, u), jnp.where(keep, cnt, cnt_u)

        def bit_body(bi, carry):
            u, cnt_u = carry
            cand = u | lax.shift_left(I32(1), I32(15) - bi)
            cf = to_float(cand)
            cnt = count(lambda x, r0: x >= cf)
            keep = cnt >= float(TOPK)
            return jnp.where(keep, cand, u), jnp.where(keep, cnt, cnt_u)

        carry = (jnp.zeros((1, TQ), I32), jnp.zeros((1, TQ), F32))
        carry = lax.fori_loop(0, 16, top_body, carry)

        u_top, cnt_top = carry
        f_lo = to_float(u_top)
        f_hi = to_float(u_top + I32(1 << 16))
        top_hi = lax.bitcast_convert_type(f_hi, I32) & I32(-(1 << 16))
        above = count_top(lax.bitcast_convert_type(top_hi, F32).astype(BF16))
        f_hi_b = jnp.broadcast_to(f_hi, (SUBLANES, TQ))

        def compact_step(c, carry):
            for g in range(COUNT_ROWS // GROUP_ROWS):
                m1 = jnp.full((SUBLANES, TQ), NEG, F32)
                m2 = m1
                for k in range(GROUP_ROWS // SUBLANES):
                    r0 = pl.multiple_of(c * COUNT_ROWS + g * GROUP_ROWS + k * SUBLANES, SUBLANES)
                    v = sc_ref[pl.ds(r0, SUBLANES), :]
                    v = jnp.where(v >= f_hi_b, NEG, v)
                    m2 = jnp.maximum(m2, jnp.minimum(m1, v))
                    m1 = jnp.maximum(m1, v)
                o0 = pl.multiple_of((c * (COUNT_ROWS // GROUP_ROWS) + g) * SUBLANES, SUBLANES)
                c1_ref[pl.ds(o0, SUBLANES), :] = m1
                c2_ref[pl.ds(o0, SUBLANES), :] = m2
            return carry

        lax.fori_loop(0, n_steps, compact_step, 0)

        def count_compact(cf):
            def body(c, acc):
                r0 = pl.multiple_of(c * CROWS, CROWS)
                acc = jnp.where(c1_ref[pl.ds(r0, CROWS), :] >= cf, acc + 1.0, acc)
                return jnp.where(c2_ref[pl.ds(r0, CROWS), :] >= cf, acc + 1.0, acc)
            acc = lax.fori_loop(0, n_steps, body, jnp.zeros((CROWS, TQ), F32))
            return acc.sum(axis=0, keepdims=True)

        captured = count_compact(f_lo)
        missed = jnp.max(jnp.abs(captured - (cnt_top - above)))

        @pl.when(missed == 0.0)
        def _():
            def cbit_body(bi, carry):
                u, cnt_u = carry
                cand = u | lax.shift_left(I32(1), I32(15) - bi)
                cnt = above + count_compact(to_float(cand))
                keep = cnt >= float(TOPK)
                return jnp.where(keep, cand, u), jnp.where(keep, cnt, cnt_u)

            u, cnt = lax.fori_loop(0, 16, cbit_body, carry)
            thr_ref[...] = to_float(u)
            cge_ref[...] = cnt

        @pl.when(missed != 0.0)
        def _():
            u, cnt = lax.fori_loop(0, 16, bit_body, carry)
            thr_ref[...] = to_float(u)
            cge_ref[...] = cnt

        thr = thr_ref[...]
        cnt_ge = cge_ref[...]

        @pl.when(jnp.max(cnt_ge) > float(TOPK))
        def _():
            def mark(c, acc):
                for k in range(COUNT_ROWS // TOP_ROWS):
                    r0 = pl.multiple_of(c * COUNT_ROWS + k * TOP_ROWS, TOP_ROWS)
                    e = jnp.where(sc_ref[pl.ds(r0, TOP_ROWS), :] == thr, 1.0, 0.0).astype(BF16)
                    hi_ref[pl.ds(r0, TOP_ROWS), :] = e
                    acc = acc + e
                return acc

            n_eq = lax.fori_loop(0, n_steps, mark, jnp.zeros((TOP_ROWS, TQ), BF16))
            n_eq = n_eq.astype(F32).sum(axis=0, keepdims=True)
            need = float(TOPK) - (cnt_ge - n_eq)
            local = lax.broadcasted_iota(I32, (TOP_ROWS, TQ), 0).astype(F32).astype(BF16)
            row_step = lax.broadcasted_iota(I32, (COUNT_ROWS, TQ), 0)
            n_bits = max(1, (sc_ref.shape[0] - 1).bit_length())

            def idx_body(bi, p):
                cand = p | lax.shift_left(I32(1), I32(n_bits - 1) - bi)

                def body(c, acc):
                    for k in range(COUNT_ROWS // TOP_ROWS):
                        r0 = pl.multiple_of(c * COUNT_ROWS + k * TOP_ROWS, TOP_ROWS)
                        rel = jnp.clip(cand - r0, 0, TOP_ROWS).astype(F32).astype(BF16)
                        e = hi_ref[pl.ds(r0, TOP_ROWS), :]
                        acc = acc + jnp.where(local < rel, e, jnp.zeros_like(e))
                    return acc

                cnt = lax.fori_loop(0, n_steps, body, jnp.zeros((TOP_ROWS, TQ), BF16))
                cnt = cnt.astype(F32).sum(axis=0, keepdims=True)
                return jnp.where(cnt < need, cand, p)

            last = lax.fori_loop(0, n_bits, idx_body, jnp.zeros((1, TQ), I32))

            def drop(c, carry):
                r0 = pl.multiple_of(c * COUNT_ROWS, COUNT_ROWS)
                x = sc_ref[pl.ds(r0, COUNT_ROWS), :]
                gone = jnp.logical_and(x == thr, row_step + r0 > last)
                sc_ref[pl.ds(r0, COUNT_ROWS), :] = jnp.where(gone, NEG, x)
                return carry

            lax.fori_loop(0, n_steps, drop, 0)

    m_ref[...] = jnp.full(m_ref.shape, NEG, F32)
    acc_ref[...] = jnp.zeros(acc_ref.shape, F32)
    thr = thr_ref[...]

    def to_mask(c, carry):
        r0 = pl.multiple_of(c * KT, KT)
        sc_ref[pl.ds(r0, KT), :] = jnp.where(sc_ref[pl.ds(r0, KT), :] >= thr, F32(0.0), F32(NEG))
        return carry

    lax.fori_loop(0, n_tiles, to_mask, 0)

    stages = ((stage0_ref, tmax0_ref), (stage1_ref, tmax1_ref))

    def head_a(t, slot, h):
        stage_ref, tmax_ref = stages[slot]
        r0 = pl.multiple_of(t * KT, KT)
        dl = jnp.minimum(i - t, N_NEAR)
        j = h // 2
        kk = k_ref[0, pl.ds(r0, KT), LANES * j:LANES * (j + 1)]
        s = lax.dot_general(kk, qw_ref[h], _NT_DIMS, preferred_element_type=F32)
        s = s + sc_ref[pl.ds(r0, KT), :] + bias_ref[dl, h]
        stage_ref[h] = s
        tmax_ref[h:h + 1, :] = s.reshape(KT // ACC_ROWS, ACC_ROWS, TQ).max(axis=0).max(
            axis=0, keepdims=True)

    def begin_b(slot):
        m_old = m_ref[...]
        m_new = jnp.maximum(m_old, stages[slot][1][...])
        m_ref[...] = m_new
        return m_new, jnp.exp2(m_old - m_new)

    def head_b(t, slot, h, m_new, alpha):
        p = jnp.exp2(stages[slot][0][h] - m_new[h:h + 1, :])
        pv = jnp.dot(vt_ref[0, t, V_ROWS * h:V_ROWS * (h + 1), :], p.astype(BF16),
                     preferred_element_type=F32)
        acc_ref[V_ROWS * h:V_ROWS * (h + 1), :] = (
            alpha[h:h + 1, :] * acc_ref[V_ROWS * h:V_ROWS * (h + 1), :] + pv)

    def pass_a(t, slot):
        for h in range(N_HEADS):
            head_a(t, slot, h)

    def pass_b(t, slot):
        m_new, alpha = begin_b(slot)
        for h in range(N_HEADS):
            head_b(t, slot, h, m_new, alpha)

    pass_a(0, 0)
    n_pairs = (n_tiles - 1) // 2

    def attn_pair(pr, carry):
        t = 2 * pr
        pass_a(t + 1, 1)
        pass_b(t, 0)
        pass_a(t + 2, 0)
        pass_b(t + 1, 1)
        return carry

    lax.fori_loop(0, n_pairs, attn_pair, 0)
    t_last = 2 * n_pairs

    @pl.when(n_tiles - 1 > t_last)
    def _():
        pass_a(t_last + 1, 1)
        pass_b(t_last, 0)
        pass_b(t_last + 1, 1)

    @pl.when(n_tiles - 1 == t_last)
    def _():
        pass_b(t_last, 0)

    heads = []
    for h in range(N_HEADS):
        denom = acc_ref[V_ROWS * h + HEAD_DIM:V_ROWS * h + HEAD_DIM + 1, :]
        heads.append(acc_ref[V_ROWS * h:V_ROWS * h + HEAD_DIM, :] * (1.0 / denom))
    y = jnp.concatenate(heads, axis=0).T
    o_ref[0] = (y * jax.nn.silu(az_ref[0])).astype(BF16)


def _dsa(aq, iq, iwt, az, ik2, ak, avt, bias):
    b, s, _ = aq.shape
    nq = s // TQ
    qrow = lambda w: pl.BlockSpec((1, TQ, w), lambda bi, i: (bi, i, 0))
    once = pl.Buffered(1)
    small = pltpu.VMEM((BANK_ROWS, TQ), F32)
    return pl.pallas_call(
        _dsa_kernel,
        grid=(b, nq),
        in_specs=[qrow(ATTN_WIDTH), qrow(IDX_HEADS * IDX_DIM),
                  pl.BlockSpec((1, IDX_HEADS, TQ), lambda bi, i: (bi, 0, i)),
                  qrow(ATTN_WIDTH),
                  pl.BlockSpec((1, s, LANES), lambda bi, i: (bi, 0, 0), pipeline_mode=once),
                  pl.BlockSpec((1, s, ATTN_WIDTH), lambda bi, i: (bi, 0, 0), pipeline_mode=once),
                  pl.BlockSpec((1, s // KT, N_HEADS * V_ROWS, KT), lambda bi, i: (bi, 0, 0, 0),
                               pipeline_mode=once),
                  pl.BlockSpec(bias.shape, lambda bi, i: (0, 0, 0, 0), pipeline_mode=once)],
        out_specs=qrow(ATTN_WIDTH),
        out_shape=jax.ShapeDtypeStruct((b, s, ATTN_WIDTH), BF16),
        scratch_shapes=[pltpu.VMEM((s, TQ), F32),
                        pltpu.VMEM((s, TQ), BF16),
                        pltpu.VMEM((s // GROUP_ROWS * SUBLANES, TQ), F32),
                        pltpu.VMEM((s // GROUP_ROWS * SUBLANES, TQ), F32),
                        small,
                        pltpu.VMEM((N_HEADS, TQ, LANES), BF16),
                        pltpu.VMEM((IDX_HEADS, TQ, LANES), BF16),
                        pltpu.VMEM((N_HEADS * V_ROWS, TQ), F32),
                        small,
                        small,
                        pltpu.VMEM((N_HEADS, KT, TQ), F32),
                        pltpu.VMEM((N_HEADS, KT, TQ), F32),
                        small,
                        small],
        compiler_params=pltpu.CompilerParams(
            dimension_semantics=("arbitrary", "arbitrary"), vmem_limit_bytes=VMEM_LIMIT),
        name="dsa",
    )(aq, iq, iwt, az, ik2, ak, avt, bias)


def _merge_kernel(final, x_ref, pu_ref, halo_ref, pz_ref, mq_ref, mz_ref, ya_ref, gates_ref,
                  mk_ref, mv_ref, pw_ref, ps_ref, wb_ref, wo_ref, fg_ref, o_ref):
    i = pl.program_id(1)

    u = pu_ref[0]
    halo = jnp.where(i == 0, 0.0, halo_ref[0])
    ext = jnp.concatenate([halo, u], axis=0)
    t_glob = i * TM + lax.broadcasted_iota(I32, (TM, POOL_GC), 0)
    mixed = []
    for g in range(POOL_GROUPS):
        sl = slice(POOL_GC * g, POOL_GC * (g + 1))
        wsum = ext[:, sl]
        for step in range(g + 1):
            wsum = wsum + pltpu.roll(wsum, 2 ** step, axis=0)
        cnt = jnp.minimum(t_glob + 1, POOL_WINDOWS[g]).astype(F32)
        pooled = wsum[HALO:, :] / cnt - u[:, sl]
        mixed.append(jnp.dot(pooled.astype(BF16), pw_ref[g], preferred_element_type=F32))
    y_pool = jnp.concatenate(mixed, axis=1) * ps_ref[...] * jax.nn.silu(pz_ref[0])

    mem_out = []
    for h in range(MEM_HEADS):
        sl = slice(MEM_HEAD_DIM * h, MEM_HEAD_DIM * (h + 1))
        logits = lax.dot_general(mq_ref[0, :, sl], mk_ref[0, :, sl], _NT_DIMS,
                                 preferred_element_type=F32) * MEM_SCALE
        e = jnp.exp(logits - logits.max(axis=-1, keepdims=True))
        p = e / e.sum(axis=-1, keepdims=True)
        mem_out.append(jnp.dot(p.astype(BF16), mv_ref[0, :, sl], preferred_element_type=F32))
    y_mem = jnp.concatenate(mem_out, axis=1) * jax.nn.silu(mz_ref[0])

    branches = (y_pool.astype(BF16), ya_ref[0], y_mem.astype(BF16))
    merged = jnp.zeros((TM, D_MODEL), F32)
    for br in range(N_BRANCH):
        gate = jax.nn.sigmoid(gates_ref[0, :, D_MODEL * br:D_MODEL * (br + 1)])
        merged = merged + gate * jnp.dot(branches[br], wb_ref[br], preferred_element_type=F32)
    out = x_ref[0] + jnp.dot(merged.astype(BF16), wo_ref[...], preferred_element_type=F32)
    if final:
        out = _rms_norm_f32(out, fg_ref[...])
    o_ref[0] = out


def _merge(final, x, pu, pz, mq, mz, ya, gates, mk, mv, pw, ps, wb, wo, fg):
    b, s, d = x.shape
    row = lambda w: pl.BlockSpec((1, TM, w), lambda bi, i: (bi, i, 0))
    full = lambda a: pl.BlockSpec(a.shape, lambda bi, i: (0,) * a.ndim)
    per_b = lambda a: pl.BlockSpec((1,) + a.shape[1:], lambda bi, i: (bi,) + (0,) * (a.ndim - 1))
    halo = pl.BlockSpec((1, HALO, POOL_WIDTH),
                        lambda bi, i: (bi, jnp.maximum(i * (TM // HALO) - 1, 0), 0))
    return pl.pallas_call(
        functools.partial(_merge_kernel, final),
        grid=(b, s // TM),
        in_specs=[row(d), row(POOL_WIDTH), halo, row(POOL_WIDTH), row(MEM_WIDTH), row(MEM_WIDTH),
                  row(ATTN_WIDTH), row(N_BRANCH * D_MODEL), per_b(mk), per_b(mv),
                  full(pw), full(ps), full(wb), full(wo), full(fg)],
        out_specs=row(d),
        out_shape=jax.ShapeDtypeStruct((b, s, d), F32),
        compiler_params=pltpu.CompilerParams(
            dimension_semantics=("arbitrary", "arbitrary"), vmem_limit_bytes=VMEM_LIMIT),
        name="merge_final" if final else "merge",
    )(x, pu, pu, pz, mq, mz, ya, gates, mk, mv, pw, ps, wb, wo, fg)


def _split_w_in(w):
    col = lambda lo, hi: w[:, lo:hi]
    wf = jnp.concatenate([col(O_PU, O_PZ), col(O_PZ, O_AQ), col(O_AZ, O_IQ), col(O_MZ, O_G),
                          col(O_G, O_END)], axis=1)
    wb = jnp.concatenate([col(O_AQ, O_AK) * (ATTN_SCALE * LOG2E), col(O_AK, O_AV), col(O_IQ, O_IK),
                          col(O_MQ, O_MZ), col(O_IK, O_IW), col(O_IK, O_IW)], axis=1)
    wt = jnp.concatenate([col(O_AV, O_AZ), col(O_IW, O_MQ),
                          jnp.zeros((w.shape[0], WT_ROWS - ATTN_WIDTH - IDX_HEADS), w.dtype)], axis=1).T
    return wf.astype(BF16), wb.astype(BF16), wt.astype(BF16)


def kernel(x, mem, norm_g, w_in, pool_w, pool_scale, mem_norm_g, w_mem_kv, w_branch, w_out,
           rel_bias, final_g):
    depth = w_in.shape[0]
    assert x.shape[1] % COUNT_ROWS == 0 and x.shape[2] == D_MODEL and mem.shape[1] == MEM_LEN
    assert w_in.shape[2] == O_END and TQ == KT == TM
    assert x.shape[1] // TOP_ROWS <= 256
    bias = _bias_tiles(rel_bias)
    fg = final_g.reshape(1, D_MODEL)
    for l in range(depth):
        wf, wb, wt = _split_w_in(w_in[l])
        (pu, pz, az, mz, gates, aq, ak, iq, mq, ik2, avt, iwt) = _project(
            x, norm_g[l].reshape(1, D_MODEL), wf, wb, wt)
        mk, mv = _mem_kv(mem, mem_norm_g[l].reshape(1, D_MODEL), w_mem_kv[l].astype(BF16))
        ya = _dsa(aq, iq, iwt, az, ik2, ak, avt, bias)
        x = _merge(l == depth - 1, x, pu, pz, mq, mz, ya, gates, mk, mv,
                   pool_w[l].astype(BF16), pool_scale[l].reshape(1, POOL_WIDTH),
                   w_branch[l].astype(BF16), w_out[l].astype(BF16), fg)
    return x
```

```python
import functools
import math

import numpy as np
import jax
import jax.numpy as jnp
from jax import lax
from jax.experimental import pallas as pl
from jax.experimental.pallas import tpu as pltpu

F32 = jnp.float32
BF16 = jnp.bfloat16
I32 = jnp.int32

D_MODEL = 1024
CHUNK = 64
MEM_LEN = 256
POOL_GROUPS = 4
POOL_WINDOWS = (2, 4, 8, 16)
POOL_WIDTH = D_MODEL // 2
POOL_GC = POOL_WIDTH // POOL_GROUPS
N_HEADS = 8
HEAD_DIM = 64
ATTN_WIDTH = N_HEADS * HEAD_DIM
IDX_HEADS = 8
IDX_DIM = 64
TOPK = 256
ATTN_SCALE = HEAD_DIM ** -0.5
LOG2E = math.log2(math.e)
IDX_W_SCALE = (IDX_DIM ** -0.5) * (IDX_HEADS ** -0.5)
MEM_HEADS = 4
MEM_HEAD_DIM = 128
MEM_WIDTH = MEM_HEADS * MEM_HEAD_DIM
MEM_SCALE = MEM_HEAD_DIM ** -0.5
N_BRANCH = 3
REL_BUCKETS = 32
REL_MAX_DIST = 1024
EPS = 1e-6
NEG = -1e30
HALF_NEG = -5e29

IN_SPLITS = (POOL_WIDTH, POOL_WIDTH, ATTN_WIDTH, ATTN_WIDTH, ATTN_WIDTH, ATTN_WIDTH,
             IDX_HEADS * IDX_DIM, IDX_DIM, IDX_HEADS, MEM_WIDTH, MEM_WIDTH, N_BRANCH * D_MODEL)
_OFF = np.concatenate([[0], np.cumsum(IN_SPLITS)]).tolist()
(O_PU, O_PZ, O_AQ, O_AK, O_AV, O_AZ, O_IQ, O_IK, O_IW, O_MQ, O_MZ, O_G, O_END) = _OFF

LANES = 128
SUBLANES = 8
TQ = 256
KT = 256
TM = 256
HALO = 16
N_NEAR = 4
COUNT_ROWS = 2 * KT
ACC_ROWS = 4 * SUBLANES
TOP_ROWS = 8 * SUBLANES
BANK_ROWS = 16
GROUP_ROWS = 128
CROWS = COUNT_ROWS // GROUP_ROWS * SUBLANES
WT_ROWS = 528
V_ROWS = HEAD_DIM + 16
VMEM_LIMIT = 56 * 1024 * 1024

_NT_DIMS = (((1,), (1,)), ((), ()))


def _rms_norm_f32(x, g):
    ms = jnp.mean(x * x, axis=-1, keepdims=True)
    return x * lax.rsqrt(ms + EPS) * g


def _proj_kernel(x_ref, g_ref, wf_ref, wb_ref, wt_ref,
                 pu_ref, pz_ref, az_ref, mz_ref, gates_ref,
                 aq_ref, ak_ref, iq_ref, mq_ref, ik2_ref, avt_ref, iwt_ref):
    h = _rms_norm_f32(x_ref[0], g_ref[...]).astype(BF16)

    def mm(w_ref, lo, hi):
        return jnp.dot(h, w_ref[:, lo:hi], preferred_element_type=F32)

    pu_ref[0] = mm(wf_ref, 0, 512)
    pz_ref[0] = mm(wf_ref, 512, 1024)
    az_ref[0] = mm(wf_ref, 1024, 1536)
    mz_ref[0] = mm(wf_ref, 1536, 2048)
    for c in range(N_BRANCH * D_MODEL // 512):
        gates_ref[0, :, 512 * c:512 * (c + 1)] = mm(wf_ref, 2048 + 512 * c, 2048 + 512 * (c + 1))
    aq_ref[0] = mm(wb_ref, 0, 512).astype(BF16)
    ak_ref[0] = mm(wb_ref, 512, 1024).astype(BF16)
    iq_ref[0] = mm(wb_ref, 1024, 1536).astype(BF16)
    mq_ref[0] = mm(wb_ref, 1536, 2048).astype(BF16)
    ik2_ref[0] = mm(wb_ref, 2048, 2176).astype(BF16)
    t = lax.dot_general(wt_ref[...], h, _NT_DIMS, preferred_element_type=F32)
    pad_row = lax.broadcasted_iota(I32, (V_ROWS - HEAD_DIM, TM), 0)
    pad = jnp.where(pad_row == 0, 1.0, 0.0).astype(BF16)
    for hd in range(N_HEADS):
        avt_ref[0, 0, V_ROWS * hd:V_ROWS * hd + HEAD_DIM, :] = (
            t[HEAD_DIM * hd:HEAD_DIM * (hd + 1)].astype(BF16))
        avt_ref[0, 0, V_ROWS * hd + HEAD_DIM:V_ROWS * (hd + 1), :] = pad
    iwt_ref[0] = t[ATTN_WIDTH:ATTN_WIDTH + IDX_HEADS] * IDX_W_SCALE


def _project(x, g, wf, wb, wt):
    b, s, d = x.shape
    nt = s // TM
    row = lambda w: pl.BlockSpec((1, TM, w), lambda bi, i: (bi, i, 0))
    full = lambda a: pl.BlockSpec(a.shape, lambda bi, i: (0,) * a.ndim)
    f32o = lambda w: jax.ShapeDtypeStruct((b, s, w), F32)
    b16o = lambda w: jax.ShapeDtypeStruct((b, s, w), BF16)
    out_shape = (f32o(512), f32o(512), f32o(512), f32o(512), f32o(N_BRANCH * D_MODEL),
                 b16o(512), b16o(512), b16o(512), b16o(512), b16o(128),
                 jax.ShapeDtypeStruct((b, nt, N_HEADS * V_ROWS, TM), BF16),
                 jax.ShapeDtypeStruct((b, IDX_HEADS, s), F32))
    out_specs = (row(512), row(512), row(512), row(512), row(N_BRANCH * D_MODEL),
                 row(512), row(512), row(512), row(512), row(128),
                 pl.BlockSpec((1, 1, N_HEADS * V_ROWS, TM), lambda bi, i: (bi, i, 0, 0)),
                 pl.BlockSpec((1, IDX_HEADS, TM), lambda bi, i: (bi, 0, i)))
    return pl.pallas_call(
        _proj_kernel,
        grid=(b, nt),
        in_specs=[row(d), full(g), full(wf), full(wb), full(wt)],
        out_specs=out_specs,
        out_shape=out_shape,
        compiler_params=pltpu.CompilerParams(
            dimension_semantics=("arbitrary", "arbitrary"), vmem_limit_bytes=VMEM_LIMIT),
        name="proj",
    )(x, g, wf, wb, wt)


def _memkv_kernel(mem_ref, g_ref, w_ref, mk_ref, mv_ref):
    h = _rms_norm_f32(mem_ref[0], g_ref[...]).astype(BF16)
    mk_ref[0] = jnp.dot(h, w_ref[:, 0:MEM_WIDTH], preferred_element_type=F32).astype(BF16)
    mv_ref[0] = jnp.dot(h, w_ref[:, MEM_WIDTH:2 * MEM_WIDTH], preferred_element_type=F32).astype(BF16)


def _mem_kv(mem, g, w):
    b, m, d = mem.shape
    o = jax.ShapeDtypeStruct((b, m, MEM_WIDTH), BF16)
    return pl.pallas_call(
        _memkv_kernel,
        grid=(b,),
        in_specs=[pl.BlockSpec((1, m, d), lambda bi: (bi, 0, 0)),
                  pl.BlockSpec(g.shape, lambda bi: (0, 0)),
                  pl.BlockSpec(w.shape, lambda bi: (0, 0))],
        out_specs=(pl.BlockSpec((1, m, MEM_WIDTH), lambda bi: (bi, 0, 0)),) * 2,
        out_shape=(o, o),
        compiler_params=pltpu.CompilerParams(
            dimension_semantics=("arbitrary",), vmem_limit_bytes=VMEM_LIMIT),
        name="memkv",
    )(mem, g, w)


def _bucket_table():
    half = REL_BUCKETS // 2
    max_exact = half // 2
    delta = np.arange(N_NEAR + 1, dtype=np.int64)[:, None, None]
    key = np.arange(KT, dtype=np.int64)[None, :, None]
    qry = np.arange(TQ, dtype=np.int64)[None, None, :]
    rel = key - qry - delta * KT
    ret = np.where(rel > 0, half, 0)
    n = np.abs(rel)
    nf = np.maximum(n, 1).astype(np.float32)
    large = max_exact + (np.log(nf / np.float32(max_exact))
                         / np.float32(math.log(REL_MAX_DIST / max_exact))
                         * np.float32(half - max_exact)).astype(np.int32)
    large = np.minimum(large, half - 1)
    return (ret + np.where(n < max_exact, n, large)).astype(np.int32)


FAR_BUCKET = REL_BUCKETS // 2 - 1


def _bias_kernel(rb_ref, bk_ref, o_ref):
    h = pl.program_id(1)
    bk = bk_ref[0]
    far = rb_ref[FAR_BUCKET, h]
    acc = jnp.zeros((KT, TQ), F32)
    for bkt in range(REL_BUCKETS):
        acc = jnp.where(bk == bkt, (rb_ref[bkt, h] - far) * LOG2E, acc)
    o_ref[0, 0] = acc


def _bias_tiles(rel_bias):
    table = jnp.asarray(_bucket_table())
    return pl.pallas_call(
        _bias_kernel,
        grid=(N_NEAR + 1, N_HEADS),
        in_specs=[pl.BlockSpec(memory_space=pltpu.SMEM),
                  pl.BlockSpec((1, KT, TQ), lambda dl, h: (dl, 0, 0))],
        out_specs=pl.BlockSpec((1, 1, KT, TQ), lambda dl, h: (dl, h, 0, 0)),
        out_shape=jax.ShapeDtypeStruct((N_NEAR + 1, N_HEADS, KT, TQ), F32),
        compiler_params=pltpu.CompilerParams(dimension_semantics=("arbitrary", "arbitrary")),
        name="bias_tiles",
    )(rel_bias, table)


def _dsa_kernel(q_ref, iq_ref, iwt_ref, az_ref, ik2_ref, k_ref, vt_ref, bias_ref, o_ref,
                sc_ref, hi_ref, c1_ref, c2_ref, tri_ref, cge_buf, cgt_buf, qw_ref, iqw_ref, acc_ref, m_buf, thr_buf,
                stage0_ref, stage1_ref, tmax0_buf, tmax1_buf):
    i = pl.program_id(1)
    n_tiles = i + 1
    cge_ref, cgt_ref, thr_ref = cge_buf.at[0:1], cgt_buf.at[0:1], thr_buf.at[0:1]
    m_ref, tmax0_ref, tmax1_ref = (r.at[0:N_HEADS] for r in (m_buf, tmax0_buf, tmax1_buf))

    lane = lax.broadcasted_iota(I32, (TQ, LANES), 1)
    keep_lo = jnp.where(lane < HEAD_DIM, 1.0, 0.0).astype(BF16)
    keep_hi = jnp.where(lane >= HEAD_DIM, 1.0, 0.0).astype(BF16)
    for h in range(N_HEADS):
        j = h // 2
        keep = keep_hi if (h % 2) else keep_lo
        qw_ref[h] = q_ref[0, :, LANES * j:LANES * (j + 1)] * keep
        iqw_ref[h] = iq_ref[0, :, LANES * j:LANES * (j + 1)] * keep

    key_l = lax.broadcasted_iota(I32, (KT, TQ), 0)
    qry_l = lax.broadcasted_iota(I32, (KT, TQ), 1)
    adm_diag = key_l < ((qry_l // CHUNK) + 1) * CHUNK

    def score_tile(t):
        r0 = pl.multiple_of(t * KT, KT)
        ikt = ik2_ref[0, pl.ds(r0, KT), :]
        acc = jnp.zeros((KT, TQ), F32)
        for h in range(IDX_HEADS):
            y = lax.dot_general(ikt, iqw_ref[h], _NT_DIMS, preferred_element_type=F32)
            acc = acc + jnp.maximum(y, 0.0) * iwt_ref[0, h:h + 1, :]
        adm = jnp.logical_or(jnp.logical_and(adm_diag, t == i), t < i)
        val = jnp.where(adm, acc, NEG)
        sc_ref[pl.ds(r0, KT), :] = val
        top = lax.bitcast_convert_type(val, I32) & I32(-(1 << 16))
        hi_ref[pl.ds(r0, KT), :] = lax.bitcast_convert_type(top, F32).astype(BF16)

    n_steps = (n_tiles + 1) // 2

    def score_step(c, carry):
        score_tile(2 * c)
        score_tile(2 * c + 1)
        return carry

    def score_two_steps(c2, carry):
        score_step(2 * c2, carry)
        return score_step(2 * c2 + 1, carry)

    lax.fori_loop(0, n_steps // 2, score_two_steps, 0)
    lax.fori_loop(n_steps // 2 * 2, n_steps, score_step, 0)

    def count(pred):
        def body(c, acc):
            for k in range(COUNT_ROWS // ACC_ROWS):
                r0 = pl.multiple_of(c * COUNT_ROWS + k * ACC_ROWS, ACC_ROWS)
                x = sc_ref[pl.ds(r0, ACC_ROWS), :]
                acc = jnp.where(pred(x, r0), acc + 1.0, acc)
            return acc
        acc = lax.fori_loop(0, n_steps, body, jnp.zeros((ACC_ROWS, TQ), F32))
        return acc.sum(axis=0, keepdims=True)

    def to_float(u):
        bits = jnp.where(u < 0, u ^ I32(-2 ** 31), ~u)
        return lax.bitcast_convert_type(bits, F32)

    @pl.when(i == 0)
    def _():
        thr_ref[...] = jnp.full((1, TQ), HALF_NEG, F32)

    def count_top(cand_bf16):
        cb = jnp.broadcast_to(cand_bf16, (TOP_ROWS, TQ))

        def body(c, acc):
            for k in range(COUNT_ROWS // TOP_ROWS):
                r0 = pl.multiple_of(c * COUNT_ROWS + k * TOP_ROWS, TOP_ROWS)
                acc = jnp.where(hi_ref[pl.ds(r0, TOP_ROWS), :] >= cb, acc + 1.0, acc)
            return acc
        acc = lax.fori_loop(0, n_steps, body, jnp.zeros((TOP_ROWS, TQ), BF16))
        return acc.astype(F32).sum(axis=0, keepdims=True)

    @pl.when(i > 0)
    def _():
        def top_body(bi, carry):
            u, cnt_u = carry
            cand = u | lax.shift_left(I32(1 << 16), I32(15) - bi)
            top = lax.bitcast_convert_type(to_float(cand), I32) & I32(-(1 << 16))
            cnt = count_top(lax.bitcast_convert_type(top, F32).astype(BF16))
            keep = cnt >= float(TOPK)
            return jnp.where(keep, cand, u), jnp.where(keep, cnt, cnt_u)

        def bit_body(bi, carry):
            u, cnt_u = carry
            cand = u | lax.shift_left(I32(1), I32(15) - bi)
            cf = to_float(cand)
            cnt = count(lambda x, r0: x >= cf)
            keep = cnt >= float(TOPK)
            return jnp.where(keep, cand, u), jnp.where(keep, cnt, cnt_u)

        carry = (jnp.zeros((1, TQ), I32), jnp.zeros((1, TQ), F32))
        carry = lax.fori_loop(0, 16, top_body, carry)

        u_top, cnt_top = carry
        f_lo = to_float(u_top)
        f_hi = to_float(u_top + I32(1 << 16))
        top_hi = lax.bitcast_convert_type(f_hi, I32) & I32(-(1 << 16))
        above = count_top(lax.bitcast_convert_type(top_hi, F32).astype(BF16))
        f_hi_b = jnp.broadcast_to(f_hi, (SUBLANES, TQ))

        def compact_step(c, carry):
            for g in range(COUNT_ROWS // GROUP_ROWS):
                m1 = jnp.full((SUBLANES, TQ), NEG, F32)
                m2 = m1
                for k in range(GROUP_ROWS // SUBLANES):
                    r0 = pl.multiple_of(c * COUNT_ROWS + g * GROUP_ROWS + k * SUBLANES, SUBLANES)
                    v = sc_ref[pl.ds(r0, SUBLANES), :]
                    v = jnp.where(v >= f_hi_b, NEG, v)
                    m2 = jnp.maximum(m2, jnp.minimum(m1, v))
                    m1 = jnp.maximum(m1, v)
                o0 = pl.multiple_of((c * (COUNT_ROWS // GROUP_ROWS) + g) * SUBLANES, SUBLANES)
                c1_ref[pl.ds(o0, SUBLANES), :] = m1
                c2_ref[pl.ds(o0, SUBLANES), :] = m2
            return carry

        lax.fori_loop(0, n_steps, compact_step, 0)

        def count_compact(cf, cmp=jnp.greater_equal):
            def body(c, acc):
                r0 = pl.multiple_of(c * CROWS, CROWS)
                acc = jnp.where(cmp(c1_ref[pl.ds(r0, CROWS), :], cf), acc + 1.0, acc)
                return jnp.where(cmp(c2_ref[pl.ds(r0, CROWS), :], cf), acc + 1.0, acc)
            acc = lax.fori_loop(0, n_steps, body, jnp.zeros((CROWS, TQ), F32))
            return acc.sum(axis=0, keepdims=True)

        captured = count_compact(f_lo)
        missed = jnp.max(jnp.abs(captured - (cnt_top - above)))

        @pl.when(missed == 0.0)
        def _():
            def cbit_body(bi, carry):
                u, cnt_u = carry
                cand = u | lax.shift_left(I32(1), I32(15) - bi)
                cnt = above + count_compact(to_float(cand))
                keep = cnt >= float(TOPK)
                return jnp.where(keep, cand, u), jnp.where(keep, cnt, cnt_u)

            u, cnt = lax.fori_loop(0, 16, cbit_body, carry)
            thr_ref[...] = to_float(u)
            cge_ref[...] = cnt
            cgt_ref[...] = above + count_compact(to_float(u), jnp.greater)

        @pl.when(missed != 0.0)
        def _():
            u, cnt = lax.fori_loop(0, 16, bit_body, carry)
            thr = to_float(u)
            thr_ref[...] = thr
            cge_ref[...] = cnt
            cgt_ref[...] = count(lambda x, r0: x > thr)

        thr = thr_ref[...]

        @pl.when(jnp.max(cge_ref[...]) > float(TOPK))
        def _():
            need = float(TOPK) - cgt_ref[...]
            tri_r = lax.broadcasted_iota(I32, (KT, KT), 0)
            tri_c = lax.broadcasted_iota(I32, (KT, KT), 1)
            tri_ref[...] = jnp.where(tri_r >= tri_c, 1.0, 0.0).astype(BF16)

            def drop_step(c, seen):
                rows = [pl.multiple_of((2 * c + k) * KT, KT) for k in range(2)]
                xs = [sc_ref[pl.ds(r0, KT), :] for r0 in rows]
                es = [jnp.where(x == thr, 1.0, 0.0) for x in xs]
                uptos = [jnp.dot(tri_ref[...], e.astype(BF16), preferred_element_type=F32)
                         for e in es]
                for r0, x, e, upto in zip(rows, xs, es, uptos):
                    gone = jnp.logical_and(e > 0.0, upto - e + seen >= need)
                    sc_ref[pl.ds(r0, KT), :] = jnp.where(gone, NEG, x)
                    seen = seen + upto[KT - 1:KT, :]
                return seen

            lax.fori_loop(0, n_steps, drop_step, jnp.zeros((1, TQ), F32))

    m_ref[...] = jnp.full(m_ref.shape, NEG, F32)
    acc_ref[...] = jnp.zeros(acc_ref.shape, F32)
    thr = thr_ref[...]

    def to_mask(c, carry):
        r0 = pl.multiple_of(c * KT, KT)
        sc_ref[pl.ds(r0, KT), :] = jnp.where(sc_ref[pl.ds(r0, KT), :] >= thr, F32(0.0), F32(NEG))
        return carry

    lax.fori_loop(0, n_tiles, to_mask, 0)

    stages = ((stage0_ref, tmax0_ref), (stage1_ref, tmax1_ref))

    def head_a(t, slot, h):
        stage_ref, tmax_ref = stages[slot]
        r0 = pl.multiple_of(t * KT, KT)
        dl = jnp.minimum(i - t, N_NEAR)
        j = h // 2
        kk = k_ref[0, pl.ds(r0, KT), LANES * j:LANES * (j + 1)]
        s = lax.dot_general(kk, qw_ref[h], _NT_DIMS, preferred_element_type=F32)
        s = s + sc_ref[pl.ds(r0, KT), :] + bias_ref[dl, h]
        stage_ref[h] = s
        tmax_ref[h:h + 1, :] = s.reshape(KT // ACC_ROWS, ACC_ROWS, TQ).max(axis=0).max(
            axis=0, keepdims=True)

    def begin_b(slot):
        m_old = m_ref[...]
        m_new = jnp.maximum(m_old, stages[slot][1][...])
        m_ref[...] = m_new
        return m_new, jnp.exp2(m_old - m_new)

    def head_b(t, slot, h, m_new, alpha):
        p = jnp.exp2(stages[slot][0][h] - m_new[h:h + 1, :])
        pv = jnp.dot(vt_ref[0, t, V_ROWS * h:V_ROWS * (h + 1), :], p.astype(BF16),
                     preferred_element_type=F32)
        acc_ref[V_ROWS * h:V_ROWS * (h + 1), :] = (
            alpha[h:h + 1, :] * acc_ref[V_ROWS * h:V_ROWS * (h + 1), :] + pv)

    def pass_a(t, slot):
        for h in range(N_HEADS):
            head_a(t, slot, h)

    def pass_b(t, slot):
        m_new, alpha = begin_b(slot)
        for h in range(N_HEADS):
            head_b(t, slot, h, m_new, alpha)

    pass_a(0, 0)
    n_pairs = (n_tiles - 1) // 2

    def attn_pair(pr, carry):
        t = 2 * pr
        pass_a(t + 1, 1)
        pass_b(t, 0)
        pass_a(t + 2, 0)
        pass_b(t + 1, 1)
        return carry

    lax.fori_loop(0, n_pairs, attn_pair, 0)
    t_last = 2 * n_pairs

    @pl.when(n_tiles - 1 > t_last)
    def _():
        pass_a(t_last + 1, 1)
        pass_b(t_last, 0)
        pass_b(t_last + 1, 1)

    @pl.when(n_tiles - 1 == t_last)
    def _():
        pass_b(t_last, 0)

    heads = []
    for h in range(N_HEADS):
        denom = acc_ref[V_ROWS * h + HEAD_DIM:V_ROWS * h + HEAD_DIM + 1, :]
        heads.append(acc_ref[V_ROWS * h:V_ROWS * h + HEAD_DIM, :] * (1.0 / denom))
    y = jnp.concatenate(heads, axis=0).T
    o_ref[0] = (y * jax.nn.silu(az_ref[0])).astype(BF16)


def _dsa(aq, iq, iwt, az, ik2, ak, avt, bias):
    b, s, _ = aq.shape
    nq = s // TQ
    qrow = lambda w: pl.BlockSpec((1, TQ, w), lambda bi, i: (bi, i, 0))
    once = pl.Buffered(1)
    small = pltpu.VMEM((BANK_ROWS, TQ), F32)
    return pl.pallas_call(
        _dsa_kernel,
        grid=(b, nq),
        in_specs=[qrow(ATTN_WIDTH), qrow(IDX_HEADS * IDX_DIM),
                  pl.BlockSpec((1, IDX_HEADS, TQ), lambda bi, i: (bi, 0, i)),
                  qrow(ATTN_WIDTH),
                  pl.BlockSpec((1, s, LANES), lambda bi, i: (bi, 0, 0), pipeline_mode=once),
                  pl.BlockSpec((1, s, ATTN_WIDTH), lambda bi, i: (bi, 0, 0), pipeline_mode=once),
                  pl.BlockSpec((1, s // KT, N_HEADS * V_ROWS, KT), lambda bi, i: (bi, 0, 0, 0),
                               pipeline_mode=once),
                  pl.BlockSpec(bias.shape, lambda bi, i: (0, 0, 0, 0), pipeline_mode=once)],
        out_specs=qrow(ATTN_WIDTH),
        out_shape=jax.ShapeDtypeStruct((b, s, ATTN_WIDTH), BF16),
        scratch_shapes=[pltpu.VMEM((s, TQ), F32),
                        pltpu.VMEM((s, TQ), BF16),
                        pltpu.VMEM((s // GROUP_ROWS * SUBLANES, TQ), F32),
                        pltpu.VMEM((s // GROUP_ROWS * SUBLANES, TQ), F32),
                        pltpu.VMEM((KT, KT), BF16),
                        small,
                        small,
                        pltpu.VMEM((N_HEADS, TQ, LANES), BF16),
                        pltpu.VMEM((IDX_HEADS, TQ, LANES), BF16),
                        pltpu.VMEM((N_HEADS * V_ROWS, TQ), F32),
                        small,
                        small,
                        pltpu.VMEM((N_HEADS, KT, TQ), F32),
                        pltpu.VMEM((N_HEADS, KT, TQ), F32),
                        small,
                        small],
        compiler_params=pltpu.CompilerParams(
            dimension_semantics=("arbitrary", "arbitrary"), vmem_limit_bytes=VMEM_LIMIT),
        name="dsa",
    )(aq, iq, iwt, az, ik2, ak, avt, bias)


def _merge_kernel(final, x_ref, pu_ref, halo_ref, pz_ref, mq_ref, mz_ref, ya_ref, gates_ref,
                  mk_ref, mv_ref, pw_ref, ps_ref, wb_ref, wo_ref, fg_ref, o_ref):
    i = pl.program_id(1)

    u = pu_ref[0]
    halo = jnp.where(i == 0, 0.0, halo_ref[0])
    ext = jnp.concatenate([halo, u], axis=0)
    t_glob = i * TM + lax.broadcasted_iota(I32, (TM, POOL_GC), 0)
    mixed = []
    for g in range(POOL_GROUPS):
        sl = slice(POOL_GC * g, POOL_GC * (g + 1))
        wsum = ext[:, sl]
        for step in range(g + 1):
            wsum = wsum + pltpu.roll(wsum, 2 ** step, axis=0)
        cnt = jnp.minimum(t_glob + 1, POOL_WINDOWS[g]).astype(F32)
        pooled = wsum[HALO:, :] / cnt - u[:, sl]
        mixed.append(jnp.dot(pooled.astype(BF16), pw_ref[g], preferred_element_type=F32))
    y_pool = jnp.concatenate(mixed, axis=1) * ps_ref[...] * jax.nn.silu(pz_ref[0])

    mem_out = []
    for h in range(MEM_HEADS):
        sl = slice(MEM_HEAD_DIM * h, MEM_HEAD_DIM * (h + 1))
        logits = lax.dot_general(mq_ref[0, :, sl], mk_ref[0, :, sl], _NT_DIMS,
                                 preferred_element_type=F32) * MEM_SCALE
        e = jnp.exp(logits - logits.max(axis=-1, keepdims=True))
        p = e / e.sum(axis=-1, keepdims=True)
        mem_out.append(jnp.dot(p.astype(BF16), mv_ref[0, :, sl], preferred_element_type=F32))
    y_mem = jnp.concatenate(mem_out, axis=1) * jax.nn.silu(mz_ref[0])

    branches = (y_pool.astype(BF16), ya_ref[0], y_mem.astype(BF16))
    merged = jnp.zeros((TM, D_MODEL), F32)
    for br in range(N_BRANCH):
        gate = jax.nn.sigmoid(gates_ref[0, :, D_MODEL * br:D_MODEL * (br + 1)])
        merged = merged + gate * jnp.dot(branches[br], wb_ref[br], preferred_element_type=F32)
    out = x_ref[0] + jnp.dot(merged.astype(BF16), wo_ref[...], preferred_element_type=F32)
    if final:
        out = _rms_norm_f32(out, fg_ref[...])
    o_ref[0] = out


def _merge(final, x, pu, pz, mq, mz, ya, gates, mk, mv, pw, ps, wb, wo, fg):
    b, s, d = x.shape
    row = lambda w: pl.BlockSpec((1, TM, w), lambda bi, i: (bi, i, 0))
    full = lambda a: pl.BlockSpec(a.shape, lambda bi, i: (0,) * a.ndim)
    per_b = lambda a: pl.BlockSpec((1,) + a.shape[1:], lambda bi, i: (bi,) + (0,) * (a.ndim - 1))
    halo = pl.BlockSpec((1, HALO, POOL_WIDTH),
                        lambda bi, i: (bi, jnp.maximum(i * (TM // HALO) - 1, 0), 0))
    return pl.pallas_call(
        functools.partial(_merge_kernel, final),
        grid=(b, s // TM),
        in_specs=[row(d), row(POOL_WIDTH), halo, row(POOL_WIDTH), row(MEM_WIDTH), row(MEM_WIDTH),
                  row(ATTN_WIDTH), row(N_BRANCH * D_MODEL), per_b(mk), per_b(mv),
                  full(pw), full(ps), full(wb), full(wo), full(fg)],
        out_specs=row(d),
        out_shape=jax.ShapeDtypeStruct((b, s, d), F32),
        compiler_params=pltpu.CompilerParams(
            dimension_semantics=("arbitrary", "arbitrary"), vmem_limit_bytes=VMEM_LIMIT),
        name="merge_final" if final else "merge",
    )(x, pu, pu, pz, mq, mz, ya, gates, mk, mv, pw, ps, wb, wo, fg)


def _split_w_in(w):
    col = lambda lo, hi: w[:, lo:hi]
    wf = jnp.concatenate([col(O_PU, O_PZ), col(O_PZ, O_AQ), col(O_AZ, O_IQ), col(O_MZ, O_G),
                          col(O_G, O_END)], axis=1)
    wb = jnp.concatenate([col(O_AQ, O_AK) * (ATTN_SCALE * LOG2E), col(O_AK, O_AV), col(O_IQ, O_IK),
                          col(O_MQ, O_MZ), col(O_IK, O_IW), col(O_IK, O_IW)], axis=1)
    wt = jnp.concatenate([col(O_AV, O_AZ), col(O_IW, O_MQ),
                          jnp.zeros((w.shape[0], WT_ROWS - ATTN_WIDTH - IDX_HEADS), w.dtype)], axis=1).T
    return wf.astype(BF16), wb.astype(BF16), wt.astype(BF16)


def kernel(x, mem, norm_g, w_in, pool_w, pool_scale, mem_norm_g, w_mem_kv, w_branch, w_out,
           rel_bias, final_g):
    depth = w_in.shape[0]
    assert x.shape[1] % COUNT_ROWS == 0 and x.shape[2] == D_MODEL and mem.shape[1] == MEM_LEN
    assert w_in.shape[2] == O_END and TQ == KT == TM
    assert x.shape[1] // TOP_ROWS <= 256
    bias = _bias_tiles(rel_bias)
    fg = final_g.reshape(1, D_MODEL)
    for l in range(depth):
        wf, wb, wt = _split_w_in(w_in[l])
        (pu, pz, az, mz, gates, aq, ak, iq, mq, ik2, avt, iwt) = _project(
            x, norm_g[l].reshape(1, D_MODEL), wf, wb, wt)
        mk, mv = _mem_kv(mem, mem_norm_g[l].reshape(1, D_MODEL), w_mem_kv[l].astype(BF16))
        ya = _dsa(aq, iq, iwt, az, ik2, ak, avt, bias)
        x = _merge(l == depth - 1, x, pu, pz, mq, mz, ya, gates, mk, mv,
                   pool_w[l].astype(BF16), pool_scale[l].reshape(1, POOL_WIDTH),
                   w_branch[l].astype(BF16), w_out[l].astype(BF16), fg)
    return x
```

```python
import functools
import math

import numpy as np
import jax
import jax.numpy as jnp
from jax import lax
from jax.experimental import pallas as pl
from jax.experimental.pallas import tpu as pltpu

F32 = jnp.float32
BF16 = jnp.bfloat16
I32 = jnp.int32

D_MODEL = 1024
CHUNK = 64
MEM_LEN = 256
POOL_GROUPS = 4
POOL_WINDOWS = (2, 4, 8, 16)
POOL_WIDTH = D_MODEL // 2
POOL_GC = POOL_WIDTH // POOL_GROUPS
N_HEADS = 8
HEAD_DIM = 64
ATTN_WIDTH = N_HEADS * HEAD_DIM
IDX_HEADS = 8
IDX_DIM = 64
TOPK = 256
ATTN_SCALE = HEAD_DIM ** -0.5
LOG2E = math.log2(math.e)
IDX_W_SCALE = (IDX_DIM ** -0.5) * (IDX_HEADS ** -0.5)
MEM_HEADS = 4
MEM_HEAD_DIM = 128
MEM_WIDTH = MEM_HEADS * MEM_HEAD_DIM
MEM_SCALE = MEM_HEAD_DIM ** -0.5
N_BRANCH = 3
REL_BUCKETS = 32
REL_MAX_DIST = 1024
EPS = 1e-6
NEG = -1e30
HALF_NEG = -5e29

IN_SPLITS = (POOL_WIDTH, POOL_WIDTH, ATTN_WIDTH, ATTN_WIDTH, ATTN_WIDTH, ATTN_WIDTH,
             IDX_HEADS * IDX_DIM, IDX_DIM, IDX_HEADS, MEM_WIDTH, MEM_WIDTH, N_BRANCH * D_MODEL)
_OFF = np.concatenate([[0], np.cumsum(IN_SPLITS)]).tolist()
(O_PU, O_PZ, O_AQ, O_AK, O_AV, O_AZ, O_IQ, O_IK, O_IW, O_MQ, O_MZ, O_G, O_END) = _OFF

LANES = 128
SUBLANES = 8
TQ = 256
KT = 256
TM = 512
PTM = 512
HALO = 16
N_NEAR = 4
COUNT_ROWS = 2 * KT
ACC_ROWS = 4 * SUBLANES
TOP_ROWS = 8 * SUBLANES
BANK_ROWS = 16
GROUP_ROWS = 128
CROWS = COUNT_ROWS // GROUP_ROWS * SUBLANES
WT_ROWS = 528
V_ROWS = HEAD_DIM + 16
VMEM_LIMIT = 56 * 1024 * 1024

_NT_DIMS = (((1,), (1,)), ((), ()))


def _rms_norm_f32(x, g):
    ms = jnp.mean(x * x, axis=-1, keepdims=True)
    return x * lax.rsqrt(ms + EPS) * g


def _proj_kernel(x_ref, g_ref, wf_ref, wb_ref, wt_ref,
                 pu_ref, pz_ref, az_ref, mz_ref, gates_ref,
                 aq_ref, ak_ref, iq_ref, mq_ref, ik2_ref, avt_ref, iwt_ref):
    h = _rms_norm_f32(x_ref[0], g_ref[...]).astype(BF16)

    def mm(w_ref, lo, hi):
        return jnp.dot(h, w_ref[:, lo:hi], preferred_element_type=F32)

    pu_ref[0] = mm(wf_ref, 0, 512)
    pz_ref[0] = mm(wf_ref, 512, 1024)
    az_ref[0] = mm(wf_ref, 1024, 1536)
    mz_ref[0] = mm(wf_ref, 1536, 2048)
    for c in range(N_BRANCH * D_MODEL // 512):
        gates_ref[0, :, 512 * c:512 * (c + 1)] = mm(wf_ref, 2048 + 512 * c, 2048 + 512 * (c + 1))
    aq_ref[0] = mm(wb_ref, 0, 512).astype(BF16)
    ak_ref[0] = mm(wb_ref, 512, 1024).astype(BF16)
    iq_ref[0] = mm(wb_ref, 1024, 1536).astype(BF16)
    mq_ref[0] = mm(wb_ref, 1536, 2048).astype(BF16)
    ik2_ref[0] = mm(wb_ref, 2048, 2176).astype(BF16)
    t = lax.dot_general(wt_ref[...], h, _NT_DIMS, preferred_element_type=F32)
    pad_row = lax.broadcasted_iota(I32, (V_ROWS - HEAD_DIM, KT), 0)
    pad = jnp.where(pad_row == 0, 1.0, 0.0).astype(BF16)
    for kt in range(PTM // KT):
        for hd in range(N_HEADS):
            avt_ref[0, kt, V_ROWS * hd:V_ROWS * hd + HEAD_DIM, :] = (
                t[HEAD_DIM * hd:HEAD_DIM * (hd + 1), KT * kt:KT * (kt + 1)].astype(BF16))
            avt_ref[0, kt, V_ROWS * hd + HEAD_DIM:V_ROWS * (hd + 1), :] = pad
    iwt_ref[0] = t[ATTN_WIDTH:ATTN_WIDTH + IDX_HEADS] * IDX_W_SCALE


def _project(x, g, wf, wb, wt):
    b, s, d = x.shape
    row = lambda w: pl.BlockSpec((1, PTM, w), lambda bi, i: (bi, i, 0))
    full = lambda a: pl.BlockSpec(a.shape, lambda bi, i: (0,) * a.ndim,
                                  pipeline_mode=pl.Buffered(1))
    f32o = lambda w: jax.ShapeDtypeStruct((b, s, w), F32)
    b16o = lambda w: jax.ShapeDtypeStruct((b, s, w), BF16)
    out_shape = (f32o(512), f32o(512), f32o(512), f32o(512), f32o(N_BRANCH * D_MODEL),
                 b16o(512), b16o(512), b16o(512), b16o(512), b16o(128),
                 jax.ShapeDtypeStruct((b, s // KT, N_HEADS * V_ROWS, KT), BF16),
                 jax.ShapeDtypeStruct((b, IDX_HEADS, s), F32))
    out_specs = (row(512), row(512), row(512), row(512), row(N_BRANCH * D_MODEL),
                 row(512), row(512), row(512), row(512), row(128),
                 pl.BlockSpec((1, PTM // KT, N_HEADS * V_ROWS, KT), lambda bi, i: (bi, i, 0, 0)),
                 pl.BlockSpec((1, IDX_HEADS, PTM), lambda bi, i: (bi, 0, i)))
    return pl.pallas_call(
        _proj_kernel,
        grid=(b, s // PTM),
        in_specs=[row(d), full(g), full(wf), full(wb), full(wt)],
        out_specs=out_specs,
        out_shape=out_shape,
        compiler_params=pltpu.CompilerParams(
            dimension_semantics=("arbitrary", "arbitrary"), vmem_limit_bytes=VMEM_LIMIT),
        name="proj",
    )(x, g, wf, wb, wt)


def _memkv_kernel(mem_ref, g_ref, w_ref, mk_ref, mv_ref):
    h = _rms_norm_f32(mem_ref[0], g_ref[...]).astype(BF16)
    mk_ref[0] = jnp.dot(h, w_ref[:, 0:MEM_WIDTH], preferred_element_type=F32).astype(BF16)
    mv_ref[0] = jnp.dot(h, w_ref[:, MEM_WIDTH:2 * MEM_WIDTH], preferred_element_type=F32).astype(BF16)


def _mem_kv(mem, g, w):
    b, m, d = mem.shape
    o = jax.ShapeDtypeStruct((b, m, MEM_WIDTH), BF16)
    return pl.pallas_call(
        _memkv_kernel,
        grid=(b,),
        in_specs=[pl.BlockSpec((1, m, d), lambda bi: (bi, 0, 0)),
                  pl.BlockSpec(g.shape, lambda bi: (0, 0)),
                  pl.BlockSpec(w.shape, lambda bi: (0, 0))],
        out_specs=(pl.BlockSpec((1, m, MEM_WIDTH), lambda bi: (bi, 0, 0)),) * 2,
        out_shape=(o, o),
        compiler_params=pltpu.CompilerParams(
            dimension_semantics=("arbitrary",), vmem_limit_bytes=VMEM_LIMIT),
        name="memkv",
    )(mem, g, w)


def _bucket_table():
    half = REL_BUCKETS // 2
    max_exact = half // 2
    delta = np.arange(N_NEAR + 1, dtype=np.int64)[:, None, None]
    key = np.arange(KT, dtype=np.int64)[None, :, None]
    qry = np.arange(TQ, dtype=np.int64)[None, None, :]
    rel = key - qry - delta * KT
    ret = np.where(rel > 0, half, 0)
    n = np.abs(rel)
    nf = np.maximum(n, 1).astype(np.float32)
    large = max_exact + (np.log(nf / np.float32(max_exact))
                         / np.float32(math.log(REL_MAX_DIST / max_exact))
                         * np.float32(half - max_exact)).astype(np.int32)
    large = np.minimum(large, half - 1)
    return (ret + np.where(n < max_exact, n, large)).astype(np.int32)


FAR_BUCKET = REL_BUCKETS // 2 - 1


def _bias_kernel(rb_ref, bk_ref, o_ref):
    h = pl.program_id(1)
    bk = bk_ref[0]
    far = rb_ref[FAR_BUCKET, h]
    acc = jnp.zeros((KT, TQ), F32)
    for bkt in range(REL_BUCKETS):
        acc = jnp.where(bk == bkt, (rb_ref[bkt, h] - far) * LOG2E, acc)
    o_ref[0, 0] = acc


def _bias_tiles(rel_bias):
    table = jnp.asarray(_bucket_table())
    return pl.pallas_call(
        _bias_kernel,
        grid=(N_NEAR + 1, N_HEADS),
        in_specs=[pl.BlockSpec(memory_space=pltpu.SMEM),
                  pl.BlockSpec((1, KT, TQ), lambda dl, h: (dl, 0, 0))],
        out_specs=pl.BlockSpec((1, 1, KT, TQ), lambda dl, h: (dl, h, 0, 0)),
        out_shape=jax.ShapeDtypeStruct((N_NEAR + 1, N_HEADS, KT, TQ), F32),
        compiler_params=pltpu.CompilerParams(dimension_semantics=("arbitrary", "arbitrary")),
        name="bias_tiles",
    )(rel_bias, table)


def _dsa_kernel(q_ref, iq_ref, iwt_ref, az_ref, ik2_ref, k_ref, vt_ref, bias_ref, o_ref,
                sc_ref, hi_ref, c1_ref, c2_ref, tri_ref, cge_buf, cgt_buf, qw_ref, iqw_ref, acc_ref, m_buf, thr_buf,
                stage0_ref, stage1_ref, tmax0_buf, tmax1_buf):
    i = pl.program_id(1)
    n_tiles = i + 1
    cge_ref, cgt_ref, thr_ref = cge_buf.at[0:1], cgt_buf.at[0:1], thr_buf.at[0:1]
    m_ref, tmax0_ref, tmax1_ref = (r.at[0:N_HEADS] for r in (m_buf, tmax0_buf, tmax1_buf))

    lane = lax.broadcasted_iota(I32, (TQ, LANES), 1)
    keep_lo = jnp.where(lane < HEAD_DIM, 1.0, 0.0).astype(BF16)
    keep_hi = jnp.where(lane >= HEAD_DIM, 1.0, 0.0).astype(BF16)
    for h in range(N_HEADS):
        j = h // 2
        keep = keep_hi if (h % 2) else keep_lo
        qw_ref[h] = q_ref[0, :, LANES * j:LANES * (j + 1)] * keep
        iqw_ref[h] = iq_ref[0, :, LANES * j:LANES * (j + 1)] * keep

    key_l = lax.broadcasted_iota(I32, (KT, TQ), 0)
    qry_l = lax.broadcasted_iota(I32, (KT, TQ), 1)
    adm_diag = key_l < ((qry_l // CHUNK) + 1) * CHUNK

    def score_tile(t):
        r0 = pl.multiple_of(t * KT, KT)
        ikt = ik2_ref[0, pl.ds(r0, KT), :]
        acc = jnp.zeros((KT, TQ), F32)
        for h in range(IDX_HEADS):
            y = lax.dot_general(ikt, iqw_ref[h], _NT_DIMS, preferred_element_type=F32)
            acc = acc + jnp.maximum(y, 0.0) * iwt_ref[0, h:h + 1, :]
        adm = jnp.logical_or(jnp.logical_and(adm_diag, t == i), t < i)
        val = jnp.where(adm, acc, NEG)
        sc_ref[pl.ds(r0, KT), :] = val
        top = lax.bitcast_convert_type(val, I32) & I32(-(1 << 16))
        hi_ref[pl.ds(r0, KT), :] = lax.bitcast_convert_type(top, F32).astype(BF16)

    n_steps = (n_tiles + 1) // 2

    def score_step(c, carry):
        score_tile(2 * c)
        score_tile(2 * c + 1)
        return carry

    def score_two_steps(c2, carry):
        score_step(2 * c2, carry)
        return score_step(2 * c2 + 1, carry)

    lax.fori_loop(0, n_steps // 2, score_two_steps, 0)
    lax.fori_loop(n_steps // 2 * 2, n_steps, score_step, 0)

    def count(pred):
        def body(c, acc):
            for k in range(COUNT_ROWS // ACC_ROWS):
                r0 = pl.multiple_of(c * COUNT_ROWS + k * ACC_ROWS, ACC_ROWS)
                x = sc_ref[pl.ds(r0, ACC_ROWS), :]
                acc = jnp.where(pred(x, r0), acc + 1.0, acc)
            return acc
        acc = lax.fori_loop(0, n_steps, body, jnp.zeros((ACC_ROWS, TQ), F32))
        return acc.sum(axis=0, keepdims=True)

    def to_float(u):
        bits = jnp.where(u < 0, u ^ I32(-2 ** 31), ~u)
        return lax.bitcast_convert_type(bits, F32)

    @pl.when(i == 0)
    def _():
        thr_ref[...] = jnp.full((1, TQ), HALF_NEG, F32)

    def count_top(cand_bf16):
        cb = jnp.broadcast_to(cand_bf16, (TOP_ROWS, TQ))

        def body(c, acc):
            for k in range(COUNT_ROWS // TOP_ROWS):
                r0 = pl.multiple_of(c * COUNT_ROWS + k * TOP_ROWS, TOP_ROWS)
                acc = jnp.where(hi_ref[pl.ds(r0, TOP_ROWS), :] >= cb, acc + 1.0, acc)
            return acc
        acc = lax.fori_loop(0, n_steps, body, jnp.zeros((TOP_ROWS, TQ), BF16))
        return acc.astype(F32).sum(axis=0, keepdims=True)

    @pl.when(i > 0)
    def _():
        def top_body(bi, carry):
            u, cnt_u = carry
            cand = u | lax.shift_left(I32(1 << 16), I32(15) - bi)
            top = lax.bitcast_convert_type(to_float(cand), I32) & I32(-(1 << 16))
            cnt = count_top(lax.bitcast_convert_type(top, F32).astype(BF16))
            keep = cnt >= float(TOPK)
            return jnp.where(keep, cand, u), jnp.where(keep, cnt, cnt_u)

        def bit_body(bi, carry):
            u, cnt_u = carry
            cand = u | lax.shift_left(I32(1), I32(15) - bi)
            cf = to_float(cand)
            cnt = count(lambda x, r0: x >= cf)
            keep = cnt >= float(TOPK)
            return jnp.where(keep, cand, u), jnp.where(keep, cnt, cnt_u)

        carry = (jnp.zeros((1, TQ), I32), jnp.zeros((1, TQ), F32))
        carry = lax.fori_loop(0, 16, top_body, carry)

        u_top, cnt_top = carry
        f_lo = to_float(u_top)
        f_hi = to_float(u_top + I32(1 << 16))
        top_hi = lax.bitcast_convert_type(f_hi, I32) & I32(-(1 << 16))
        above = count_top(lax.bitcast_convert_type(top_hi, F32).astype(BF16))
        f_hi_b = jnp.broadcast_to(f_hi, (SUBLANES, TQ))

        def compact_step(c, carry):
            for g in range(COUNT_ROWS // GROUP_ROWS):
                m1 = jnp.full((SUBLANES, TQ), NEG, F32)
                m2 = m1
                for k in range(GROUP_ROWS // SUBLANES):
                    r0 = pl.multiple_of(c * COUNT_ROWS + g * GROUP_ROWS + k * SUBLANES, SUBLANES)
                    v = sc_ref[pl.ds(r0, SUBLANES), :]
                    v = jnp.where(v >= f_hi_b, NEG, v)
                    m2 = jnp.maximum(m2, jnp.minimum(m1, v))
                    m1 = jnp.maximum(m1, v)
                o0 = pl.multiple_of((c * (COUNT_ROWS // GROUP_ROWS) + g) * SUBLANES, SUBLANES)
                c1_ref[pl.ds(o0, SUBLANES), :] = m1
                c2_ref[pl.ds(o0, SUBLANES), :] = m2
            return carry

        lax.fori_loop(0, n_steps, compact_step, 0)

        def count_compact(cf, cmp=jnp.greater_equal):
            def body(c, acc):
                r0 = pl.multiple_of(c * CROWS, CROWS)
                acc = jnp.where(cmp(c1_ref[pl.ds(r0, CROWS), :], cf), acc + 1.0, acc)
                return jnp.where(cmp(c2_ref[pl.ds(r0, CROWS), :], cf), acc + 1.0, acc)
            acc = lax.fori_loop(0, n_steps, body, jnp.zeros((CROWS, TQ), F32))
            return acc.sum(axis=0, keepdims=True)

        captured = count_compact(f_lo)
        missed = jnp.max(jnp.abs(captured - (cnt_top - above)))

        @pl.when(missed == 0.0)
        def _():
            def cbit_body(bi, carry):
                u, cnt_u = carry
                cand = u | lax.shift_left(I32(1), I32(15) - bi)
                cnt = above + count_compact(to_float(cand))
                keep = cnt >= float(TOPK)
                return jnp.where(keep, cand, u), jnp.where(keep, cnt, cnt_u)

            u, cnt = lax.fori_loop(0, 16, cbit_body, carry)
            thr_ref[...] = to_float(u)
            cge_ref[...] = cnt
            cgt_ref[...] = above + count_compact(to_float(u), jnp.greater)

        @pl.when(missed != 0.0)
        def _():
            u, cnt = lax.fori_loop(0, 16, bit_body, carry)
            thr = to_float(u)
            thr_ref[...] = thr
            cge_ref[...] = cnt
            cgt_ref[...] = count(lambda x, r0: x > thr)

        thr = thr_ref[...]

        @pl.when(jnp.max(cge_ref[...]) > float(TOPK))
        def _():
            need = float(TOPK) - cgt_ref[...]
            tri_r = lax.broadcasted_iota(I32, (KT, KT), 0)
            tri_c = lax.broadcasted_iota(I32, (KT, KT), 1)
            tri_ref[...] = jnp.where(tri_r >= tri_c, 1.0, 0.0).astype(BF16)

            def drop_step(c, seen):
                rows = [pl.multiple_of((2 * c + k) * KT, KT) for k in range(2)]
                xs = [sc_ref[pl.ds(r0, KT), :] for r0 in rows]
                es = [jnp.where(x == thr, 1.0, 0.0) for x in xs]
                uptos = [jnp.dot(tri_ref[...], e.astype(BF16), preferred_element_type=F32)
                         for e in es]
                for r0, x, e, upto in zip(rows, xs, es, uptos):
                    gone = jnp.logical_and(e > 0.0, upto - e + seen >= need)
                    sc_ref[pl.ds(r0, KT), :] = jnp.where(gone, NEG, x)
                    seen = seen + upto[KT - 1:KT, :]
                return seen

            lax.fori_loop(0, n_steps, drop_step, jnp.zeros((1, TQ), F32))

    m_ref[...] = jnp.full(m_ref.shape, NEG, F32)
    acc_ref[...] = jnp.zeros(acc_ref.shape, F32)
    thr = thr_ref[...]

    def to_mask(c, carry):
        r0 = pl.multiple_of(c * KT, KT)
        sc_ref[pl.ds(r0, KT), :] = jnp.where(sc_ref[pl.ds(r0, KT), :] >= thr, F32(0.0), F32(NEG))
        return carry

    lax.fori_loop(0, n_tiles, to_mask, 0)

    stages = ((stage0_ref, tmax0_ref), (stage1_ref, tmax1_ref))

    def head_a(t, slot, h):
        stage_ref, tmax_ref = stages[slot]
        r0 = pl.multiple_of(t * KT, KT)
        dl = jnp.minimum(i - t, N_NEAR)
        j = h // 2
        kk = k_ref[0, pl.ds(r0, KT), LANES * j:LANES * (j + 1)]
        s = lax.dot_general(kk, qw_ref[h], _NT_DIMS, preferred_element_type=F32)
        s = s + sc_ref[pl.ds(r0, KT), :] + bias_ref[dl, h]
        stage_ref[h] = s
        tmax_ref[h:h + 1, :] = s.reshape(KT // ACC_ROWS, ACC_ROWS, TQ).max(axis=0).max(
            axis=0, keepdims=True)

    def begin_b(slot):
        m_old = m_ref[...]
        m_new = jnp.maximum(m_old, stages[slot][1][...])
        m_ref[...] = m_new
        return m_new, jnp.exp2(m_old - m_new)

    def head_b(t, slot, h, m_new, alpha):
        p = jnp.exp2(stages[slot][0][h] - m_new[h:h + 1, :])
        pv = jnp.dot(vt_ref[0, t, V_ROWS * h:V_ROWS * (h + 1), :], p.astype(BF16),
                     preferred_element_type=F32)
        acc_ref[V_ROWS * h:V_ROWS * (h + 1), :] = (
            alpha[h:h + 1, :] * acc_ref[V_ROWS * h:V_ROWS * (h + 1), :] + pv)

    def pass_a(t, slot):
        for h in range(N_HEADS):
            head_a(t, slot, h)

    def pass_b(t, slot):
        m_new, alpha = begin_b(slot)
        for h in range(N_HEADS):
            head_b(t, slot, h, m_new, alpha)

    pass_a(0, 0)
    n_pairs = (n_tiles - 1) // 2

    def attn_pair(pr, carry):
        t = 2 * pr
        pass_a(t + 1, 1)
        pass_b(t, 0)
        pass_a(t + 2, 0)
        pass_b(t + 1, 1)
        return carry

    lax.fori_loop(0, n_pairs, attn_pair, 0)
    t_last = 2 * n_pairs

    @pl.when(n_tiles - 1 > t_last)
    def _():
        pass_a(t_last + 1, 1)
        pass_b(t_last, 0)
        pass_b(t_last + 1, 1)

    @pl.when(n_tiles - 1 == t_last)
    def _():
        pass_b(t_last, 0)

    heads = []
    for h in range(N_HEADS):
        denom = acc_ref[V_ROWS * h + HEAD_DIM:V_ROWS * h + HEAD_DIM + 1, :]
        heads.append(acc_ref[V_ROWS * h:V_ROWS * h + HEAD_DIM, :] * (1.0 / denom))
    y = jnp.concatenate(heads, axis=0).T
    o_ref[0] = (y * jax.nn.silu(az_ref[0])).astype(BF16)


def _dsa(aq, iq, iwt, az, ik2, ak, avt, bias):
    b, s, _ = aq.shape
    nq = s // TQ
    qrow = lambda w: pl.BlockSpec((1, TQ, w), lambda bi, i: (bi, i, 0))
    once = pl.Buffered(1)
    small = pltpu.VMEM((BANK_ROWS, TQ), F32)
    return pl.pallas_call(
        _dsa_kernel,
        grid=(b, nq),
        in_specs=[qrow(ATTN_WIDTH), qrow(IDX_HEADS * IDX_DIM),
                  pl.BlockSpec((1, IDX_HEADS, TQ), lambda bi, i: (bi, 0, i)),
                  qrow(ATTN_WIDTH),
                  pl.BlockSpec((1, s, LANES), lambda bi, i: (bi, 0, 0), pipeline_mode=once),
                  pl.BlockSpec((1, s, ATTN_WIDTH), lambda bi, i: (bi, 0, 0), pipeline_mode=once),
                  pl.BlockSpec((1, s // KT, N_HEADS * V_ROWS, KT), lambda bi, i: (bi, 0, 0, 0),
                               pipeline_mode=once),
                  pl.BlockSpec(bias.shape, lambda bi, i: (0, 0, 0, 0), pipeline_mode=once)],
        out_specs=qrow(ATTN_WIDTH),
        out_shape=jax.ShapeDtypeStruct((b, s, ATTN_WIDTH), BF16),
        scratch_shapes=[pltpu.VMEM((s, TQ), F32),
                        pltpu.VMEM((s, TQ), BF16),
                        pltpu.VMEM((s // GROUP_ROWS * SUBLANES, TQ), F32),
                        pltpu.VMEM((s // GROUP_ROWS * SUBLANES, TQ), F32),
                        pltpu.VMEM((KT, KT), BF16),
                        small,
                        small,
                        pltpu.VMEM((N_HEADS, TQ, LANES), BF16),
                        pltpu.VMEM((IDX_HEADS, TQ, LANES), BF16),
                        pltpu.VMEM((N_HEADS * V_ROWS, TQ), F32),
                        small,
                        small,
                        pltpu.VMEM((N_HEADS, KT, TQ), F32),
                        pltpu.VMEM((N_HEADS, KT, TQ), F32),
                        small,
                        small],
        compiler_params=pltpu.CompilerParams(
            dimension_semantics=("arbitrary", "arbitrary"), vmem_limit_bytes=VMEM_LIMIT),
        name="dsa",
    )(aq, iq, iwt, az, ik2, ak, avt, bias)


def _merge_kernel(final, x_ref, pu_ref, halo_ref, pz_ref, mq_ref, mz_ref, ya_ref, gates_ref,
                  mk_ref, mv_ref, pw_ref, ps_ref, wb_ref, wo_ref, fg_ref, o_ref):
    i = pl.program_id(1)

    u = pu_ref[0]
    halo = jnp.where(i == 0, 0.0, halo_ref[0])
    ext = jnp.concatenate([halo, u], axis=0)
    t_glob = i * TM + lax.broadcasted_iota(I32, (TM, POOL_GC), 0)
    mixed = []
    for g in range(POOL_GROUPS):
        sl = slice(POOL_GC * g, POOL_GC * (g + 1))
        wsum = ext[:, sl]
        for step in range(g + 1):
            wsum = wsum + pltpu.roll(wsum, 2 ** step, axis=0)
        cnt = jnp.minimum(t_glob + 1, POOL_WINDOWS[g]).astype(F32)
        pooled = wsum[HALO:, :] / cnt - u[:, sl]
        mixed.append(jnp.dot(pooled.astype(BF16), pw_ref[g], preferred_element_type=F32))
    y_pool = jnp.concatenate(mixed, axis=1) * ps_ref[...] * jax.nn.silu(pz_ref[0])

    mem_out = []
    for h in range(MEM_HEADS):
        sl = slice(MEM_HEAD_DIM * h, MEM_HEAD_DIM * (h + 1))
        logits = lax.dot_general(mq_ref[0, :, sl], mk_ref[0, :, sl], _NT_DIMS,
                                 preferred_element_type=F32) * MEM_SCALE
        e = jnp.exp(logits - logits.max(axis=-1, keepdims=True))
        p = e / e.sum(axis=-1, keepdims=True)
        mem_out.append(jnp.dot(p.astype(BF16), mv_ref[0, :, sl], preferred_element_type=F32))
    y_mem = jnp.concatenate(mem_out, axis=1) * jax.nn.silu(mz_ref[0])

    branches = (y_pool.astype(BF16), ya_ref[0], y_mem.astype(BF16))
    merged = jnp.zeros((TM, D_MODEL), F32)
    for br in range(N_BRANCH):
        gate = jax.nn.sigmoid(gates_ref[0, :, D_MODEL * br:D_MODEL * (br + 1)])
        merged = merged + gate * jnp.dot(branches[br], wb_ref[br], preferred_element_type=F32)
    out = x_ref[0] + jnp.dot(merged.astype(BF16), wo_ref[...], preferred_element_type=F32)
    if final:
        out = _rms_norm_f32(out, fg_ref[...])
    o_ref[0] = out


def _merge(final, x, pu, pz, mq, mz, ya, gates, mk, mv, pw, ps, wb, wo, fg):
    b, s, d = x.shape
    row = lambda w: pl.BlockSpec((1, TM, w), lambda bi, i: (bi, i, 0))
    full = lambda a: pl.BlockSpec(a.shape, lambda bi, i: (0,) * a.ndim)
    per_b = lambda a: pl.BlockSpec((1,) + a.shape[1:], lambda bi, i: (bi,) + (0,) * (a.ndim - 1))
    halo = pl.BlockSpec((1, HALO, POOL_WIDTH),
                        lambda bi, i: (bi, jnp.maximum(i * (TM // HALO) - 1, 0), 0))
    return pl.pallas_call(
        functools.partial(_merge_kernel, final),
        grid=(b, s // TM),
        in_specs=[row(d), row(POOL_WIDTH), halo, row(POOL_WIDTH), row(MEM_WIDTH), row(MEM_WIDTH),
                  row(ATTN_WIDTH), row(N_BRANCH * D_MODEL), per_b(mk), per_b(mv),
                  full(pw), full(ps), full(wb), full(wo), full(fg)],
        out_specs=row(d),
        out_shape=jax.ShapeDtypeStruct((b, s, d), F32),
        compiler_params=pltpu.CompilerParams(
            dimension_semantics=("arbitrary", "arbitrary"), vmem_limit_bytes=VMEM_LIMIT),
        name="merge_final" if final else "merge",
    )(x, pu, pu, pz, mq, mz, ya, gates, mk, mv, pw, ps, wb, wo, fg)


def _split_w_in(w):
    col = lambda lo, hi: w[:, lo:hi]
    wf = jnp.concatenate([col(O_PU, O_PZ), col(O_PZ, O_AQ), col(O_AZ, O_IQ), col(O_MZ, O_G),
                          col(O_G, O_END)], axis=1)
    wb = jnp.concatenate([col(O_AQ, O_AK) * (ATTN_SCALE * LOG2E), col(O_AK, O_AV), col(O_IQ, O_IK),
                          col(O_MQ, O_MZ), col(O_IK, O_IW), col(O_IK, O_IW)], axis=1)
    wt = jnp.concatenate([col(O_AV, O_AZ), col(O_IW, O_MQ),
                          jnp.zeros((w.shape[0], WT_ROWS - ATTN_WIDTH - IDX_HEADS), w.dtype)], axis=1).T
    return wf.astype(BF16), wb.astype(BF16), wt.astype(BF16)


def kernel(x, mem, norm_g, w_in, pool_w, pool_scale, mem_norm_g, w_mem_kv, w_branch, w_out,
           rel_bias, final_g):
    depth = w_in.shape[0]
    assert x.shape[1] % COUNT_ROWS == 0 and x.shape[2] == D_MODEL and mem.shape[1] == MEM_LEN
    assert w_in.shape[2] == O_END and TQ == KT and PTM % KT == 0
    assert x.shape[1] // TOP_ROWS <= 256
    bias = _bias_tiles(rel_bias)
    fg = final_g.reshape(1, D_MODEL)
    for l in range(depth):
        wf, wb, wt = _split_w_in(w_in[l])
        (pu, pz, az, mz, gates, aq, ak, iq, mq, ik2, avt, iwt) = _project(
            x, norm_g[l].reshape(1, D_MODEL), wf, wb, wt)
        mk, mv = _mem_kv(mem, mem_norm_g[l].reshape(1, D_MODEL), w_mem_kv[l].astype(BF16))
        ya = _dsa(aq, iq, iwt, az, ik2, ak, avt, bias)
        x = _merge(l == depth - 1, x, pu, pz, mq, mz, ya, gates, mk, mv,
                   pool_w[l].astype(BF16), pool_scale[l].reshape(1, POOL_WIDTH),
                   w_branch[l].astype(BF16), w_out[l].astype(BF16), fg)
    return x
```

```python
import functools
import math

import numpy as np
import jax
import jax.numpy as jnp
from jax import lax
from jax.experimental import pallas as pl
from jax.experimental.pallas import tpu as pltpu

F32 = jnp.float32
BF16 = jnp.bfloat16
I32 = jnp.int32

D_MODEL = 1024
CHUNK = 64
MEM_LEN = 256
POOL_GROUPS = 4
POOL_WINDOWS = (2, 4, 8, 16)
POOL_WIDTH = D_MODEL // 2
POOL_GC = POOL_WIDTH // POOL_GROUPS
N_HEADS = 8
HEAD_DIM = 64
ATTN_WIDTH = N_HEADS * HEAD_DIM
IDX_HEADS = 8
IDX_DIM = 64
TOPK = 256
ATTN_SCALE = HEAD_DIM ** -0.5
LOG2E = math.log2(math.e)
IDX_W_SCALE = (IDX_DIM ** -0.5) * (IDX_HEADS ** -0.5)
MEM_HEADS = 4
MEM_HEAD_DIM = 128
MEM_WIDTH = MEM_HEADS * MEM_HEAD_DIM
MEM_SCALE = MEM_HEAD_DIM ** -0.5
N_BRANCH = 3
REL_BUCKETS = 32
REL_MAX_DIST = 1024
EPS = 1e-6
NEG = -1e30
HALF_NEG = -5e29

IN_SPLITS = (POOL_WIDTH, POOL_WIDTH, ATTN_WIDTH, ATTN_WIDTH, ATTN_WIDTH, ATTN_WIDTH,
             IDX_HEADS * IDX_DIM, IDX_DIM, IDX_HEADS, MEM_WIDTH, MEM_WIDTH, N_BRANCH * D_MODEL)
_OFF = np.concatenate([[0], np.cumsum(IN_SPLITS)]).tolist()
(O_PU, O_PZ, O_AQ, O_AK, O_AV, O_AZ, O_IQ, O_IK, O_IW, O_MQ, O_MZ, O_G, O_END) = _OFF

LANES = 128
SUBLANES = 8
TQ = 256
KT = 256
TM = 512
PTM = 512
HALO = 16
N_NEAR = 4
COUNT_ROWS = 2 * KT
ACC_ROWS = 4 * SUBLANES
TOP_ROWS = 8 * SUBLANES
BANK_ROWS = 16
GROUP_ROWS = 128
CROWS = COUNT_ROWS // GROUP_ROWS * SUBLANES
WT_ROWS = 528
V_ROWS = HEAD_DIM + 16
VMEM_LIMIT = 56 * 1024 * 1024

_NT_DIMS = (((1,), (1,)), ((), ()))


def _rms_norm_f32(x, g):
    ms = jnp.mean(x * x, axis=-1, keepdims=True)
    return x * lax.rsqrt(ms + EPS) * g


def _proj_kernel(x_ref, g_ref, wf_ref, wb_ref, wt_ref,
                 pu_ref, pz_ref, az_ref, mz_ref, gates_ref,
                 aq_ref, ak_ref, iq_ref, mq_ref, ik2_ref, avt_ref, iwt_ref):
    h = _rms_norm_f32(x_ref[0], g_ref[...]).astype(BF16)

    def mm(w_ref, lo, hi):
        return jnp.dot(h, w_ref[:, lo:hi], preferred_element_type=F32)

    pu_ref[0] = mm(wf_ref, 0, 512)
    pz_ref[0] = mm(wf_ref, 512, 1024)
    az_ref[0] = mm(wf_ref, 1024, 1536)
    mz_ref[0] = mm(wf_ref, 1536, 2048)
    for c in range(N_BRANCH * D_MODEL // 512):
        gates_ref[0, :, 512 * c:512 * (c + 1)] = mm(wf_ref, 2048 + 512 * c, 2048 + 512 * (c + 1))
    aq_ref[0] = mm(wb_ref, 0, 512).astype(BF16)
    ak_ref[0] = mm(wb_ref, 512, 1024).astype(BF16)
    iq_ref[0] = mm(wb_ref, 1024, 1536).astype(BF16)
    mq_ref[0] = mm(wb_ref, 1536, 2048).astype(BF16)
    ik2_ref[0] = mm(wb_ref, 2048, 2176).astype(BF16)
    t = lax.dot_general(wt_ref[...], h, _NT_DIMS, preferred_element_type=F32)
    pad_row = lax.broadcasted_iota(I32, (V_ROWS - HEAD_DIM, KT), 0)
    pad = jnp.where(pad_row == 0, 1.0, 0.0).astype(BF16)
    for kt in range(PTM // KT):
        for hd in range(N_HEADS):
            avt_ref[0, kt, V_ROWS * hd:V_ROWS * hd + HEAD_DIM, :] = (
                t[HEAD_DIM * hd:HEAD_DIM * (hd + 1), KT * kt:KT * (kt + 1)].astype(BF16))
            avt_ref[0, kt, V_ROWS * hd + HEAD_DIM:V_ROWS * (hd + 1), :] = pad
    iwt_ref[0] = t[ATTN_WIDTH:ATTN_WIDTH + IDX_HEADS] * IDX_W_SCALE


def _project(x, g, wf, wb, wt):
    b, s, d = x.shape
    row = lambda w: pl.BlockSpec((1, PTM, w), lambda bi, i: (bi, i, 0))
    full = lambda a: pl.BlockSpec(a.shape, lambda bi, i: (0,) * a.ndim,
                                  pipeline_mode=pl.Buffered(1))
    f32o = lambda w: jax.ShapeDtypeStruct((b, s, w), F32)
    b16o = lambda w: jax.ShapeDtypeStruct((b, s, w), BF16)
    out_shape = (f32o(512), f32o(512), f32o(512), f32o(512), f32o(N_BRANCH * D_MODEL),
                 b16o(512), b16o(512), b16o(512), b16o(512), b16o(128),
                 jax.ShapeDtypeStruct((b, s // KT, N_HEADS * V_ROWS, KT), BF16),
                 jax.ShapeDtypeStruct((b, IDX_HEADS, s), F32))
    out_specs = (row(512), row(512), row(512), row(512), row(N_BRANCH * D_MODEL),
                 row(512), row(512), row(512), row(512), row(128),
                 pl.BlockSpec((1, PTM // KT, N_HEADS * V_ROWS, KT), lambda bi, i: (bi, i, 0, 0)),
                 pl.BlockSpec((1, IDX_HEADS, PTM), lambda bi, i: (bi, 0, i)))
    return pl.pallas_call(
        _proj_kernel,
        grid=(b, s // PTM),
        in_specs=[row(d), full(g), full(wf), full(wb), full(wt)],
        out_specs=out_specs,
        out_shape=out_shape,
        compiler_params=pltpu.CompilerParams(
            dimension_semantics=("arbitrary", "arbitrary"), vmem_limit_bytes=VMEM_LIMIT),
        name="proj",
    )(x, g, wf, wb, wt)


def _memkv_kernel(mem_ref, g_ref, w_ref, mk_ref, mv_ref):
    h = _rms_norm_f32(mem_ref[0], g_ref[...]).astype(BF16)
    mk_ref[0] = jnp.dot(h, w_ref[:, 0:MEM_WIDTH], preferred_element_type=F32).astype(BF16)
    mv_ref[0] = jnp.dot(h, w_ref[:, MEM_WIDTH:2 * MEM_WIDTH], preferred_element_type=F32).astype(BF16)


def _mem_kv(mem, g, w):
    b, m, d = mem.shape
    o = jax.ShapeDtypeStruct((b, m, MEM_WIDTH), BF16)
    return pl.pallas_call(
        _memkv_kernel,
        grid=(b,),
        in_specs=[pl.BlockSpec((1, m, d), lambda bi: (bi, 0, 0)),
                  pl.BlockSpec(g.shape, lambda bi: (0, 0)),
                  pl.BlockSpec(w.shape, lambda bi: (0, 0))],
        out_specs=(pl.BlockSpec((1, m, MEM_WIDTH), lambda bi: (bi, 0, 0)),) * 2,
        out_shape=(o, o),
        compiler_params=pltpu.CompilerParams(
            dimension_semantics=("arbitrary",), vmem_limit_bytes=VMEM_LIMIT),
        name="memkv",
    )(mem, g, w)


def _bucket_table():
    half = REL_BUCKETS // 2
    max_exact = half // 2
    delta = np.arange(N_NEAR + 1, dtype=np.int64)[:, None, None]
    key = np.arange(KT, dtype=np.int64)[None, :, None]
    qry = np.arange(TQ, dtype=np.int64)[None, None, :]
    rel = key - qry - delta * KT
    ret = np.where(rel > 0, half, 0)
    n = np.abs(rel)
    nf = np.maximum(n, 1).astype(np.float32)
    large = max_exact + (np.log(nf / np.float32(max_exact))
                         / np.float32(math.log(REL_MAX_DIST / max_exact))
                         * np.float32(half - max_exact)).astype(np.int32)
    large = np.minimum(large, half - 1)
    return (ret + np.where(n < max_exact, n, large)).astype(np.int32)


FAR_BUCKET = REL_BUCKETS // 2 - 1


def _bias_kernel(rb_ref, bk_ref, o_ref):
    h = pl.program_id(1)
    bk = bk_ref[0]
    far = rb_ref[FAR_BUCKET, h]
    acc = jnp.zeros((KT, TQ), F32)
    for bkt in range(REL_BUCKETS):
        acc = jnp.where(bk == bkt, (rb_ref[bkt, h] - far) * LOG2E, acc)
    o_ref[0, 0] = acc


def _bias_tiles(rel_bias):
    table = jnp.asarray(_bucket_table())
    return pl.pallas_call(
        _bias_kernel,
        grid=(N_NEAR + 1, N_HEADS),
        in_specs=[pl.BlockSpec(memory_space=pltpu.SMEM),
                  pl.BlockSpec((1, KT, TQ), lambda dl, h: (dl, 0, 0))],
        out_specs=pl.BlockSpec((1, 1, KT, TQ), lambda dl, h: (dl, h, 0, 0)),
        out_shape=jax.ShapeDtypeStruct((N_NEAR + 1, N_HEADS, KT, TQ), F32),
        compiler_params=pltpu.CompilerParams(dimension_semantics=("arbitrary", "arbitrary")),
        name="bias_tiles",
    )(rel_bias, table)


def _dsa_kernel(q_ref, iq_ref, iwt_ref, az_ref, ik2_ref, k_ref, vt_ref, bias_ref, o_ref,
                sc_ref, hi_ref, c1_ref, c2_ref, tri_ref, cge_buf, cgt_buf, qw_ref, iqw_ref, acc_ref, m_buf, thr_buf,
                stage0_ref, stage1_ref, tmax0_buf, tmax1_buf):
    i = pl.program_id(1)
    n_tiles = i + 1
    cge_ref, cgt_ref, thr_ref = cge_buf.at[0:1], cgt_buf.at[0:1], thr_buf.at[0:1]
    m_ref, tmax0_ref, tmax1_ref = (r.at[0:N_HEADS] for r in (m_buf, tmax0_buf, tmax1_buf))

    lane = lax.broadcasted_iota(I32, (TQ, LANES), 1)
    keep_lo = jnp.where(lane < HEAD_DIM, 1.0, 0.0).astype(BF16)
    keep_hi = jnp.where(lane >= HEAD_DIM, 1.0, 0.0).astype(BF16)
    for h in range(N_HEADS):
        j = h // 2
        keep = keep_hi if (h % 2) else keep_lo
        qw_ref[h] = q_ref[0, :, LANES * j:LANES * (j + 1)] * keep
        iqw_ref[h] = iq_ref[0, :, LANES * j:LANES * (j + 1)] * keep

    key_l = lax.broadcasted_iota(I32, (KT, TQ), 0)
    qry_l = lax.broadcasted_iota(I32, (KT, TQ), 1)
    adm_diag = key_l < ((qry_l // CHUNK) + 1) * CHUNK

    def score_tile(t):
        r0 = pl.multiple_of(t * KT, KT)
        ikt = ik2_ref[0, pl.ds(r0, KT), :]
        acc = jnp.zeros((KT, TQ), F32)
        for h in range(IDX_HEADS):
            y = lax.dot_general(ikt, iqw_ref[h], _NT_DIMS, preferred_element_type=F32)
            acc = acc + jnp.maximum(y, 0.0) * iwt_ref[0, h:h + 1, :]
        adm = jnp.logical_or(jnp.logical_and(adm_diag, t == i), t < i)
        val = jnp.where(adm, acc, NEG)
        sc_ref[pl.ds(r0, KT), :] = val
        top = lax.bitcast_convert_type(val, I32) & I32(-(1 << 16))
        hi_ref[pl.ds(r0, KT), :] = lax.bitcast_convert_type(top, F32).astype(BF16)

    n_steps = (n_tiles + 1) // 2

    def score_step(c, carry):
        score_tile(2 * c)
        score_tile(2 * c + 1)
        return carry

    def score_two_steps(c2, carry):
        score_step(2 * c2, carry)
        return score_step(2 * c2 + 1, carry)

    lax.fori_loop(0, n_steps // 2, score_two_steps, 0)
    lax.fori_loop(n_steps // 2 * 2, n_steps, score_step, 0)

    def count(pred):
        def body(c, acc):
            for k in range(COUNT_ROWS // ACC_ROWS):
                r0 = pl.multiple_of(c * COUNT_ROWS + k * ACC_ROWS, ACC_ROWS)
                x = sc_ref[pl.ds(r0, ACC_ROWS), :]
                acc = jnp.where(pred(x, r0), acc + 1.0, acc)
            return acc
        acc = lax.fori_loop(0, n_steps, body, jnp.zeros((ACC_ROWS, TQ), F32))
        return acc.sum(axis=0, keepdims=True)

    def to_float(u):
        bits = jnp.where(u < 0, u ^ I32(-2 ** 31), ~u)
        return lax.bitcast_convert_type(bits, F32)

    @pl.when(i == 0)
    def _():
        thr_ref[...] = jnp.full((1, TQ), HALF_NEG, F32)

    def count_top(cand_bf16):
        cb = jnp.broadcast_to(cand_bf16, (TOP_ROWS, TQ))

        def body(c, acc):
            for k in range(COUNT_ROWS // TOP_ROWS):
                r0 = pl.multiple_of(c * COUNT_ROWS + k * TOP_ROWS, TOP_ROWS)
                acc = jnp.where(hi_ref[pl.ds(r0, TOP_ROWS), :] >= cb, acc + 1.0, acc)
            return acc
        acc = lax.fori_loop(0, n_steps, body, jnp.zeros((TOP_ROWS, TQ), BF16))
        return acc.astype(F32).sum(axis=0, keepdims=True)

    @pl.when(i > 0)
    def _():
        def top_body(bi, carry):
            u, cnt_u = carry
            cand = u | lax.shift_left(I32(1 << 16), I32(15) - bi)
            top = lax.bitcast_convert_type(to_float(cand), I32) & I32(-(1 << 16))
            cnt = count_top(lax.bitcast_convert_type(top, F32).astype(BF16))
            keep = cnt >= float(TOPK)
            return jnp.where(keep, cand, u), jnp.where(keep, cnt, cnt_u)

        def bit_body(bi, carry):
            u, cnt_u = carry
            cand = u | lax.shift_left(I32(1), I32(15) - bi)
            cf = to_float(cand)
            cnt = count(lambda x, r0: x >= cf)
            keep = cnt >= float(TOPK)
            return jnp.where(keep, cand, u), jnp.where(keep, cnt, cnt_u)

        carry = (jnp.zeros((1, TQ), I32), jnp.zeros((1, TQ), F32))
        carry = lax.fori_loop(0, 16, top_body, carry)

        u_top, cnt_top = carry
        f_lo = to_float(u_top)
        f_hi = to_float(u_top + I32(1 << 16))
        top_hi = lax.bitcast_convert_type(f_hi, I32) & I32(-(1 << 16))
        above = count_top(lax.bitcast_convert_type(top_hi, F32).astype(BF16))
        f_hi_b = jnp.broadcast_to(f_hi, (SUBLANES, TQ))

        def compact_step(c, carry):
            for g in range(COUNT_ROWS // GROUP_ROWS):
                m1 = jnp.full((SUBLANES, TQ), NEG, F32)
                m2 = m1
                for k in range(GROUP_ROWS // SUBLANES):
                    r0 = pl.multiple_of(c * COUNT_ROWS + g * GROUP_ROWS + k * SUBLANES, SUBLANES)
                    v = sc_ref[pl.ds(r0, SUBLANES), :]
                    v = jnp.where(v >= f_hi_b, NEG, v)
                    m2 = jnp.maximum(m2, jnp.minimum(m1, v))
                    m1 = jnp.maximum(m1, v)
                o0 = pl.multiple_of((c * (COUNT_ROWS // GROUP_ROWS) + g) * SUBLANES, SUBLANES)
                c1_ref[pl.ds(o0, SUBLANES), :] = m1
                c2_ref[pl.ds(o0, SUBLANES), :] = m2
            return carry

        lax.fori_loop(0, n_steps, compact_step, 0)

        def count_compact(cf, cmp=jnp.greater_equal):
            def body(c, acc):
                r0 = pl.multiple_of(c * CROWS, CROWS)
                acc = jnp.where(cmp(c1_ref[pl.ds(r0, CROWS), :], cf), acc + 1.0, acc)
                return jnp.where(cmp(c2_ref[pl.ds(r0, CROWS), :], cf), acc + 1.0, acc)
            acc = lax.fori_loop(0, n_steps, body, jnp.zeros((CROWS, TQ), F32))
            return acc.sum(axis=0, keepdims=True)

        captured = count_compact(f_lo)
        missed = jnp.max(jnp.abs(captured - (cnt_top - above)))

        @pl.when(missed == 0.0)
        def _():
            def cbit_body(bi, carry):
                u, cnt_u = carry
                cand = u | lax.shift_left(I32(1), I32(15) - bi)
                cnt = above + count_compact(to_float(cand))
                keep = cnt >= float(TOPK)
                return jnp.where(keep, cand, u), jnp.where(keep, cnt, cnt_u)

            u, cnt = lax.fori_loop(0, 16, cbit_body, carry)
            thr_ref[...] = to_float(u)
            cge_ref[...] = cnt
            cgt_ref[...] = above + count_compact(to_float(u), jnp.greater)

        @pl.when(missed != 0.0)
        def _():
            u, cnt = lax.fori_loop(0, 16, bit_body, carry)
            thr = to_float(u)
            thr_ref[...] = thr
            cge_ref[...] = cnt
            cgt_ref[...] = count(lambda x, r0: x > thr)

        thr = thr_ref[...]

        @pl.when(jnp.max(cge_ref[...]) > float(TOPK))
        def _():
            need = float(TOPK) - cgt_ref[...]
            tri_r = lax.broadcasted_iota(I32, (KT, KT), 0)
            tri_c = lax.broadcasted_iota(I32, (KT, KT), 1)
            tri_ref[...] = jnp.where(tri_r >= tri_c, 1.0, 0.0).astype(BF16)

            def drop_step(c, seen):
                rows = [pl.multiple_of((2 * c + k) * KT, KT) for k in range(2)]
                xs = [sc_ref[pl.ds(r0, KT), :] for r0 in rows]
                es = [jnp.where(x == thr, 1.0, 0.0) for x in xs]
                uptos = [jnp.dot(tri_ref[...], e.astype(BF16), preferred_element_type=F32)
                         for e in es]
                for r0, x, e, upto in zip(rows, xs, es, uptos):
                    gone = jnp.logical_and(e > 0.0, upto - e + seen >= need)
                    sc_ref[pl.ds(r0, KT), :] = jnp.where(gone, NEG, x)
                    seen = seen + upto[KT - 1:KT, :]
                return seen

            lax.fori_loop(0, n_steps, drop_step, jnp.zeros((1, TQ), F32))

    m_ref[...] = jnp.full(m_ref.shape, NEG, F32)
    acc_ref[...] = jnp.zeros(acc_ref.shape, F32)
    thr = thr_ref[...]

    def to_mask(c, carry):
        r0 = pl.multiple_of(c * KT, KT)
        sc_ref[pl.ds(r0, KT), :] = jnp.where(sc_ref[pl.ds(r0, KT), :] >= thr, F32(0.0), F32(NEG))
        return carry

    lax.fori_loop(0, n_tiles, to_mask, 0)

    stages = ((stage0_ref, tmax0_ref), (stage1_ref, tmax1_ref))

    def head_a(t, slot, h):
        stage_ref, tmax_ref = stages[slot]
        r0 = pl.multiple_of(t * KT, KT)
        dl = jnp.minimum(i - t, N_NEAR)
        j = h // 2
        kk = k_ref[0, pl.ds(r0, KT), LANES * j:LANES * (j + 1)]
        s = lax.dot_general(kk, qw_ref[h], _NT_DIMS, preferred_element_type=F32)
        s = s + sc_ref[pl.ds(r0, KT), :] + bias_ref[dl, h]
        stage_ref[h] = s
        tmax_ref[h:h + 1, :] = s.reshape(KT // ACC_ROWS, ACC_ROWS, TQ).max(axis=0).max(
            axis=0, keepdims=True)

    def begin_b(slot):
        m_old = m_ref[...]
        m_new = jnp.maximum(m_old, stages[slot][1][...])
        m_ref[...] = m_new
        return m_new, jnp.exp2(m_old - m_new)

    def head_b(t, slot, h, m_new, alpha):
        p = jnp.exp2(stages[slot][0][h] - m_new[h:h + 1, :])
        pv = jnp.dot(vt_ref[0, t, V_ROWS * h:V_ROWS * (h + 1), :], p.astype(BF16),
                     preferred_element_type=F32)
        acc_ref[V_ROWS * h:V_ROWS * (h + 1), :] = (
            alpha[h:h + 1, :] * acc_ref[V_ROWS * h:V_ROWS * (h + 1), :] + pv)

    def pass_a(t, slot):
        for h in range(N_HEADS):
            head_a(t, slot, h)

    def pass_b(t, slot):
        m_new, alpha = begin_b(slot)
        for h in range(N_HEADS):
            head_b(t, slot, h, m_new, alpha)

    pass_a(0, 0)
    n_pairs = (n_tiles - 1) // 2

    def attn_pair(pr, carry):
        t = 2 * pr
        pass_a(t + 1, 1)
        pass_b(t, 0)
        pass_a(t + 2, 0)
        pass_b(t + 1, 1)
        return carry

    lax.fori_loop(0, n_pairs, attn_pair, 0)
    t_last = 2 * n_pairs

    @pl.when(n_tiles - 1 > t_last)
    def _():
        pass_a(t_last + 1, 1)
        pass_b(t_last, 0)
        pass_b(t_last + 1, 1)

    @pl.when(n_tiles - 1 == t_last)
    def _():
        pass_b(t_last, 0)

    heads = []
    for h in range(N_HEADS):
        denom = acc_ref[V_ROWS * h + HEAD_DIM:V_ROWS * h + HEAD_DIM + 1, :]
        heads.append(acc_ref[V_ROWS * h:V_ROWS * h + HEAD_DIM, :] * (1.0 / denom))
    y = jnp.concatenate(heads, axis=0).T
    o_ref[0] = (y * jax.nn.silu(az_ref[0])).astype(BF16)


def _dsa(aq, iq, iwt, az, ik2, ak, avt, bias):
    b, s, _ = aq.shape
    nq = s // TQ
    qrow = lambda w: pl.BlockSpec((1, TQ, w), lambda bi, i: (bi, i, 0))
    once = pl.Buffered(1)
    small = pltpu.VMEM((BANK_ROWS, TQ), F32)
    return pl.pallas_call(
        _dsa_kernel,
        grid=(b, nq),
        in_specs=[qrow(ATTN_WIDTH), qrow(IDX_HEADS * IDX_DIM),
                  pl.BlockSpec((1, IDX_HEADS, TQ), lambda bi, i: (bi, 0, i)),
                  qrow(ATTN_WIDTH),
                  pl.BlockSpec((1, s, LANES), lambda bi, i: (bi, 0, 0), pipeline_mode=once),
                  pl.BlockSpec((1, s, ATTN_WIDTH), lambda bi, i: (bi, 0, 0), pipeline_mode=once),
                  pl.BlockSpec((1, s // KT, N_HEADS * V_ROWS, KT), lambda bi, i: (bi, 0, 0, 0),
                               pipeline_mode=once),
                  pl.BlockSpec(bias.shape, lambda bi, i: (0, 0, 0, 0), pipeline_mode=once)],
        out_specs=qrow(ATTN_WIDTH),
        out_shape=jax.ShapeDtypeStruct((b, s, ATTN_WIDTH), BF16),
        scratch_shapes=[pltpu.VMEM((s, TQ), F32),
                        pltpu.VMEM((s, TQ), BF16),
                        pltpu.VMEM((s // GROUP_ROWS * SUBLANES, TQ), F32),
                        pltpu.VMEM((s // GROUP_ROWS * SUBLANES, TQ), F32),
                        pltpu.VMEM((KT, KT), BF16),
                        small,
                        small,
                        pltpu.VMEM((N_HEADS, TQ, LANES), BF16),
                        pltpu.VMEM((IDX_HEADS, TQ, LANES), BF16),
                        pltpu.VMEM((N_HEADS * V_ROWS, TQ), F32),
                        small,
                        small,
                        pltpu.VMEM((N_HEADS, KT, TQ), F32),
                        pltpu.VMEM((N_HEADS, KT, TQ), F32),
                        small,
                        small],
        compiler_params=pltpu.CompilerParams(
            dimension_semantics=("arbitrary", "arbitrary"), vmem_limit_bytes=VMEM_LIMIT),
        name="dsa",
    )(aq, iq, iwt, az, ik2, ak, avt, bias)


def _merge_kernel(final, x_ref, pu_ref, halo_ref, pz_ref, mq_ref, mz_ref, ya_ref, gates_ref,
                  mk_ref, mv_ref, pw_ref, ps_ref, wb_ref, wo_ref, fg_ref, o_ref):
    i = pl.program_id(1)

    u = pu_ref[0]
    halo = jnp.where(i == 0, 0.0, halo_ref[0])
    ext = jnp.concatenate([halo, u], axis=0)
    t_glob = i * TM + lax.broadcasted_iota(I32, (TM, POOL_GC), 0)
    mixed = []
    for g in range(POOL_GROUPS):
        sl = slice(POOL_GC * g, POOL_GC * (g + 1))
        wsum = ext[:, sl]
        for step in range(g + 1):
            wsum = wsum + pltpu.roll(wsum, 2 ** step, axis=0)
        cnt = jnp.minimum(t_glob + 1, POOL_WINDOWS[g]).astype(F32)
        pooled = wsum[HALO:, :] / cnt - u[:, sl]
        mixed.append(jnp.dot(pooled.astype(BF16), pw_ref[g], preferred_element_type=F32))
    y_pool = jnp.concatenate(mixed, axis=1) * ps_ref[...] * jax.nn.silu(pz_ref[0])

    mem_out = []
    for h in range(MEM_HEADS):
        sl = slice(MEM_HEAD_DIM * h, MEM_HEAD_DIM * (h + 1))
        logits = lax.dot_general(mq_ref[0, :, sl], mk_ref[0, :, sl], _NT_DIMS,
                                 preferred_element_type=F32) * MEM_SCALE
        e = jnp.exp(logits - logits.max(axis=-1, keepdims=True))
        p = e / e.sum(axis=-1, keepdims=True)
        mem_out.append(jnp.dot(p.astype(BF16), mv_ref[0, :, sl], preferred_element_type=F32))
    y_mem = jnp.concatenate(mem_out, axis=1) * jax.nn.silu(mz_ref[0])

    branches = (y_pool.astype(BF16), ya_ref[0], y_mem.astype(BF16))
    merged = jnp.zeros((TM, D_MODEL), F32)
    for br in range(N_BRANCH):
        gate = jax.nn.sigmoid(gates_ref[0, :, D_MODEL * br:D_MODEL * (br + 1)])
        merged = merged + gate * jnp.dot(branches[br], wb_ref[br], preferred_element_type=F32)
    out = x_ref[0] + jnp.dot(merged.astype(BF16), wo_ref[...], preferred_element_type=F32)
    if final:
        out = _rms_norm_f32(out, fg_ref[...])
    o_ref[0] = out


def _merge(final, x, pu, pz, mq, mz, ya, gates, mk, mv, pw, ps, wb, wo, fg):
    b, s, d = x.shape
    row = lambda w: pl.BlockSpec((1, TM, w), lambda bi, i: (bi, i, 0))
    full = lambda a: pl.BlockSpec(a.shape, lambda bi, i: (0,) * a.ndim)
    per_b = lambda a: pl.BlockSpec((1,) + a.shape[1:], lambda bi, i: (bi,) + (0,) * (a.ndim - 1))
    halo = pl.BlockSpec((1, HALO, POOL_WIDTH),
                        lambda bi, i: (bi, jnp.maximum(i * (TM // HALO) - 1, 0), 0))
    return pl.pallas_call(
        functools.partial(_merge_kernel, final),
        grid=(b, s // TM),
        in_specs=[row(d), row(POOL_WIDTH), halo, row(POOL_WIDTH), row(MEM_WIDTH), row(MEM_WIDTH),
                  row(ATTN_WIDTH), row(N_BRANCH * D_MODEL), per_b(mk), per_b(mv),
                  full(pw), full(ps), full(wb), full(wo), full(fg)],
        out_specs=row(d),
        out_shape=jax.ShapeDtypeStruct((b, s, d), F32),
        compiler_params=pltpu.CompilerParams(
            dimension_semantics=("arbitrary", "arbitrary"), vmem_limit_bytes=VMEM_LIMIT),
        name="merge_final" if final else "merge",
    )(x, pu, pu, pz, mq, mz, ya, gates, mk, mv, pw, ps, wb, wo, fg)


def _split_w_in(w):
    col = lambda lo, hi: w[..., lo:hi]
    wf = jnp.concatenate([col(O_PU, O_PZ), col(O_PZ, O_AQ), col(O_AZ, O_IQ), col(O_MZ, O_G),
                          col(O_G, O_END)], axis=-1)
    wb = jnp.concatenate([col(O_AQ, O_AK) * (ATTN_SCALE * LOG2E), col(O_AK, O_AV), col(O_IQ, O_IK),
                          col(O_MQ, O_MZ), col(O_IK, O_IW), col(O_IK, O_IW)], axis=-1)
    pad = jnp.zeros(w.shape[:-1] + (WT_ROWS - ATTN_WIDTH - IDX_HEADS,), w.dtype)
    wt = jnp.swapaxes(jnp.concatenate([col(O_AV, O_AZ), col(O_IW, O_MQ), pad], axis=-1), -1, -2)
    return wf.astype(BF16), wb.astype(BF16), wt.astype(BF16)


def kernel(x, mem, norm_g, w_in, pool_w, pool_scale, mem_norm_g, w_mem_kv, w_branch, w_out,
           rel_bias, final_g):
    depth = w_in.shape[0]
    assert x.shape[1] % COUNT_ROWS == 0 and x.shape[2] == D_MODEL and mem.shape[1] == MEM_LEN
    assert w_in.shape[2] == O_END and TQ == KT and PTM % KT == 0
    assert x.shape[1] // TOP_ROWS <= 256
    bias = _bias_tiles(rel_bias)
    fg = final_g.reshape(1, D_MODEL)
    wf, wb, wt = _split_w_in(w_in)
    w_mem_kv, pool_w, w_branch, w_out = (a.astype(BF16) for a in (w_mem_kv, pool_w, w_branch, w_out))
    for l in range(depth):
        (pu, pz, az, mz, gates, aq, ak, iq, mq, ik2, avt, iwt) = _project(
            x, norm_g[l].reshape(1, D_MODEL), wf[l], wb[l], wt[l])
        mk, mv = _mem_kv(mem, mem_norm_g[l].reshape(1, D_MODEL), w_mem_kv[l])
        ya = _dsa(aq, iq, iwt, az, ik2, ak, avt, bias)
        x = _merge(l == depth - 1, x, pu, pz, mq, mz, ya, gates, mk, mv,
                   pool_w[l], pool_scale[l].reshape(1, POOL_WIDTH), w_branch[l], w_out[l], fg)
    return x
```

```python
import functools
import math

import numpy as np
import jax
import jax.numpy as jnp
from jax import lax
from jax.experimental import pallas as pl
from jax.experimental.pallas import tpu as pltpu

F32 = jnp.float32
BF16 = jnp.bfloat16
I32 = jnp.int32

D_MODEL = 1024
CHUNK = 64
MEM_LEN = 256
POOL_GROUPS = 4
POOL_WINDOWS = (2, 4, 8, 16)
POOL_WIDTH = D_MODEL // 2
POOL_GC = POOL_WIDTH // POOL_GROUPS
N_HEADS = 8
HEAD_DIM = 64
ATTN_WIDTH = N_HEADS * HEAD_DIM
IDX_HEADS = 8
IDX_DIM = 64
TOPK = 256
ATTN_SCALE = HEAD_DIM ** -0.5
LOG2E = math.log2(math.e)
IDX_W_SCALE = (IDX_DIM ** -0.5) * (IDX_HEADS ** -0.5)
MEM_HEADS = 4
MEM_HEAD_DIM = 128
MEM_WIDTH = MEM_HEADS * MEM_HEAD_DIM
MEM_SCALE = MEM_HEAD_DIM ** -0.5
N_BRANCH = 3
REL_BUCKETS = 32
REL_MAX_DIST = 1024
EPS = 1e-6
NEG = -1e30
HALF_NEG = -5e29

IN_SPLITS = (POOL_WIDTH, POOL_WIDTH, ATTN_WIDTH, ATTN_WIDTH, ATTN_WIDTH, ATTN_WIDTH,
             IDX_HEADS * IDX_DIM, IDX_DIM, IDX_HEADS, MEM_WIDTH, MEM_WIDTH, N_BRANCH * D_MODEL)
_OFF = np.concatenate([[0], np.cumsum(IN_SPLITS)]).tolist()
(O_PU, O_PZ, O_AQ, O_AK, O_AV, O_AZ, O_IQ, O_IK, O_IW, O_MQ, O_MZ, O_G, O_END) = _OFF

LANES = 128
SUBLANES = 8
TQ = 256
KT = 256
TM = 512
PTM = 512
HALO = 16
N_NEAR = 4
COUNT_ROWS = 2 * KT
ACC_ROWS = 4 * SUBLANES
TOP_ROWS = 8 * SUBLANES
BANK_ROWS = 16
GROUP_ROWS = 128
CROWS = COUNT_ROWS // GROUP_ROWS * SUBLANES
WT_ROWS = 528
V_ROWS = HEAD_DIM + 16
VMEM_LIMIT = 56 * 1024 * 1024

_NT_DIMS = (((1,), (1,)), ((), ()))


def _rms_norm_f32(x, g):
    ms = jnp.mean(x * x, axis=-1, keepdims=True)
    return x * lax.rsqrt(ms + EPS) * g


def _proj_kernel(x_ref, g_ref, wf_ref, wb_ref, wt_ref,
                 pu_ref, pz_ref, az_ref, mz_ref, gates_ref,
                 aq_ref, ak_ref, iq_ref, mq_ref, ik2_ref, avt_ref, iwt_ref):
    h = _rms_norm_f32(x_ref[0], g_ref[...]).astype(BF16)

    def mm(w_ref, lo, hi):
        return jnp.dot(h, w_ref[:, lo:hi], preferred_element_type=F32)

    pu_ref[0] = mm(wf_ref, 0, 512)
    pz_ref[0] = mm(wf_ref, 512, 1024)
    az_ref[0] = mm(wf_ref, 1024, 1536)
    mz_ref[0] = mm(wf_ref, 1536, 2048)
    for c in range(N_BRANCH * D_MODEL // 512):
        gates_ref[0, :, 512 * c:512 * (c + 1)] = mm(wf_ref, 2048 + 512 * c, 2048 + 512 * (c + 1))
    aq_ref[0] = mm(wb_ref, 0, 512).astype(BF16)
    ak_ref[0] = mm(wb_ref, 512, 1024).astype(BF16)
    iq_ref[0] = mm(wb_ref, 1024, 1536).astype(BF16)
    mq_ref[0] = mm(wb_ref, 1536, 2048).astype(BF16)
    ik2_ref[0] = mm(wb_ref, 2048, 2176).astype(BF16)
    t = lax.dot_general(wt_ref[...], h, _NT_DIMS, preferred_element_type=F32)
    pad_row = lax.broadcasted_iota(I32, (V_ROWS - HEAD_DIM, KT), 0)
    pad = jnp.where(pad_row == 0, 1.0, 0.0).astype(BF16)
    for kt in range(PTM // KT):
        for hd in range(N_HEADS):
            avt_ref[0, kt, V_ROWS * hd:V_ROWS * hd + HEAD_DIM, :] = (
                t[HEAD_DIM * hd:HEAD_DIM * (hd + 1), KT * kt:KT * (kt + 1)].astype(BF16))
            avt_ref[0, kt, V_ROWS * hd + HEAD_DIM:V_ROWS * (hd + 1), :] = pad
    iwt_ref[0] = t[ATTN_WIDTH:ATTN_WIDTH + IDX_HEADS] * IDX_W_SCALE


def _project(x, g, wf, wb, wt):
    b, s, d = x.shape
    row = lambda w: pl.BlockSpec((1, PTM, w), lambda bi, i: (bi, i, 0))
    full = lambda a: pl.BlockSpec(a.shape, lambda bi, i: (0,) * a.ndim,
                                  pipeline_mode=pl.Buffered(1))
    f32o = lambda w: jax.ShapeDtypeStruct((b, s, w), F32)
    b16o = lambda w: jax.ShapeDtypeStruct((b, s, w), BF16)
    out_shape = (f32o(512), f32o(512), f32o(512), f32o(512), f32o(N_BRANCH * D_MODEL),
                 b16o(512), b16o(512), b16o(512), b16o(512), b16o(128),
                 jax.ShapeDtypeStruct((b, s // KT, N_HEADS * V_ROWS, KT), BF16),
                 jax.ShapeDtypeStruct((b, IDX_HEADS, s), F32))
    out_specs = (row(512), row(512), row(512), row(512), row(N_BRANCH * D_MODEL),
                 row(512), row(512), row(512), row(512), row(128),
                 pl.BlockSpec((1, PTM // KT, N_HEADS * V_ROWS, KT), lambda bi, i: (bi, i, 0, 0)),
                 pl.BlockSpec((1, IDX_HEADS, PTM), lambda bi, i: (bi, 0, i)))
    return pl.pallas_call(
        _proj_kernel,
        grid=(b, s // PTM),
        in_specs=[row(d), full(g), full(wf), full(wb), full(wt)],
        out_specs=out_specs,
        out_shape=out_shape,
        compiler_params=pltpu.CompilerParams(
            dimension_semantics=("arbitrary", "arbitrary"), vmem_limit_bytes=VMEM_LIMIT),
        name="proj",
    )(x, g, wf, wb, wt)


def _memkv_kernel(mem_ref, g_ref, w_ref, mk_ref, mv_ref):
    h = _rms_norm_f32(mem_ref[0], g_ref[...]).astype(BF16)
    mk_ref[0] = jnp.dot(h, w_ref[:, 0:MEM_WIDTH], preferred_element_type=F32).astype(BF16)
    mv_ref[0] = jnp.dot(h, w_ref[:, MEM_WIDTH:2 * MEM_WIDTH], preferred_element_type=F32).astype(BF16)


def _mem_kv(mem, g, w):
    b, m, d = mem.shape
    o = jax.ShapeDtypeStruct((b, m, MEM_WIDTH), BF16)
    return pl.pallas_call(
        _memkv_kernel,
        grid=(b,),
        in_specs=[pl.BlockSpec((1, m, d), lambda bi: (bi, 0, 0)),
                  pl.BlockSpec(g.shape, lambda bi: (0, 0)),
                  pl.BlockSpec(w.shape, lambda bi: (0, 0))],
        out_specs=(pl.BlockSpec((1, m, MEM_WIDTH), lambda bi: (bi, 0, 0)),) * 2,
        out_shape=(o, o),
        compiler_params=pltpu.CompilerParams(
            dimension_semantics=("arbitrary",), vmem_limit_bytes=VMEM_LIMIT),
        name="memkv",
    )(mem, g, w)


def _bucket_table():
    half = REL_BUCKETS // 2
    max_exact = half // 2
    delta = np.arange(N_NEAR + 1, dtype=np.int64)[:, None, None]
    key = np.arange(KT, dtype=np.int64)[None, :, None]
    qry = np.arange(TQ, dtype=np.int64)[None, None, :]
    rel = key - qry - delta * KT
    ret = np.where(rel > 0, half, 0)
    n = np.abs(rel)
    nf = np.maximum(n, 1).astype(np.float32)
    large = max_exact + (np.log(nf / np.float32(max_exact))
                         / np.float32(math.log(REL_MAX_DIST / max_exact))
                         * np.float32(half - max_exact)).astype(np.int32)
    large = np.minimum(large, half - 1)
    return (ret + np.where(n < max_exact, n, large)).astype(np.int32)


FAR_BUCKET = REL_BUCKETS // 2 - 1


def _bias_kernel(rb_ref, bk_ref, o_ref):
    h = pl.program_id(1)
    bk = bk_ref[0]
    far = rb_ref[FAR_BUCKET, h]
    acc = jnp.zeros((KT, TQ), F32)
    for bkt in range(REL_BUCKETS):
        acc = jnp.where(bk == bkt, (rb_ref[bkt, h] - far) * LOG2E, acc)
    o_ref[0, 0] = acc


def _bias_tiles(rel_bias):
    table = jnp.asarray(_bucket_table())
    return pl.pallas_call(
        _bias_kernel,
        grid=(N_NEAR + 1, N_HEADS),
        in_specs=[pl.BlockSpec(memory_space=pltpu.SMEM),
                  pl.BlockSpec((1, KT, TQ), lambda dl, h: (dl, 0, 0))],
        out_specs=pl.BlockSpec((1, 1, KT, TQ), lambda dl, h: (dl, h, 0, 0)),
        out_shape=jax.ShapeDtypeStruct((N_NEAR + 1, N_HEADS, KT, TQ), F32),
        compiler_params=pltpu.CompilerParams(dimension_semantics=("arbitrary", "arbitrary")),
        name="bias_tiles",
    )(rel_bias, table)


def _dsa_kernel(q_ref, iq_ref, iwt_ref, az_ref, ik2_ref, k_ref, vt_ref, bias_ref, o_ref,
                sc_ref, hi_ref, c1_ref, c2_ref, tri_ref, cge_buf, cgt_buf, qw_ref, iqw_ref, acc_ref, m_buf, thr_buf,
                stage0_ref, stage1_ref, tmax0_buf, tmax1_buf):
    i = pl.program_id(1)
    n_tiles = i + 1
    cge_ref, cgt_ref, thr_ref = cge_buf.at[0:1], cgt_buf.at[0:1], thr_buf.at[0:1]
    m_ref, tmax0_ref, tmax1_ref = (r.at[0:N_HEADS] for r in (m_buf, tmax0_buf, tmax1_buf))

    lane = lax.broadcasted_iota(I32, (TQ, LANES), 1)
    keep_lo = jnp.where(lane < HEAD_DIM, 1.0, 0.0).astype(BF16)
    keep_hi = jnp.where(lane >= HEAD_DIM, 1.0, 0.0).astype(BF16)
    for h in range(N_HEADS):
        j = h // 2
        keep = keep_hi if (h % 2) else keep_lo
        qw_ref[h] = q_ref[0, :, LANES * j:LANES * (j + 1)] * keep
        iqw_ref[h] = iq_ref[0, :, LANES * j:LANES * (j + 1)] * keep

    key_l = lax.broadcasted_iota(I32, (KT, TQ), 0)
    qry_l = lax.broadcasted_iota(I32, (KT, TQ), 1)
    adm_diag = key_l < ((qry_l // CHUNK) + 1) * CHUNK

    def score_tile(t):
        r0 = pl.multiple_of(t * KT, KT)
        ikt = ik2_ref[0, pl.ds(r0, KT), :]
        acc = jnp.zeros((KT, TQ), F32)
        for h in range(IDX_HEADS):
            y = lax.dot_general(ikt, iqw_ref[h], _NT_DIMS, preferred_element_type=F32)
            acc = acc + jnp.maximum(y, 0.0) * iwt_ref[0, h:h + 1, :]
        adm = jnp.logical_or(jnp.logical_and(adm_diag, t == i), t < i)
        val = jnp.where(adm, acc, NEG)
        sc_ref[pl.ds(r0, KT), :] = val
        top = lax.bitcast_convert_type(val, I32) & I32(-(1 << 16))
        hi_ref[pl.ds(r0, KT), :] = lax.bitcast_convert_type(top, F32).astype(BF16)

    n_steps = (n_tiles + 1) // 2

    def score_step(c, carry):
        score_tile(2 * c)
        score_tile(2 * c + 1)
        return carry

    def score_two_steps(c2, carry):
        score_step(2 * c2, carry)
        return score_step(2 * c2 + 1, carry)

    lax.fori_loop(0, n_steps // 2, score_two_steps, 0)
    lax.fori_loop(n_steps // 2 * 2, n_steps, score_step, 0)

    def count(pred):
        def body(c, acc):
            for k in range(COUNT_ROWS // ACC_ROWS):
                r0 = pl.multiple_of(c * COUNT_ROWS + k * ACC_ROWS, ACC_ROWS)
                x = sc_ref[pl.ds(r0, ACC_ROWS), :]
                acc = jnp.where(pred(x, r0), acc + 1.0, acc)
            return acc
        acc = lax.fori_loop(0, n_steps, body, jnp.zeros((ACC_ROWS, TQ), F32))
        return acc.sum(axis=0, keepdims=True)

    def to_float(u):
        bits = jnp.where(u < 0, u ^ I32(-2 ** 31), ~u)
        return lax.bitcast_convert_type(bits, F32)

    @pl.when(i == 0)
    def _():
        thr_ref[...] = jnp.full((1, TQ), HALF_NEG, F32)

    def count_top(cand_bf16):
        cb = jnp.broadcast_to(cand_bf16, (TOP_ROWS, TQ))

        def body(c, acc):
            for k in range(COUNT_ROWS // TOP_ROWS):
                r0 = pl.multiple_of(c * COUNT_ROWS + k * TOP_ROWS, TOP_ROWS)
                acc = jnp.where(hi_ref[pl.ds(r0, TOP_ROWS), :] >= cb, acc + 1.0, acc)
            return acc
        acc = lax.fori_loop(0, n_steps, body, jnp.zeros((TOP_ROWS, TQ), BF16))
        return acc.astype(F32).sum(axis=0, keepdims=True)

    @pl.when(i > 0)
    def _():
        def top_body(bi, u):
            cand = u | lax.shift_left(I32(1 << 16), I32(15) - bi)
            top = lax.bitcast_convert_type(to_float(cand), I32) & I32(-(1 << 16))
            cnt = count_top(lax.bitcast_convert_type(top, F32).astype(BF16))
            return jnp.where(cnt >= float(TOPK), cand, u)

        u_top = lax.fori_loop(0, 16, top_body, jnp.zeros((1, TQ), I32))

        f_lo = to_float(u_top)
        f_hi = to_float(u_top + I32(1 << 16))
        f_hi_b = jnp.broadcast_to(f_hi, (SUBLANES, TQ))

        def count_both(c, accs):
            a_lo, a_hi = accs
            for k in range(COUNT_ROWS // ACC_ROWS):
                r0 = pl.multiple_of(c * COUNT_ROWS + k * ACC_ROWS, ACC_ROWS)
                x = sc_ref[pl.ds(r0, ACC_ROWS), :]
                a_lo = jnp.where(x >= f_lo, a_lo + 1.0, a_lo)
                a_hi = jnp.where(x >= f_hi, a_hi + 1.0, a_hi)
            return a_lo, a_hi

        zeros = jnp.zeros((ACC_ROWS, TQ), F32)
        cnt_top, above = (a.sum(axis=0, keepdims=True)
                          for a in lax.fori_loop(0, n_steps, count_both, (zeros, zeros)))
        bracket_bad = jnp.max(jnp.where(cnt_top >= float(TOPK), 0.0, 1.0)
                              + jnp.where(above < float(TOPK), 0.0, 1.0))
        carry = (u_top, cnt_top)

        def compact_step(c, carry):
            for g in range(COUNT_ROWS // GROUP_ROWS):
                m1 = jnp.full((SUBLANES, TQ), NEG, F32)
                m2 = m1
                for k in range(GROUP_ROWS // SUBLANES):
                    r0 = pl.multiple_of(c * COUNT_ROWS + g * GROUP_ROWS + k * SUBLANES, SUBLANES)
                    v = sc_ref[pl.ds(r0, SUBLANES), :]
                    v = jnp.where(v >= f_hi_b, NEG, v)
                    m2 = jnp.maximum(m2, jnp.minimum(m1, v))
                    m1 = jnp.maximum(m1, v)
                o0 = pl.multiple_of((c * (COUNT_ROWS // GROUP_ROWS) + g) * SUBLANES, SUBLANES)
                c1_ref[pl.ds(o0, SUBLANES), :] = m1
                c2_ref[pl.ds(o0, SUBLANES), :] = m2
            return carry

        lax.fori_loop(0, n_steps, compact_step, 0)

        def count_compact(cf, cmp=jnp.greater_equal):
            def body(c, acc):
                r0 = pl.multiple_of(c * CROWS, CROWS)
                acc = jnp.where(cmp(c1_ref[pl.ds(r0, CROWS), :], cf), acc + 1.0, acc)
                return jnp.where(cmp(c2_ref[pl.ds(r0, CROWS), :], cf), acc + 1.0, acc)
            acc = lax.fori_loop(0, n_steps, body, jnp.zeros((CROWS, TQ), F32))
            return acc.sum(axis=0, keepdims=True)

        captured = count_compact(f_lo)
        missed = jnp.max(jnp.abs(captured - (cnt_top - above)))

        @pl.when(jnp.logical_and(missed == 0.0, bracket_bad == 0.0))
        def _():
            def cbit_body(bi, carry):
                u, cnt_u = carry
                cand = u | lax.shift_left(I32(1), I32(15) - bi)
                cnt = above + count_compact(to_float(cand))
                keep = cnt >= float(TOPK)
                return jnp.where(keep, cand, u), jnp.where(keep, cnt, cnt_u)

            u, cnt = lax.fori_loop(0, 16, cbit_body, carry)
            thr_ref[...] = to_float(u)
            cge_ref[...] = cnt
            cgt_ref[...] = above + count_compact(to_float(u), jnp.greater)

        def full_search(first_bit, n_bits, start):
            def body(bi, carry):
                u, cnt_u = carry
                cand = u | lax.shift_left(I32(1), I32(first_bit) - bi)
                cf = to_float(cand)
                cnt = count(lambda x, r0: x >= cf)
                keep = cnt >= float(TOPK)
                return jnp.where(keep, cand, u), jnp.where(keep, cnt, cnt_u)

            u, cnt = lax.fori_loop(0, n_bits, body, start)
            thr = to_float(u)
            thr_ref[...] = thr
            cge_ref[...] = cnt
            cgt_ref[...] = count(lambda x, r0: x > thr)

        @pl.when(jnp.logical_and(missed != 0.0, bracket_bad == 0.0))
        def _():
            full_search(15, 16, carry)

        @pl.when(bracket_bad != 0.0)
        def _():
            full_search(31, 32, (jnp.zeros((1, TQ), I32), jnp.zeros((1, TQ), F32)))

        thr = thr_ref[...]

        @pl.when(jnp.max(cge_ref[...]) > float(TOPK))
        def _():
            need = float(TOPK) - cgt_ref[...]
            tri_r = lax.broadcasted_iota(I32, (KT, KT), 0)
            tri_c = lax.broadcasted_iota(I32, (KT, KT), 1)
            tri_ref[...] = jnp.where(tri_r >= tri_c, 1.0, 0.0).astype(BF16)

            def drop_step(c, seen):
                rows = [pl.multiple_of((2 * c + k) * KT, KT) for k in range(2)]
                xs = [sc_ref[pl.ds(r0, KT), :] for r0 in rows]
                es = [jnp.where(x == thr, 1.0, 0.0) for x in xs]
                uptos = [jnp.dot(tri_ref[...], e.astype(BF16), preferred_element_type=F32)
                         for e in es]
                for r0, x, e, upto in zip(rows, xs, es, uptos):
                    gone = jnp.logical_and(e > 0.0, upto - e + seen >= need)
                    sc_ref[pl.ds(r0, KT), :] = jnp.where(gone, NEG, x)
                    seen = seen + upto[KT - 1:KT, :]
                return seen

            lax.fori_loop(0, n_steps, drop_step, jnp.zeros((1, TQ), F32))

    m_ref[...] = jnp.full(m_ref.shape, NEG, F32)
    acc_ref[...] = jnp.zeros(acc_ref.shape, F32)
    thr = thr_ref[...]

    def to_mask(c, carry):
        r0 = pl.multiple_of(c * KT, KT)
        sc_ref[pl.ds(r0, KT), :] = jnp.where(sc_ref[pl.ds(r0, KT), :] >= thr, F32(0.0), F32(NEG))
        return carry

    lax.fori_loop(0, n_tiles, to_mask, 0)

    stages = ((stage0_ref, tmax0_ref), (stage1_ref, tmax1_ref))

    def head_a(t, slot, h):
        stage_ref, tmax_ref = stages[slot]
        r0 = pl.multiple_of(t * KT, KT)
        dl = jnp.minimum(i - t, N_NEAR)
        j = h // 2
        kk = k_ref[0, pl.ds(r0, KT), LANES * j:LANES * (j + 1)]
        s = lax.dot_general(kk, qw_ref[h], _NT_DIMS, preferred_element_type=F32)
        s = s + sc_ref[pl.ds(r0, KT), :] + bias_ref[dl, h]
        stage_ref[h] = s
        tmax_ref[h:h + 1, :] = s.reshape(KT // ACC_ROWS, ACC_ROWS, TQ).max(axis=0).max(
            axis=0, keepdims=True)

    def begin_b(slot):
        m_old = m_ref[...]
        m_new = jnp.maximum(m_old, stages[slot][1][...])
        m_ref[...] = m_new
        return m_new, jnp.exp2(m_old - m_new)

    def head_b(t, slot, h, m_new, alpha):
        p = jnp.exp2(stages[slot][0][h] - m_new[h:h + 1, :])
        pv = jnp.dot(vt_ref[0, t, V_ROWS * h:V_ROWS * (h + 1), :], p.astype(BF16),
                     preferred_element_type=F32)
        acc_ref[V_ROWS * h:V_ROWS * (h + 1), :] = (
            alpha[h:h + 1, :] * acc_ref[V_ROWS * h:V_ROWS * (h + 1), :] + pv)

    def pass_a(t, slot):
        for h in range(N_HEADS):
            head_a(t, slot, h)

    def pass_b(t, slot):
        m_new, alpha = begin_b(slot)
        for h in range(N_HEADS):
            head_b(t, slot, h, m_new, alpha)

    pass_a(0, 0)
    n_pairs = (n_tiles - 1) // 2

    def attn_pair(pr, carry):
        t = 2 * pr
        pass_a(t + 1, 1)
        pass_b(t, 0)
        pass_a(t + 2, 0)
        pass_b(t + 1, 1)
        return carry

    lax.fori_loop(0, n_pairs, attn_pair, 0)
    t_last = 2 * n_pairs

    @pl.when(n_tiles - 1 > t_last)
    def _():
        pass_a(t_last + 1, 1)
        pass_b(t_last, 0)
        pass_b(t_last + 1, 1)

    @pl.when(n_tiles - 1 == t_last)
    def _():
        pass_b(t_last, 0)

    heads = []
    for h in range(N_HEADS):
        denom = acc_ref[V_ROWS * h + HEAD_DIM:V_ROWS * h + HEAD_DIM + 1, :]
        heads.append(acc_ref[V_ROWS * h:V_ROWS * h + HEAD_DIM, :] * (1.0 / denom))
    y = jnp.concatenate(heads, axis=0).T
    o_ref[0] = (y * jax.nn.silu(az_ref[0])).astype(BF16)


def _dsa(aq, iq, iwt, az, ik2, ak, avt, bias):
    b, s, _ = aq.shape
    nq = s // TQ
    qrow = lambda w: pl.BlockSpec((1, TQ, w), lambda bi, i: (bi, i, 0))
    once = pl.Buffered(1)
    small = pltpu.VMEM((BANK_ROWS, TQ), F32)
    return pl.pallas_call(
        _dsa_kernel,
        grid=(b, nq),
        in_specs=[qrow(ATTN_WIDTH), qrow(IDX_HEADS * IDX_DIM),
                  pl.BlockSpec((1, IDX_HEADS, TQ), lambda bi, i: (bi, 0, i)),
                  qrow(ATTN_WIDTH),
                  pl.BlockSpec((1, s, LANES), lambda bi, i: (bi, 0, 0), pipeline_mode=once),
                  pl.BlockSpec((1, s, ATTN_WIDTH), lambda bi, i: (bi, 0, 0), pipeline_mode=once),
                  pl.BlockSpec((1, s // KT, N_HEADS * V_ROWS, KT), lambda bi, i: (bi, 0, 0, 0),
                               pipeline_mode=once),
                  pl.BlockSpec(bias.shape, lambda bi, i: (0, 0, 0, 0), pipeline_mode=once)],
        out_specs=qrow(ATTN_WIDTH),
        out_shape=jax.ShapeDtypeStruct((b, s, ATTN_WIDTH), BF16),
        scratch_shapes=[pltpu.VMEM((s, TQ), F32),
                        pltpu.VMEM((s, TQ), BF16),
                        pltpu.VMEM((s // GROUP_ROWS * SUBLANES, TQ), F32),
                        pltpu.VMEM((s // GROUP_ROWS * SUBLANES, TQ), F32),
                        pltpu.VMEM((KT, KT), BF16),
                        small,
                        small,
                        pltpu.VMEM((N_HEADS, TQ, LANES), BF16),
                        pltpu.VMEM((IDX_HEADS, TQ, LANES), BF16),
                        pltpu.VMEM((N_HEADS * V_ROWS, TQ), F32),
                        small,
                        small,
                        pltpu.VMEM((N_HEADS, KT, TQ), F32),
                        pltpu.VMEM((N_HEADS, KT, TQ), F32),
                        small,
                        small],
        compiler_params=pltpu.CompilerParams(
            dimension_semantics=("arbitrary", "arbitrary"), vmem_limit_bytes=VMEM_LIMIT),
        name="dsa",
    )(aq, iq, iwt, az, ik2, ak, avt, bias)


def _merge_kernel(final, x_ref, pu_ref, halo_ref, pz_ref, mq_ref, mz_ref, ya_ref, gates_ref,
                  mk_ref, mv_ref, pw_ref, ps_ref, wb_ref, wo_ref, fg_ref, o_ref):
    i = pl.program_id(1)

    u = pu_ref[0]
    halo = jnp.where(i == 0, 0.0, halo_ref[0])
    ext = jnp.concatenate([halo, u], axis=0)
    t_glob = i * TM + lax.broadcasted_iota(I32, (TM, POOL_GC), 0)
    mixed = []
    for g in range(POOL_GROUPS):
        sl = slice(POOL_GC * g, POOL_GC * (g + 1))
        wsum = ext[:, sl]
        for step in range(g + 1):
            wsum = wsum + pltpu.roll(wsum, 2 ** step, axis=0)
        cnt = jnp.minimum(t_glob + 1, POOL_WINDOWS[g]).astype(F32)
        pooled = wsum[HALO:, :] / cnt - u[:, sl]
        mixed.append(jnp.dot(pooled.astype(BF16), pw_ref[g], preferred_element_type=F32))
    y_pool = jnp.concatenate(mixed, axis=1) * ps_ref[...] * jax.nn.silu(pz_ref[0])

    mem_out = []
    for h in range(MEM_HEADS):
        sl = slice(MEM_HEAD_DIM * h, MEM_HEAD_DIM * (h + 1))
        logits = lax.dot_general(mq_ref[0, :, sl], mk_ref[0, :, sl], _NT_DIMS,
                                 preferred_element_type=F32) * MEM_SCALE
        e = jnp.exp(logits - logits.max(axis=-1, keepdims=True))
        p = e / e.sum(axis=-1, keepdims=True)
        mem_out.append(jnp.dot(p.astype(BF16), mv_ref[0, :, sl], preferred_element_type=F32))
    y_mem = jnp.concatenate(mem_out, axis=1) * jax.nn.silu(mz_ref[0])

    branches = (y_pool.astype(BF16), ya_ref[0], y_mem.astype(BF16))
    merged = jnp.zeros((TM, D_MODEL), F32)
    for br in range(N_BRANCH):
        gate = jax.nn.sigmoid(gates_ref[0, :, D_MODEL * br:D_MODEL * (br + 1)])
        merged = merged + gate * jnp.dot(branches[br], wb_ref[br], preferred_element_type=F32)
    out = x_ref[0] + jnp.dot(merged.astype(BF16), wo_ref[...], preferred_element_type=F32)
    if final:
        out = _rms_norm_f32(out, fg_ref[...])
    o_ref[0] = out


def _merge(final, x, pu, pz, mq, mz, ya, gates, mk, mv, pw, ps, wb, wo, fg):
    b, s, d = x.shape
    row = lambda w: pl.BlockSpec((1, TM, w), lambda bi, i: (bi, i, 0))
    full = lambda a: pl.BlockSpec(a.shape, lambda bi, i: (0,) * a.ndim)
    per_b = lambda a: pl.BlockSpec((1,) + a.shape[1:], lambda bi, i: (bi,) + (0,) * (a.ndim - 1))
    halo = pl.BlockSpec((1, HALO, POOL_WIDTH),
                        lambda bi, i: (bi, jnp.maximum(i * (TM // HALO) - 1, 0), 0))
    return pl.pallas_call(
        functools.partial(_merge_kernel, final),
        grid=(b, s // TM),
        in_specs=[row(d), row(POOL_WIDTH), halo, row(POOL_WIDTH), row(MEM_WIDTH), row(MEM_WIDTH),
                  row(ATTN_WIDTH), row(N_BRANCH * D_MODEL), per_b(mk), per_b(mv),
                  full(pw), full(ps), full(wb), full(wo), full(fg)],
        out_specs=row(d),
        out_shape=jax.ShapeDtypeStruct((b, s, d), F32),
        compiler_params=pltpu.CompilerParams(
            dimension_semantics=("arbitrary", "arbitrary"), vmem_limit_bytes=VMEM_LIMIT),
        name="merge_final" if final else "merge",
    )(x, pu, pu, pz, mq, mz, ya, gates, mk, mv, pw, ps, wb, wo, fg)


def _split_w_in(w):
    col = lambda lo, hi: w[:, lo:hi]
    wf = jnp.concatenate([col(O_PU, O_PZ), col(O_PZ, O_AQ), col(O_AZ, O_IQ), col(O_MZ, O_G),
                          col(O_G, O_END)], axis=1)
    wb = jnp.concatenate([col(O_AQ, O_AK) * (ATTN_SCALE * LOG2E), col(O_AK, O_AV), col(O_IQ, O_IK),
                          col(O_MQ, O_MZ), col(O_IK, O_IW), col(O_IK, O_IW)], axis=1)
    wt = jnp.concatenate([col(O_AV, O_AZ), col(O_IW, O_MQ),
                          jnp.zeros((w.shape[0], WT_ROWS - ATTN_WIDTH - IDX_HEADS), w.dtype)], axis=1).T
    return wf.astype(BF16), wb.astype(BF16), wt.astype(BF16)


def kernel(x, mem, norm_g, w_in, pool_w, pool_scale, mem_norm_g, w_mem_kv, w_branch, w_out,
           rel_bias, final_g):
    depth = w_in.shape[0]
    assert x.shape[1] % COUNT_ROWS == 0 and x.shape[2] == D_MODEL and mem.shape[1] == MEM_LEN
    assert w_in.shape[2] == O_END and TQ == KT and PTM % KT == 0
    assert x.shape[1] // TOP_ROWS <= 256
    bias = _bias_tiles(rel_bias)
    fg = final_g.reshape(1, D_MODEL)
    for l in range(depth):
        wf, wb, wt = _split_w_in(w_in[l])
        (pu, pz, az, mz, gates, aq, ak, iq, mq, ik2, avt, iwt) = _project(
            x, norm_g[l].reshape(1, D_MODEL), wf, wb, wt)
        mk, mv = _mem_kv(mem, mem_norm_g[l].reshape(1, D_MODEL), w_mem_kv[l].astype(BF16))
        ya = _dsa(aq, iq, iwt, az, ik2, ak, avt, bias)
        x = _merge(l == depth - 1, x, pu, pz, mq, mz, ya, gates, mk, mv,
                   pool_w[l].astype(BF16), pool_scale[l].reshape(1, POOL_WIDTH),
                   w_branch[l].astype(BF16), w_out[l].astype(BF16), fg)
    return x
```

```python
import functools
import math

import numpy as np
import jax
import jax.numpy as jnp
from jax import lax
from jax.experimental import pallas as pl
from jax.experimental.pallas import tpu as pltpu

F32 = jnp.float32
BF16 = jnp.bfloat16
I32 = jnp.int32

D_MODEL = 1024
CHUNK = 64
MEM_LEN = 256
POOL_GROUPS = 4
POOL_WINDOWS = (2, 4, 8, 16)
POOL_WIDTH = D_MODEL // 2
POOL_GC = POOL_WIDTH // POOL_GROUPS
N_HEADS = 8
HEAD_DIM = 64
ATTN_WIDTH = N_HEADS * HEAD_DIM
IDX_HEADS = 8
IDX_DIM = 64
TOPK = 256
ATTN_SCALE = HEAD_DIM ** -0.5
LOG2E = math.log2(math.e)
IDX_W_SCALE = (IDX_DIM ** -0.5) * (IDX_HEADS ** -0.5)
MEM_HEADS = 4
MEM_HEAD_DIM = 128
MEM_WIDTH = MEM_HEADS * MEM_HEAD_DIM
MEM_SCALE = MEM_HEAD_DIM ** -0.5
N_BRANCH = 3
REL_BUCKETS = 32
REL_MAX_DIST = 1024
EPS = 1e-6
NEG = -1e30
HALF_NEG = -5e29

IN_SPLITS = (POOL_WIDTH, POOL_WIDTH, ATTN_WIDTH, ATTN_WIDTH, ATTN_WIDTH, ATTN_WIDTH,
             IDX_HEADS * IDX_DIM, IDX_DIM, IDX_HEADS, MEM_WIDTH, MEM_WIDTH, N_BRANCH * D_MODEL)
_OFF = np.concatenate([[0], np.cumsum(IN_SPLITS)]).tolist()
(O_PU, O_PZ, O_AQ, O_AK, O_AV, O_AZ, O_IQ, O_IK, O_IW, O_MQ, O_MZ, O_G, O_END) = _OFF

LANES = 128
SUBLANES = 8
TQ = 256
KT = 256
TM = 512
PTM = 512
HALO = 16
N_NEAR = 4
COUNT_ROWS = 2 * KT
ACC_ROWS = 4 * SUBLANES
TOP_ROWS = 8 * SUBLANES
BANK_ROWS = 16
GROUP_ROWS = 128
CROWS = COUNT_ROWS // GROUP_ROWS * SUBLANES
WT_ROWS = 528
V_ROWS = HEAD_DIM + 16
VMEM_LIMIT = 56 * 1024 * 1024

_NT_DIMS = (((1,), (1,)), ((), ()))


def _rms_norm_f32(x, g):
    ms = jnp.mean(x * x, axis=-1, keepdims=True)
    return x * lax.rsqrt(ms + EPS) * g


def _proj_kernel(x_ref, g_ref, wf_ref, wb_ref, wt_ref,
                 pu_ref, pz_ref, az_ref, mz_ref, gates_ref,
                 aq_ref, ak_ref, iq_ref, mq_ref, ik2_ref, avt_ref, iwt_ref):
    h = _rms_norm_f32(x_ref[0], g_ref[...]).astype(BF16)

    def mm(w_ref, lo, hi):
        return jnp.dot(h, w_ref[:, lo:hi], preferred_element_type=F32)

    pu_ref[0] = mm(wf_ref, 0, 512)
    pz_ref[0] = mm(wf_ref, 512, 1024)
    az_ref[0] = mm(wf_ref, 1024, 1536)
    mz_ref[0] = mm(wf_ref, 1536, 2048)
    for c in range(N_BRANCH * D_MODEL // 512):
        gates_ref[0, :, 512 * c:512 * (c + 1)] = mm(wf_ref, 2048 + 512 * c, 2048 + 512 * (c + 1))
    aq_ref[0] = mm(wb_ref, 0, 512).astype(BF16)
    ak_ref[0] = mm(wb_ref, 512, 1024).astype(BF16)
    iq_ref[0] = mm(wb_ref, 1024, 1536).astype(BF16)
    mq_ref[0] = mm(wb_ref, 1536, 2048).astype(BF16)
    ik2_ref[0] = mm(wb_ref, 2048, 2176).astype(BF16)
    t = lax.dot_general(wt_ref[...], h, _NT_DIMS, preferred_element_type=F32)
    pad_row = lax.broadcasted_iota(I32, (V_ROWS - HEAD_DIM, KT), 0)
    pad = jnp.where(pad_row == 0, 1.0, 0.0).astype(BF16)
    for kt in range(PTM // KT):
        for hd in range(N_HEADS):
            avt_ref[0, kt, V_ROWS * hd:V_ROWS * hd + HEAD_DIM, :] = (
                t[HEAD_DIM * hd:HEAD_DIM * (hd + 1), KT * kt:KT * (kt + 1)].astype(BF16))
            avt_ref[0, kt, V_ROWS * hd + HEAD_DIM:V_ROWS * (hd + 1), :] = pad
    iwt_ref[0] = t[ATTN_WIDTH:ATTN_WIDTH + IDX_HEADS] * IDX_W_SCALE


def _project(x, g, wf, wb, wt):
    b, s, d = x.shape
    row = lambda w: pl.BlockSpec((1, PTM, w), lambda bi, i: (bi, i, 0))
    full = lambda a: pl.BlockSpec(a.shape, lambda bi, i: (0,) * a.ndim,
                                  pipeline_mode=pl.Buffered(1))
    f32o = lambda w: jax.ShapeDtypeStruct((b, s, w), F32)
    b16o = lambda w: jax.ShapeDtypeStruct((b, s, w), BF16)
    out_shape = (f32o(512), f32o(512), f32o(512), f32o(512), f32o(N_BRANCH * D_MODEL),
                 b16o(512), b16o(512), b16o(512), b16o(512), b16o(128),
                 jax.ShapeDtypeStruct((b, s // KT, N_HEADS * V_ROWS, KT), BF16),
                 jax.ShapeDtypeStruct((b, IDX_HEADS, s), F32))
    out_specs = (row(512), row(512), row(512), row(512), row(N_BRANCH * D_MODEL),
                 row(512), row(512), row(512), row(512), row(128),
                 pl.BlockSpec((1, PTM // KT, N_HEADS * V_ROWS, KT), lambda bi, i: (bi, i, 0, 0)),
                 pl.BlockSpec((1, IDX_HEADS, PTM), lambda bi, i: (bi, 0, i)))
    return pl.pallas_call(
        _proj_kernel,
        grid=(b, s // PTM),
        in_specs=[row(d), full(g), full(wf), full(wb), full(wt)],
        out_specs=out_specs,
        out_shape=out_shape,
        compiler_params=pltpu.CompilerParams(
            dimension_semantics=("arbitrary", "arbitrary"), vmem_limit_bytes=VMEM_LIMIT),
        name="proj",
    )(x, g, wf, wb, wt)


def _memkv_kernel(mem_ref, g_ref, w_ref, mk_ref, mv_ref):
    h = _rms_norm_f32(mem_ref[0], g_ref[...]).astype(BF16)
    mk_ref[0] = jnp.dot(h, w_ref[:, 0:MEM_WIDTH], preferred_element_type=F32).astype(BF16)
    mv_ref[0] = jnp.dot(h, w_ref[:, MEM_WIDTH:2 * MEM_WIDTH], preferred_element_type=F32).astype(BF16)


def _mem_kv(mem, g, w):
    b, m, d = mem.shape
    o = jax.ShapeDtypeStruct((b, m, MEM_WIDTH), BF16)
    return pl.pallas_call(
        _memkv_kernel,
        grid=(b,),
        in_specs=[pl.BlockSpec((1, m, d), lambda bi: (bi, 0, 0)),
                  pl.BlockSpec(g.shape, lambda bi: (0, 0)),
                  pl.BlockSpec(w.shape, lambda bi: (0, 0))],
        out_specs=(pl.BlockSpec((1, m, MEM_WIDTH), lambda bi: (bi, 0, 0)),) * 2,
        out_shape=(o, o),
        compiler_params=pltpu.CompilerParams(
            dimension_semantics=("arbitrary",), vmem_limit_bytes=VMEM_LIMIT),
        name="memkv",
    )(mem, g, w)


def _bucket_table():
    half = REL_BUCKETS // 2
    max_exact = half // 2
    delta = np.arange(N_NEAR + 1, dtype=np.int64)[:, None, None]
    key = np.arange(KT, dtype=np.int64)[None, :, None]
    qry = np.arange(TQ, dtype=np.int64)[None, None, :]
    rel = key - qry - delta * KT
    ret = np.where(rel > 0, half, 0)
    n = np.abs(rel)
    nf = np.maximum(n, 1).astype(np.float32)
    large = max_exact + (np.log(nf / np.float32(max_exact))
                         / np.float32(math.log(REL_MAX_DIST / max_exact))
                         * np.float32(half - max_exact)).astype(np.int32)
    large = np.minimum(large, half - 1)
    return (ret + np.where(n < max_exact, n, large)).astype(np.int32)


FAR_BUCKET = REL_BUCKETS // 2 - 1


def _bias_kernel(rb_ref, bk_ref, o_ref):
    h = pl.program_id(1)
    bk = bk_ref[0]
    far = rb_ref[FAR_BUCKET, h]
    acc = jnp.zeros((KT, TQ), F32)
    for bkt in range(REL_BUCKETS):
        acc = jnp.where(bk == bkt, (rb_ref[bkt, h] - far) * LOG2E, acc)
    o_ref[0, 0] = acc


def _bias_tiles(rel_bias):
    table = jnp.asarray(_bucket_table())
    return pl.pallas_call(
        _bias_kernel,
        grid=(N_NEAR + 1, N_HEADS),
        in_specs=[pl.BlockSpec(memory_space=pltpu.SMEM),
                  pl.BlockSpec((1, KT, TQ), lambda dl, h: (dl, 0, 0))],
        out_specs=pl.BlockSpec((1, 1, KT, TQ), lambda dl, h: (dl, h, 0, 0)),
        out_shape=jax.ShapeDtypeStruct((N_NEAR + 1, N_HEADS, KT, TQ), F32),
        compiler_params=pltpu.CompilerParams(dimension_semantics=("arbitrary", "arbitrary")),
        name="bias_tiles",
    )(rel_bias, table)


def _dsa_kernel(q_ref, iq_ref, iwt_ref, az_ref, ik2_ref, k_ref, vt_ref, bias_ref, o_ref,
                sc_ref, hi_ref, c1_ref, c2_ref, tri_ref, cge_buf, cgt_buf, qw_ref, iqw_ref, acc_ref, m_buf, thr_buf,
                stage0_ref, stage1_ref, tmax0_buf, tmax1_buf):
    i = pl.program_id(1)
    n_tiles = i + 1
    cge_ref, cgt_ref, thr_ref = cge_buf.at[0:1], cgt_buf.at[0:1], thr_buf.at[0:1]
    m_ref, tmax0_ref, tmax1_ref = (r.at[0:N_HEADS] for r in (m_buf, tmax0_buf, tmax1_buf))

    lane = lax.broadcasted_iota(I32, (TQ, LANES), 1)
    keep_lo = jnp.where(lane < HEAD_DIM, 1.0, 0.0).astype(BF16)
    keep_hi = jnp.where(lane >= HEAD_DIM, 1.0, 0.0).astype(BF16)
    for h in range(N_HEADS):
        j = h // 2
        keep = keep_hi if (h % 2) else keep_lo
        qw_ref[h] = q_ref[0, :, LANES * j:LANES * (j + 1)] * keep
        iqw_ref[h] = iq_ref[0, :, LANES * j:LANES * (j + 1)] * keep

    key_l = lax.broadcasted_iota(I32, (KT, TQ), 0)
    qry_l = lax.broadcasted_iota(I32, (KT, TQ), 1)
    adm_diag = key_l < ((qry_l // CHUNK) + 1) * CHUNK

    def score_tile(t):
        r0 = pl.multiple_of(t * KT, KT)
        ikt = ik2_ref[0, pl.ds(r0, KT), :]
        acc = jnp.zeros((KT, TQ), F32)
        for h in range(IDX_HEADS):
            y = lax.dot_general(ikt, iqw_ref[h], _NT_DIMS, preferred_element_type=F32)
            acc = acc + jnp.maximum(y, 0.0) * iwt_ref[0, h:h + 1, :]
        adm = jnp.logical_or(jnp.logical_and(adm_diag, t == i), t < i)
        val = jnp.where(adm, acc, NEG)
        sc_ref[pl.ds(r0, KT), :] = val
        top = lax.bitcast_convert_type(val, I32) & I32(-(1 << 16))
        hi_ref[pl.ds(r0, KT), :] = lax.bitcast_convert_type(top, F32).astype(BF16)

    n_steps = (n_tiles + 1) // 2

    def score_step(c, carry):
        score_tile(2 * c)
        score_tile(2 * c + 1)
        return carry

    def score_two_steps(c2, carry):
        score_step(2 * c2, carry)
        return score_step(2 * c2 + 1, carry)

    lax.fori_loop(0, n_steps // 2, score_two_steps, 0)
    lax.fori_loop(n_steps // 2 * 2, n_steps, score_step, 0)

    def count(pred):
        def body(c, acc):
            for k in range(COUNT_ROWS // ACC_ROWS):
                r0 = pl.multiple_of(c * COUNT_ROWS + k * ACC_ROWS, ACC_ROWS)
                x = sc_ref[pl.ds(r0, ACC_ROWS), :]
                acc = jnp.where(pred(x, r0), acc + 1.0, acc)
            return acc
        acc = lax.fori_loop(0, n_steps, body, jnp.zeros((ACC_ROWS, TQ), F32))
        return acc.sum(axis=0, keepdims=True)

    def to_float(u):
        bits = jnp.where(u < 0, u ^ I32(-2 ** 31), ~u)
        return lax.bitcast_convert_type(bits, F32)

    @pl.when(i == 0)
    def _():
        thr_ref[...] = jnp.full((1, TQ), HALF_NEG, F32)

    def count_top(cand_bf16):
        cb = jnp.broadcast_to(cand_bf16, (TOP_ROWS, TQ))

        def body(c, acc):
            for k in range(COUNT_ROWS // TOP_ROWS):
                r0 = pl.multiple_of(c * COUNT_ROWS + k * TOP_ROWS, TOP_ROWS)
                acc = jnp.where(hi_ref[pl.ds(r0, TOP_ROWS), :] >= cb, acc + 1.0, acc)
            return acc
        acc = lax.fori_loop(0, n_steps, body, jnp.zeros((TOP_ROWS, TQ), BF16))
        return acc.astype(F32).sum(axis=0, keepdims=True)

    @pl.when(i > 0)
    def _():
        def top_body(bi, u):
            cand = u | lax.shift_left(I32(1 << 16), I32(15) - bi)
            top = lax.bitcast_convert_type(to_float(cand), I32) & I32(-(1 << 16))
            cnt = count_top(lax.bitcast_convert_type(top, F32).astype(BF16))
            return jnp.where(cnt >= float(TOPK), cand, u)

        u_top = lax.fori_loop(0, 16, top_body, jnp.zeros((1, TQ), I32))

        f_lo = to_float(u_top)
        f_hi = to_float(u_top + I32(1 << 16))
        f_hi_b = jnp.broadcast_to(f_hi, (SUBLANES, TQ))

        def count_both(c, accs):
            a_lo, a_hi = accs
            for k in range(COUNT_ROWS // ACC_ROWS):
                r0 = pl.multiple_of(c * COUNT_ROWS + k * ACC_ROWS, ACC_ROWS)
                x = sc_ref[pl.ds(r0, ACC_ROWS), :]
                a_lo = jnp.where(x >= f_lo, a_lo + 1.0, a_lo)
                a_hi = jnp.where(x >= f_hi, a_hi + 1.0, a_hi)
            return a_lo, a_hi

        zeros = jnp.zeros((ACC_ROWS, TQ), F32)
        cnt_top, above = (a.sum(axis=0, keepdims=True)
                          for a in lax.fori_loop(0, n_steps, count_both, (zeros, zeros)))
        bracket_bad = jnp.max(jnp.where(cnt_top >= float(TOPK), 0.0, 1.0)
                              + jnp.where(above < float(TOPK), 0.0, 1.0))
        carry = (u_top, cnt_top)

        def compact_step(c, carry):
            for g in range(COUNT_ROWS // GROUP_ROWS):
                m1 = jnp.full((SUBLANES, TQ), NEG, F32)
                m2 = m1
                for k in range(GROUP_ROWS // SUBLANES):
                    r0 = pl.multiple_of(c * COUNT_ROWS + g * GROUP_ROWS + k * SUBLANES, SUBLANES)
                    v = sc_ref[pl.ds(r0, SUBLANES), :]
                    v = jnp.where(v >= f_hi_b, NEG, v)
                    m2 = jnp.maximum(m2, jnp.minimum(m1, v))
                    m1 = jnp.maximum(m1, v)
                o0 = pl.multiple_of((c * (COUNT_ROWS // GROUP_ROWS) + g) * SUBLANES, SUBLANES)
                c1_ref[pl.ds(o0, SUBLANES), :] = m1
                c2_ref[pl.ds(o0, SUBLANES), :] = m2
            return carry

        lax.fori_loop(0, n_steps, compact_step, 0)

        def count_compact(cf, cmp=jnp.greater_equal):
            def body(c, acc):
                r0 = pl.multiple_of(c * CROWS, CROWS)
                acc = jnp.where(cmp(c1_ref[pl.ds(r0, CROWS), :], cf), acc + 1.0, acc)
                return jnp.where(cmp(c2_ref[pl.ds(r0, CROWS), :], cf), acc + 1.0, acc)
            acc = lax.fori_loop(0, n_steps, body, jnp.zeros((CROWS, TQ), F32))
            return acc.sum(axis=0, keepdims=True)

        captured = count_compact(f_lo)
        missed = jnp.max(jnp.abs(captured - (cnt_top - above)))

        @pl.when(jnp.logical_and(missed == 0.0, bracket_bad == 0.0))
        def _():
            def cbit_body(bi, carry):
                u, cnt_u = carry
                cand = u | lax.shift_left(I32(1), I32(15) - bi)
                cnt = above + count_compact(to_float(cand))
                keep = cnt >= float(TOPK)
                return jnp.where(keep, cand, u), jnp.where(keep, cnt, cnt_u)

            u, cnt = lax.fori_loop(0, 16, cbit_body, carry)
            thr_ref[...] = to_float(u)
            cge_ref[...] = cnt
            cgt_ref[...] = above + count_compact(to_float(u), jnp.greater)

        def full_search(first_bit, n_bits, start):
            def body(bi, carry):
                u, cnt_u = carry
                cand = u | lax.shift_left(I32(1), I32(first_bit) - bi)
                cf = to_float(cand)
                cnt = count(lambda x, r0: x >= cf)
                keep = cnt >= float(TOPK)
                return jnp.where(keep, cand, u), jnp.where(keep, cnt, cnt_u)

            u, cnt = lax.fori_loop(0, n_bits, body, start)
            thr = to_float(u)
            thr_ref[...] = thr
            cge_ref[...] = cnt
            cgt_ref[...] = count(lambda x, r0: x > thr)

        @pl.when(jnp.logical_and(missed != 0.0, bracket_bad == 0.0))
        def _():
            full_search(15, 16, carry)

        @pl.when(bracket_bad != 0.0)
        def _():
            full_search(31, 32, (jnp.zeros((1, TQ), I32), jnp.zeros((1, TQ), F32)))

        thr = thr_ref[...]

        @pl.when(jnp.max(cge_ref[...]) > float(TOPK))
        def _():
            need = float(TOPK) - cgt_ref[...]
            tri_r = lax.broadcasted_iota(I32, (KT, KT), 0)
            tri_c = lax.broadcasted_iota(I32, (KT, KT), 1)
            tri_ref[...] = jnp.where(tri_r >= tri_c, 1.0, 0.0).astype(BF16)

            def drop_step(c, seen):
                rows = [pl.multiple_of((2 * c + k) * KT, KT) for k in range(2)]
                xs = [sc_ref[pl.ds(r0, KT), :] for r0 in rows]
                es = [jnp.where(x == thr, 1.0, 0.0) for x in xs]
                uptos = [jnp.dot(tri_ref[...], e.astype(BF16), preferred_element_type=F32)
                         for e in es]
                for r0, x, e, upto in zip(rows, xs, es, uptos):
                    gone = jnp.logical_and(e > 0.0, upto - e + seen >= need)
                    sc_ref[pl.ds(r0, KT), :] = jnp.where(gone, NEG, x)
                    seen = seen + upto[KT - 1:KT, :]
                return seen

            lax.fori_loop(0, n_steps, drop_step, jnp.zeros((1, TQ), F32))

    m_ref[...] = jnp.full(m_ref.shape, NEG, F32)
    acc_ref[...] = jnp.zeros(acc_ref.shape, F32)
    thr = thr_ref[...]

    def to_mask(c, carry):
        r0 = pl.multiple_of(c * COUNT_ROWS, COUNT_ROWS)
        x = sc_ref[pl.ds(r0, COUNT_ROWS), :]
        sc_ref[pl.ds(r0, COUNT_ROWS), :] = jnp.where(x >= thr, F32(0.0), F32(NEG))
        return carry

    lax.fori_loop(0, n_steps, to_mask, 0)

    stages = ((stage0_ref, tmax0_ref), (stage1_ref, tmax1_ref))

    def head_a(t, slot, h):
        stage_ref, tmax_ref = stages[slot]
        r0 = pl.multiple_of(t * KT, KT)
        dl = jnp.minimum(i - t, N_NEAR)
        j = h // 2
        kk = k_ref[0, pl.ds(r0, KT), LANES * j:LANES * (j + 1)]
        s = lax.dot_general(kk, qw_ref[h], _NT_DIMS, preferred_element_type=F32)
        s = s + sc_ref[pl.ds(r0, KT), :] + bias_ref[dl, h]
        stage_ref[h] = s
        tmax_ref[h:h + 1, :] = s.reshape(KT // ACC_ROWS, ACC_ROWS, TQ).max(axis=0).max(
            axis=0, keepdims=True)

    def begin_b(slot):
        m_old = m_ref[...]
        m_new = jnp.maximum(m_old, stages[slot][1][...])
        m_ref[...] = m_new
        return m_new, jnp.exp2(m_old - m_new)

    def head_b(t, slot, h, m_new, alpha):
        p = jnp.exp2(stages[slot][0][h] - m_new[h:h + 1, :])
        pv = jnp.dot(vt_ref[0, t, V_ROWS * h:V_ROWS * (h + 1), :], p.astype(BF16),
                     preferred_element_type=F32)
        acc_ref[V_ROWS * h:V_ROWS * (h + 1), :] = (
            alpha[h:h + 1, :] * acc_ref[V_ROWS * h:V_ROWS * (h + 1), :] + pv)

    def pass_a(t, slot):
        for h in range(N_HEADS):
            head_a(t, slot, h)

    def pass_b(t, slot):
        m_new, alpha = begin_b(slot)
        for h in range(N_HEADS):
            head_b(t, slot, h, m_new, alpha)

    pass_a(0, 0)
    n_pairs = (n_tiles - 1) // 2

    def attn_pair(pr, carry):
        t = 2 * pr
        pass_a(t + 1, 1)
        pass_b(t, 0)
        pass_a(t + 2, 0)
        pass_b(t + 1, 1)
        return carry

    lax.fori_loop(0, n_pairs, attn_pair, 0)
    t_last = 2 * n_pairs

    @pl.when(n_tiles - 1 > t_last)
    def _():
        pass_a(t_last + 1, 1)
        pass_b(t_last, 0)
        pass_b(t_last + 1, 1)

    @pl.when(n_tiles - 1 == t_last)
    def _():
        pass_b(t_last, 0)

    heads = []
    for h in range(N_HEADS):
        denom = acc_ref[V_ROWS * h + HEAD_DIM:V_ROWS * h + HEAD_DIM + 1, :]
        heads.append(acc_ref[V_ROWS * h:V_ROWS * h + HEAD_DIM, :] * (1.0 / denom))
    y = jnp.concatenate(heads, axis=0).T
    o_ref[0] = (y * jax.nn.silu(az_ref[0])).astype(BF16)


def _dsa(aq, iq, iwt, az, ik2, ak, avt, bias):
    b, s, _ = aq.shape
    nq = s // TQ
    qrow = lambda w: pl.BlockSpec((1, TQ, w), lambda bi, i: (bi, i, 0))
    once = pl.Buffered(1)
    small = pltpu.VMEM((BANK_ROWS, TQ), F32)
    return pl.pallas_call(
        _dsa_kernel,
        grid=(b, nq),
        in_specs=[qrow(ATTN_WIDTH), qrow(IDX_HEADS * IDX_DIM),
                  pl.BlockSpec((1, IDX_HEADS, TQ), lambda bi, i: (bi, 0, i)),
                  qrow(ATTN_WIDTH),
                  pl.BlockSpec((1, s, LANES), lambda bi, i: (bi, 0, 0), pipeline_mode=once),
                  pl.BlockSpec((1, s, ATTN_WIDTH), lambda bi, i: (bi, 0, 0), pipeline_mode=once),
                  pl.BlockSpec((1, s // KT, N_HEADS * V_ROWS, KT), lambda bi, i: (bi, 0, 0, 0),
                               pipeline_mode=once),
                  pl.BlockSpec(bias.shape, lambda bi, i: (0, 0, 0, 0), pipeline_mode=once)],
        out_specs=qrow(ATTN_WIDTH),
        out_shape=jax.ShapeDtypeStruct((b, s, ATTN_WIDTH), BF16),
        scratch_shapes=[pltpu.VMEM((s, TQ), F32),
                        pltpu.VMEM((s, TQ), BF16),
                        pltpu.VMEM((s // GROUP_ROWS * SUBLANES, TQ), F32),
                        pltpu.VMEM((s // GROUP_ROWS * SUBLANES, TQ), F32),
                        pltpu.VMEM((KT, KT), BF16),
                        small,
                        small,
                        pltpu.VMEM((N_HEADS, TQ, LANES), BF16),
                        pltpu.VMEM((IDX_HEADS, TQ, LANES), BF16),
                        pltpu.VMEM((N_HEADS * V_ROWS, TQ), F32),
                        small,
                        small,
                        pltpu.VMEM((N_HEADS, KT, TQ), F32),
                        pltpu.VMEM((N_HEADS, KT, TQ), F32),
                        small,
                        small],
        compiler_params=pltpu.CompilerParams(
            dimension_semantics=("arbitrary", "arbitrary"), vmem_limit_bytes=VMEM_LIMIT),
        name="dsa",
    )(aq, iq, iwt, az, ik2, ak, avt, bias)


def _merge_kernel(final, x_ref, pu_ref, halo_ref, pz_ref, mq_ref, mz_ref, ya_ref, gates_ref,
                  mk_ref, mv_ref, pw_ref, ps_ref, wb_ref, wo_ref, fg_ref, o_ref):
    i = pl.program_id(1)

    u = pu_ref[0]
    halo = jnp.where(i == 0, 0.0, halo_ref[0])
    ext = jnp.concatenate([halo, u], axis=0)
    t_glob = i * TM + lax.broadcasted_iota(I32, (TM, POOL_GC), 0)
    mixed = []
    for g in range(POOL_GROUPS):
        sl = slice(POOL_GC * g, POOL_GC * (g + 1))
        wsum = ext[:, sl]
        for step in range(g + 1):
            wsum = wsum + pltpu.roll(wsum, 2 ** step, axis=0)
        cnt = jnp.minimum(t_glob + 1, POOL_WINDOWS[g]).astype(F32)
        pooled = wsum[HALO:, :] / cnt - u[:, sl]
        mixed.append(jnp.dot(pooled.astype(BF16), pw_ref[g], preferred_element_type=F32))
    y_pool = jnp.concatenate(mixed, axis=1) * ps_ref[...] * jax.nn.silu(pz_ref[0])

    mem_out = []
    for h in range(MEM_HEADS):
        sl = slice(MEM_HEAD_DIM * h, MEM_HEAD_DIM * (h + 1))
        logits = lax.dot_general(mq_ref[0, :, sl], mk_ref[0, :, sl], _NT_DIMS,
                                 preferred_element_type=F32) * MEM_SCALE
        e = jnp.exp(logits - logits.max(axis=-1, keepdims=True))
        p = e / e.sum(axis=-1, keepdims=True)
        mem_out.append(jnp.dot(p.astype(BF16), mv_ref[0, :, sl], preferred_element_type=F32))
    y_mem = jnp.concatenate(mem_out, axis=1) * jax.nn.silu(mz_ref[0])

    branches = (y_pool.astype(BF16), ya_ref[0], y_mem.astype(BF16))
    merged = jnp.zeros((TM, D_MODEL), F32)
    for br in range(N_BRANCH):
        gate = jax.nn.sigmoid(gates_ref[0, :, D_MODEL * br:D_MODEL * (br + 1)])
        merged = merged + gate * jnp.dot(branches[br], wb_ref[br], preferred_element_type=F32)
    out = x_ref[0] + jnp.dot(merged.astype(BF16), wo_ref[...], preferred_element_type=F32)
    if final:
        out = _rms_norm_f32(out, fg_ref[...])
    o_ref[0] = out


def _merge(final, x, pu, pz, mq, mz, ya, gates, mk, mv, pw, ps, wb, wo, fg):
    b, s, d = x.shape
    row = lambda w: pl.BlockSpec((1, TM, w), lambda bi, i: (bi, i, 0))
    full = lambda a: pl.BlockSpec(a.shape, lambda bi, i: (0,) * a.ndim)
    per_b = lambda a: pl.BlockSpec((1,) + a.shape[1:], lambda bi, i: (bi,) + (0,) * (a.ndim - 1))
    halo = pl.BlockSpec((1, HALO, POOL_WIDTH),
                        lambda bi, i: (bi, jnp.maximum(i * (TM // HALO) - 1, 0), 0))
    return pl.pallas_call(
        functools.partial(_merge_kernel, final),
        grid=(b, s // TM),
        in_specs=[row(d), row(POOL_WIDTH), halo, row(POOL_WIDTH), row(MEM_WIDTH), row(MEM_WIDTH),
                  row(ATTN_WIDTH), row(N_BRANCH * D_MODEL), per_b(mk), per_b(mv),
                  full(pw), full(ps), full(wb), full(wo), full(fg)],
        out_specs=row(d),
        out_shape=jax.ShapeDtypeStruct((b, s, d), F32),
        compiler_params=pltpu.CompilerParams(
            dimension_semantics=("arbitrary", "arbitrary"), vmem_limit_bytes=VMEM_LIMIT),
        name="merge_final" if final else "merge",
    )(x, pu, pu, pz, mq, mz, ya, gates, mk, mv, pw, ps, wb, wo, fg)


def _split_w_in(w):
    col = lambda lo, hi: w[:, lo:hi]
    wf = jnp.concatenate([col(O_PU, O_PZ), col(O_PZ, O_AQ), col(O_AZ, O_IQ), col(O_MZ, O_G),
                          col(O_G, O_END)], axis=1)
    wb = jnp.concatenate([col(O_AQ, O_AK) * (ATTN_SCALE * LOG2E), col(O_AK, O_AV), col(O_IQ, O_IK),
                          col(O_MQ, O_MZ), col(O_IK, O_IW), col(O_IK, O_IW)], axis=1)
    wt = jnp.concatenate([col(O_AV, O_AZ), col(O_IW, O_MQ),
                          jnp.zeros((w.shape[0], WT_ROWS - ATTN_WIDTH - IDX_HEADS), w.dtype)], axis=1).T
    return wf.astype(BF16), wb.astype(BF16), wt.astype(BF16)


def kernel(x, mem, norm_g, w_in, pool_w, pool_scale, mem_norm_g, w_mem_kv, w_branch, w_out,
           rel_bias, final_g):
    depth = w_in.shape[0]
    assert x.shape[1] % COUNT_ROWS == 0 and x.shape[2] == D_MODEL and mem.shape[1] == MEM_LEN
    assert w_in.shape[2] == O_END and TQ == KT and PTM % KT == 0
    assert x.shape[1] // TOP_ROWS <= 256
    bias = _bias_tiles(rel_bias)
    fg = final_g.reshape(1, D_MODEL)
    for l in range(depth):
        wf, wb, wt = _split_w_in(w_in[l])
        (pu, pz, az, mz, gates, aq, ak, iq, mq, ik2, avt, iwt) = _project(
            x, norm_g[l].reshape(1, D_MODEL), wf, wb, wt)
        mk, mv = _mem_kv(mem, mem_norm_g[l].reshape(1, D_MODEL), w_mem_kv[l].astype(BF16))
        ya = _dsa(aq, iq, iwt, az, ik2, ak, avt, bias)
        x = _merge(l == depth - 1, x, pu, pz, mq, mz, ya, gates, mk, mv,
                   pool_w[l].astype(BF16), pool_scale[l].reshape(1, POOL_WIDTH),
                   w_branch[l].astype(BF16), w_out[l].astype(BF16), fg)
    return x
```
